```python
import math
import jax
import jax.numpy as jnp
from jax import lax
import numpy as np


D_MODEL = 2048
BATCH = 4
SEQ = 4096
DEPTH = 1

MEM_LEN = 256
RMS_EPS = 1e-6

A_HEADS = 12
A_HEAD_DIM = 128
A_WIDTH = A_HEADS * A_HEAD_DIM
MOBA_BLOCK = 256
MOBA_TOPK = 3
MOBA_Q_CHUNK = 32

REL_BUCKETS = 32
REL_MAX_DIST = 128

B_HEAD_DIM = 64
B_WIDTH = 1536
B_HEADS = B_WIDTH // B_HEAD_DIM
DECAY_LORA = max(32, int(round(1.8 * math.sqrt(D_MODEL) / 32)) * 32)
AAA_LORA = DECAY_LORA
LNX_EPS = 64e-5

C_HEADS = 4
C_HEAD_DIM = 256
C_WIDTH = C_HEADS * C_HEAD_DIM

N_BRANCHES = 3
IN_SPLITS = (3 * A_WIDTH, A_WIDTH, 3 * B_WIDTH, B_WIDTH, DECAY_LORA, AAA_LORA, C_WIDTH, C_WIDTH, N_BRANCHES * D_MODEL)
IN_WIDTH = sum(IN_SPLITS)

kernel_name = 'hybrid_moba_rwkv7_mem_block'


def rms_norm(x, g):
    xf = x.astype(jnp.float32)
    y = xf * lax.rsqrt(jnp.mean(xf * xf, axis=-1, keepdims=True) + RMS_EPS)
    return (y * g).astype(x.dtype)


def split_heads(t, n_heads):
    b, s, c = t.shape
    return t.reshape(b, s, n_heads, c // n_heads).transpose(0, 2, 1, 3)


def merge_heads(t):
    b, h, s, d = t.shape
    return t.transpose(0, 2, 1, 3).reshape(b, s, h * d)


def t5_bucket(dist):
    n = jnp.maximum(dist, 0)
    max_exact = REL_BUCKETS // 2
    nf = jnp.maximum(n, max_exact).astype(jnp.float32)
    large = max_exact + (jnp.log(nf / max_exact) / math.log(REL_MAX_DIST / max_exact) * (REL_BUCKETS - max_exact)).astype(jnp.int32)
    large = jnp.minimum(large, REL_BUCKETS - 1)
    return jnp.where(n < max_exact, n, large)


def moba_attention(q, k, v, rel_bias):
    bsz, nh, t, dh = q.shape
    f32 = jnp.float32
    nb = -(-t // MOBA_BLOCK)
    tp = nb * MOBA_BLOCK
    pad = ((0, 0), (0, 0), (0, tp - t), (0, 0))
    q = jnp.pad(q, pad)
    k = jnp.pad(k, pad)
    v = jnp.pad(v, pad)
    kb = k.reshape(bsz, nh, nb, MOBA_BLOCK, dh)
    vb = v.reshape(bsz, nh, nb, MOBA_BLOCK, dh)
    k_mean = jnp.mean(kb.astype(f32), axis=3)
    qblk = jnp.arange(tp) // MOBA_BLOCK
    gate = jnp.einsum('bhtd,bhnd->bhtn', q.astype(f32), k_mean)
    fully_past = jnp.arange(nb)[None, :] < qblk[:, None]
    gate = jnp.where(fully_past, gate, -jnp.inf)
    n_sel = min(MOBA_TOPK, nb)
    _, sel = lax.top_k(gate, n_sel)
    sel_ok = sel < qblk[:, None]
    scale = dh ** -0.5
    bias_hb = rel_bias.T.astype(f32)
    head_ix = jnp.arange(nh)[None, :, None, None, None]
    gather_blocks = jax.vmap(jax.vmap(lambda blocks, ix: blocks[ix]))

    def chunk(ci):
        t0 = ci * MOBA_Q_CHUNK
        qc = lax.dynamic_slice_in_dim(q, t0, MOBA_Q_CHUNK, axis=2)
        sel_c = lax.dynamic_slice_in_dim(sel, t0, MOBA_Q_CHUNK, axis=2)
        ok_c = lax.dynamic_slice_in_dim(sel_ok, t0, MOBA_Q_CHUNK, axis=2)
        qpos = t0 + jnp.arange(MOBA_Q_CHUNK)
        k_sel = gather_blocks(kb, sel_c)
        v_sel = gather_blocks(vb, sel_c)
        s_sel = jnp.einsum('bhqd,bhqnpd->bhqnp', qc, k_sel).astype(f32) * scale
        kpos_sel = sel_c[..., None] * MOBA_BLOCK + jnp.arange(MOBA_BLOCK)
        dist_sel = qpos[None, None, :, None, None] - kpos_sel
        s_sel = s_sel + bias_hb[head_ix, t5_bucket(dist_sel)]
        s_sel = jnp.where(ok_c[..., None], s_sel, -jnp.inf)
        own0 = (t0 // MOBA_BLOCK) * MOBA_BLOCK
        k_own = lax.dynamic_slice_in_dim(k, own0, MOBA_BLOCK, axis=2)
        v_own = lax.dynamic_slice_in_dim(v, own0, MOBA_BLOCK, axis=2)
        s_own = jnp.einsum('bhqd,bhpd->bhqp', qc, k_own).astype(f32) * scale
        dist_own = qpos[:, None] - (own0 + jnp.arange(MOBA_BLOCK))[None, :]
        s_own = s_own + bias_hb[:, t5_bucket(dist_own)][None]
        s_own = jnp.where(dist_own >= 0, s_own, -jnp.inf)
        logits = jnp.concatenate([s_sel.reshape(bsz, nh, MOBA_Q_CHUNK, n_sel * MOBA_BLOCK), s_own], axis=-1)
        p = jax.nn.softmax(logits, axis=-1).astype(v.dtype)
        p_sel = p[..., :n_sel * MOBA_BLOCK].reshape(bsz, nh, MOBA_Q_CHUNK, n_sel, MOBA_BLOCK)
        p_own = p[..., n_sel * MOBA_BLOCK:]
        return jnp.einsum('bhqnp,bhqnpd->bhqd', p_sel, v_sel) + jnp.einsum('bhqp,bhpd->bhqd', p_own, v_own)

    out = lax.map(chunk, jnp.arange(tp // MOBA_Q_CHUNK))
    out = out.transpose(1, 2, 0, 3, 4).reshape(bsz, nh, tp, dh)
    return out[:, :, :t]


def token_shift_lerp(p, mu):
    prev = jnp.pad(p, ((0, 0), (1, 0), (0, 0)))[:, :-1]
    return p + (prev - p) * mu


def wkv7_scan(r, decay, k, v, a_vec, b_vec):
    bsz, t, nh, n = r.shape

    def step(state, inp):
        r_t, w_t, k_t, v_t, a_t, b_t = inp
        sa = jnp.einsum('bhvk,bhk->bhv', state, a_t)
        state = state * w_t[:, :, None, :] + sa[..., None] * b_t[:, :, None, :] + v_t[..., None] * k_t[:, :, None, :]
        return state, jnp.einsum('bhvk,bhk->bhv', state, r_t)

    xs = (jnp.moveaxis(r, 1, 0), jnp.moveaxis(decay, 1, 0), jnp.moveaxis(k, 1, 0), jnp.moveaxis(v, 1, 0), jnp.moveaxis(a_vec, 1, 0), jnp.moveaxis(b_vec, 1, 0))
    s0 = jnp.zeros((bsz, nh, n, n), jnp.float32)
    _, y = lax.scan(step, s0, xs)
    return jnp.moveaxis(y, 0, 1)


def rwkv7_time_mix(r, k, v, lw, la, mu_r, mu_k, mu_v, mu_w, mu_a, w0, w_decay2, a0, w_aaa2, k_k, k_a, r_k, lnx_w, lnx_b):
    bsz, t, _ = r.shape
    f32 = jnp.float32
    r = token_shift_lerp(r, mu_r)
    k = token_shift_lerp(k, mu_k)
    v = token_shift_lerp(v, mu_v)
    lw = token_shift_lerp(lw, mu_w)
    la = token_shift_lerp(la, mu_a)
    w_log = -jax.nn.softplus(-(w0 + jnp.tanh(lw) @ w_decay2).astype(f32)) - 0.5
    decay = jnp.exp(-jnp.exp(w_log))
    a = jax.nn.sigmoid((a0 + la @ w_aaa2).astype(f32))

    def heads(z):
        return z.astype(f32).reshape(bsz, t, B_HEADS, B_HEAD_DIM)

    kk = heads(k * k_k)
    kk = kk / jnp.maximum(jnp.sqrt(jnp.sum(kk * kk, axis=-1, keepdims=True)), 1e-12)
    k_mod = k.astype(f32) * (1.0 + (a - 1.0) * k_a)
    rh = heads(r)
    kh = heads(k_mod)
    vh = heads(v)
    ah = heads(a)
    y = wkv7_scan(rh, heads(decay), kh, vh, -kk, kk * ah)
    mean = jnp.mean(y, axis=-1, keepdims=True)
    var = jnp.mean(jnp.square(y - mean), axis=-1, keepdims=True)
    y = ((y - mean) * lax.rsqrt(var + LNX_EPS)).reshape(bsz, t, B_WIDTH) * lnx_w + lnx_b
    bonus = jnp.sum(rh * kh * r_k, axis=-1, keepdims=True) * vh
    return (y + bonus.reshape(bsz, t, B_WIDTH)).astype(r.dtype)


def memory_cross_attention(q, mem_n, w_mem_kv):
    k_m, v_m = jnp.split(mem_n @ w_mem_kv, 2, axis=-1)
    qh = split_heads(q, C_HEADS)
    kh = split_heads(k_m, C_HEADS)
    vh = split_heads(v_m, C_HEADS)
    s = jnp.einsum('bhtd,bhmd->bhtm', qh, kh).astype(jnp.float32) * (C_HEAD_DIM ** -0.5)
    p = jax.nn.softmax(s, axis=-1).astype(vh.dtype)
    return merge_heads(jnp.einsum('bhtm,bhmd->bhtd', p, vh))


def hybrid_layer(x, mem, rel_bias, norm_g, mem_norm_g, w_in, rw_mu_r, rw_mu_k, rw_mu_v, rw_mu_w, rw_mu_a, rw_w0, rw_w_decay2, rw_a0, rw_w_aaa2, rw_k_k, rw_k_a, rw_r_k, rw_lnx_w, rw_lnx_b, w_mem_kv, w_proj_a, w_proj_b, w_proj_c, w_out):
    h = rms_norm(x, norm_g)
    mem_n = rms_norm(mem, mem_norm_g)
    proj = h @ w_in
    cuts = [int(c) for c in np.cumsum(IN_SPLITS)[:-1]]
    qkv_a, z_a, rkv_b, z_b, lw, la, q_c, z_c, gates = jnp.split(proj, cuts, axis=-1)
    q_a, k_a, v_a = jnp.split(qkv_a, 3, axis=-1)
    y_a = merge_heads(moba_attention(split_heads(q_a, A_HEADS), split_heads(k_a, A_HEADS), split_heads(v_a, A_HEADS), rel_bias))
    y_a = y_a * jax.nn.silu(z_a)
    r_b, k_b, v_b = jnp.split(rkv_b, 3, axis=-1)
    y_b = rwkv7_time_mix(r_b, k_b, v_b, lw, la, rw_mu_r, rw_mu_k, rw_mu_v, rw_mu_w, rw_mu_a, rw_w0, rw_w_decay2, rw_a0, rw_w_aaa2, rw_k_k, rw_k_a, rw_r_k, rw_lnx_w, rw_lnx_b)
    y_b = y_b * jax.nn.silu(z_b)
    y_c = memory_cross_attention(q_c, mem_n, w_mem_kv) * jax.nn.silu(z_c)
    g_a, g_b, g_c = jnp.split(jax.nn.sigmoid(gates), N_BRANCHES, axis=-1)
    merged = g_a * (y_a @ w_proj_a) + g_b * (y_b @ w_proj_b) + g_c * (y_c @ w_proj_c)
    return x + merged @ w_out


def setup_inputs(seed: int = 0) -> dict:
    key = jax.random.key(seed)
    ks = jax.random.split(key, 32)
    L = DEPTH
    nrm = jax.random.normal
    f32 = jnp.float32
    return {
        'x': nrm(ks[0], (BATCH, SEQ, D_MODEL), f32),
        'mem': nrm(ks[1], (BATCH, MEM_LEN, D_MODEL), f32),
        'rel_bias': 0.2 * nrm(ks[2], (REL_BUCKETS, A_HEADS), f32),
        'norm_g': 1.0 + 0.02 * nrm(ks[3], (L, D_MODEL), f32),
        'mem_norm_g': 1.0 + 0.02 * nrm(ks[4], (L, D_MODEL), f32),
        'w_in': nrm(ks[5], (L, D_MODEL, IN_WIDTH), f32) * D_MODEL ** -0.5,
        'rw_mu_r': jax.random.uniform(ks[6], (L, B_WIDTH), f32),
        'rw_mu_k': jax.random.uniform(ks[7], (L, B_WIDTH), f32),
        'rw_mu_v': jax.random.uniform(ks[8], (L, B_WIDTH), f32),
        'rw_mu_w': jax.random.uniform(ks[9], (L, DECAY_LORA), f32),
        'rw_mu_a': jax.random.uniform(ks[10], (L, AAA_LORA), f32),
        'rw_w0': jax.random.uniform(ks[11], (L, B_WIDTH), f32, -5.0, 0.5),
        'rw_w_decay2': 0.1 * nrm(ks[12], (L, DECAY_LORA, B_WIDTH), f32) * DECAY_LORA ** -0.5,
        'rw_a0': 0.1 * nrm(ks[13], (L, B_WIDTH), f32),
        'rw_w_aaa2': 0.5 * nrm(ks[14], (L, AAA_LORA, B_WIDTH), f32) * AAA_LORA ** -0.5,
        'rw_k_k': 0.85 + 0.05 * nrm(ks[15], (L, B_WIDTH), f32),
        'rw_k_a': 1.0 + 0.05 * nrm(ks[16], (L, B_WIDTH), f32),
        'rw_r_k': 0.1 * nrm(ks[17], (L, B_HEADS, B_HEAD_DIM), f32),
        'rw_lnx_w': 1.0 + 0.02 * nrm(ks[18], (L, B_WIDTH), f32),
        'rw_lnx_b': 0.02 * nrm(ks[19], (L, B_WIDTH), f32),
        'w_mem_kv': nrm(ks[20], (L, D_MODEL, 2 * C_WIDTH), f32) * D_MODEL ** -0.5,
        'w_proj_a': nrm(ks[21], (L, A_WIDTH, D_MODEL), f32) * A_WIDTH ** -0.5,
        'w_proj_b': nrm(ks[22], (L, B_WIDTH, D_MODEL), f32) * B_WIDTH ** -0.5,
        'w_proj_c': nrm(ks[23], (L, C_WIDTH, D_MODEL), f32) * C_WIDTH ** -0.5,
        'w_out': nrm(ks[24], (L, D_MODEL, D_MODEL), f32) * D_MODEL ** -0.5,
        'final_norm_g': 1.0 + 0.02 * nrm(ks[25], (D_MODEL,), f32),
    }


def reference(x, mem, rel_bias, norm_g, mem_norm_g, w_in, rw_mu_r, rw_mu_k, rw_mu_v, rw_mu_w, rw_mu_a, rw_w0, rw_w_decay2, rw_a0, rw_w_aaa2, rw_k_k, rw_k_a, rw_r_k, rw_lnx_w, rw_lnx_b, w_mem_kv, w_proj_a, w_proj_b, w_proj_c, w_out, final_norm_g):
    for layer in range(DEPTH):
        x = hybrid_layer(x, mem, rel_bias, norm_g[layer], mem_norm_g[layer], w_in[layer], rw_mu_r[layer], rw_mu_k[layer], rw_mu_v[layer], rw_mu_w[layer], rw_mu_a[layer], rw_w0[layer], rw_w_decay2[layer], rw_a0[layer], rw_w_aaa2[layer], rw_k_k[layer], rw_k_a[layer], rw_r_k[layer], rw_lnx_w[layer], rw_lnx_b[layer], w_mem_kv[layer], w_proj_a[layer], w_proj_b[layer], w_proj_c[layer], w_out[layer])
    return rms_norm(x, final_norm_g)
```

```python
import functools
import math

import jax
import jax.numpy as jnp
from jax import lax
from jax.experimental import pallas as pl
from jax.experimental.pallas import tpu as pltpu

F32 = jnp.float32
BF16 = jnp.bfloat16

RMS_EPS = 1e-6

A_HEADS = 12
A_HEAD_DIM = 128
A_WIDTH = A_HEADS * A_HEAD_DIM
MOBA_BLOCK = 256
MOBA_TOPK = 3
REL_BUCKETS = 32
REL_MAX_DIST = 128

B_HEAD_DIM = 64
B_WIDTH = 1536
LORA = 96
LORA_PAD = 128
LNX_EPS = 64e-5
WKV_CHUNK = 64
WKV_GROUP = 256
WKV_HEADS_PER_GROUP = WKV_GROUP // B_HEAD_DIM

C_HEADS = 4
C_HEAD_DIM = 256
C_WIDTH = C_HEADS * C_HEAD_DIM

N_BRANCHES = 3

VMEM_LIMIT = 48 * 1024 * 1024

NN = (((1,), (0,)), ((), ()))
NT = (((1,), (1,)), ((), ()))
TN = (((0,), (0,)), ((), ()))

MASKED = -1e30


def _params(n_axes):
    return pltpu.CompilerParams(dimension_semantics=("arbitrary",) * n_axes, vmem_limit_bytes=VMEM_LIMIT)


def _silu(z):
    return z * jax.nn.sigmoid(z)


def _rmsnorm_kernel(x_ref, g_ref, o_ref):
    x = x_ref[...].astype(F32)
    y = x * lax.rsqrt(jnp.mean(x * x, axis=-1, keepdims=True) + RMS_EPS)
    o_ref[...] = (y * g_ref[...]).astype(o_ref.dtype)


def _rmsnorm(x2d, g, out_dtype, tm):
    m, d = x2d.shape
    return pl.pallas_call(
        _rmsnorm_kernel,
        grid=(m // tm,),
        in_specs=[pl.BlockSpec((tm, d), lambda i: (i, 0)), pl.BlockSpec((1, d), lambda i: (0, 0))],
        out_specs=pl.BlockSpec((tm, d), lambda i: (i, 0)),
        out_shape=jax.ShapeDtypeStruct((m, d), out_dtype),
        compiler_params=_params(1),
        name="rmsnorm",
    )(x2d, g.reshape(1, d))


def _matmul_kernel(a_ref, w_ref, o_ref):
    o_ref[...] = jnp.dot(a_ref[...], w_ref[...], preferred_element_type=F32).astype(o_ref.dtype)


def _matmul(a, w, out_dtype, tm, tn, name):
    m, k = a.shape
    n = w.shape[1]
    return pl.pallas_call(
        _matmul_kernel,
        grid=(n // tn, m // tm),
        in_specs=[pl.BlockSpec((tm, k), lambda j, i: (i, 0)), pl.BlockSpec((k, tn), lambda j, i: (0, j))],
        out_specs=pl.BlockSpec((tm, tn), lambda j, i: (i, j)),
        out_shape=jax.ShapeDtypeStruct((m, n), out_dtype),
        compiler_params=_params(2),
        name=name,
    )(a, w)


def _t5_bucket(dist):
    n = jnp.maximum(dist, 0)
    max_exact = REL_BUCKETS // 2
    nf = jnp.maximum(n, max_exact).astype(F32)
    large = max_exact + (jnp.log(nf / max_exact) / math.log(REL_MAX_DIST / max_exact) * (REL_BUCKETS - max_exact)).astype(jnp.int32)
    large = jnp.minimum(large, REL_BUCKETS - 1)
    return jnp.where(n < max_exact, n, large)


def _bias_kernel(rel_ref, bucket_ref, o_ref):
    h = pl.program_id(0)
    bucket = bucket_ref[...]
    acc = jnp.zeros(bucket.shape, F32)
    for b in range(REL_BUCKETS):
        acc = jnp.where(bucket == b, rel_ref[b, h], acc)
    o_ref[0] = acc


def _bias_tiles(rel_bias):
    blk = MOBA_BLOCK
    qpos = lax.broadcasted_iota(jnp.int32, (blk, 2 * blk), 0) + blk
    kpos = lax.broadcasted_iota(jnp.int32, (blk, 2 * blk), 1)
    bucket = _t5_bucket(qpos - kpos)
    return pl.pallas_call(
        _bias_kernel,
        grid=(A_HEADS,),
        in_specs=[pl.BlockSpec(memory_space=pltpu.SMEM), pl.BlockSpec((blk, 2 * blk), lambda h: (0, 0))],
        out_specs=pl.BlockSpec((1, blk, 2 * blk), lambda h: (h, 0, 0)),
        out_shape=jax.ShapeDtypeStruct((A_HEADS, blk, 2 * blk), F32),
        compiler_params=_params(1),
        name="t5_bias",
    )(rel_bias, bucket)


def _moba_kernel(rel_ref, q_ref, k_ref, v_ref, z_ref, bias_ref, o_ref, kmean_ref):
    h = pl.program_id(1)
    qi = pl.program_id(2)
    blk = MOBA_BLOCK
    nb = k_ref.shape[0] // blk

    @pl.when(qi == 0)
    def _():
        kmean_ref[...] = jnp.zeros_like(kmean_ref)
        for j in range(nb):
            kj = k_ref[j * blk:(j + 1) * blk, :].astype(F32)
            kmean_ref[j:j + 1, :] = jnp.sum(kj, axis=0, keepdims=True) * (1.0 / blk)

    q = q_ref[...]
    km = kmean_ref[...]
    km_hi = km.astype(BF16)
    km_lo = (km - km_hi.astype(F32)).astype(BF16)
    gate = (lax.dot_general(q, km_hi, NT, preferred_element_type=F32)
            + lax.dot_general(q, km_lo, NT, preferred_element_type=F32))
    lane = lax.broadcasted_iota(jnp.int32, gate.shape, 1)
    lane_f = lane.astype(F32)
    g = jnp.where(lane < qi, gate, -jnp.inf)
    sel = jnp.zeros(gate.shape, F32)
    for _ in range(MOBA_TOPK):
        gmax = jnp.max(g, axis=-1, keepdims=True)
        first = jnp.min(jnp.where(g == gmax, lane_f, float(gate.shape[1])), axis=-1, keepdims=True)
        pick = (lane_f == first) & (gmax > -jnp.inf)
        sel = jnp.where(pick, 1.0, sel)
        g = jnp.where(pick, -jnp.inf, g)

    scale = A_HEAD_DIM ** -0.5
    row = lax.broadcasted_iota(jnp.int32, (blk, blk), 0)
    col = lax.broadcasted_iota(jnp.int32, (blk, blk), 1)

    own = pl.multiple_of(qi * blk, blk)
    s = lax.dot_general(q, k_ref[pl.ds(own, blk), :], NT, preferred_element_type=F32) * scale
    s = s + bias_ref[0, :, blk:2 * blk]
    s = jnp.where(col <= row, s, MASKED)
    m0 = jnp.max(s, axis=-1, keepdims=True)
    p = jnp.exp(s - m0)
    l0 = jnp.sum(p, axis=-1, keepdims=True)
    acc0 = jnp.dot(p.astype(BF16), v_ref[pl.ds(own, blk), :], preferred_element_type=F32)

    bias_far = rel_ref[REL_BUCKETS - 1, h]

    def body(j, carry):
        m, l, acc = carry
        off = pl.multiple_of(j * blk, blk)
        s = lax.dot_general(q, k_ref[pl.ds(off, blk), :], NT, preferred_element_type=F32) * scale
        s = s + jnp.where(j == qi - 1, bias_ref[0, :, 0:blk], bias_far)
        chosen = jnp.max(jnp.where(lane == j, sel, 0.0), axis=-1, keepdims=True)
        s = jnp.where(chosen > 0.0, s, MASKED)
        m_new = jnp.maximum(m, jnp.max(s, axis=-1, keepdims=True))
        alpha = jnp.exp(m - m_new)
        p = jnp.exp(s - m_new)
        l = alpha * l + jnp.sum(p, axis=-1, keepdims=True)
        acc = alpha * acc + jnp.dot(p.astype(BF16), v_ref[pl.ds(off, blk), :], preferred_element_type=F32)
        return m_new, l, acc

    _, l, acc = lax.fori_loop(0, qi, body, (m0, l0, acc0))
    y = acc / l
    o_ref[...] = (y * _silu(z_ref[...].astype(F32))).astype(o_ref.dtype)


def _moba(p1, rel_bias, bias_tiles, bsz, seq, col0):
    blk = MOBA_BLOCK
    nq = seq // blk
    hd = A_HEAD_DIM
    c = col0 // hd
    return pl.pallas_call(
        _moba_kernel,
        grid=(bsz, A_HEADS, nq),
        in_specs=[
            pl.BlockSpec(memory_space=pltpu.SMEM),
            pl.BlockSpec((blk, hd), lambda b, h, i: (b * nq + i, c + h)),
            pl.BlockSpec((seq, hd), lambda b, h, i: (b, c + A_HEADS + h)),
            pl.BlockSpec((seq, hd), lambda b, h, i: (b, c + 2 * A_HEADS + h)),
            pl.BlockSpec((blk, hd), lambda b, h, i: (b * nq + i, c + 3 * A_HEADS + h)),
            pl.BlockSpec((1, blk, 2 * blk), lambda b, h, i: (h, 0, 0)),
        ],
        out_specs=pl.BlockSpec((blk, hd), lambda b, h, i: (b * nq + i, h)),
        out_shape=jax.ShapeDtypeStruct((bsz * seq, A_WIDTH), BF16),
        scratch_shapes=[pltpu.VMEM((128, hd), F32)],
        compiler_params=_params(3),
        name="moba_attention",
    )(rel_bias, p1, p1, p1, p1, bias_tiles)


def _cross_kernel(q_ref, k_ref, v_ref, z_ref, o_ref):
    s = lax.dot_general(q_ref[...], k_ref[...], NT, preferred_element_type=F32) * (C_HEAD_DIM ** -0.5)
    m = jnp.max(s, axis=-1, keepdims=True)
    p = jnp.exp(s - m)
    l = jnp.sum(p, axis=-1, keepdims=True)
    y = jnp.dot(p.astype(BF16), v_ref[...], preferred_element_type=F32) / l
    o_ref[...] = (y * _silu(z_ref[...].astype(F32))).astype(o_ref.dtype)


def _cross(p1, kv, bsz, seq, mem_len, col0, tq):
    hd = C_HEAD_DIM
    c = col0 // hd
    nt = seq // tq
    return pl.pallas_call(
        _cross_kernel,
        grid=(bsz, C_HEADS, nt),
        in_specs=[
            pl.BlockSpec((tq, hd), lambda b, h, i: (b * nt + i, c + h)),
            pl.BlockSpec((mem_len, hd), lambda b, h, i: (b, h)),
            pl.BlockSpec((mem_len, hd), lambda b, h, i: (b, C_HEADS + h)),
            pl.BlockSpec((tq, hd), lambda b, h, i: (b * nt + i, c + C_HEADS + h)),
        ],
        out_specs=pl.BlockSpec((tq, hd), lambda b, h, i: (b * nt + i, h)),
        out_shape=jax.ShapeDtypeStruct((bsz * seq, C_WIDTH), BF16),
        compiler_params=_params(3),
        name="memory_attention",
    )(p1, kv, kv, p1)


def _split2(x):
    hi = x.astype(BF16)
    lo = (x - hi.astype(F32)).astype(BF16)
    return hi, lo


def _split3(x):
    hi = x.astype(BF16)
    r1 = x - hi.astype(F32)
    mid = r1.astype(BF16)
    lo = (r1 - mid.astype(F32)).astype(BF16)
    return hi, mid, lo


def _mm3(a, b, dn):
    f = lambda u, w: lax.dot_general(u, w, dn, preferred_element_type=F32)
    return f(a[0], b[0]) + f(a[0], b[1]) + f(a[1], b[0])


def _stack(a, b):
    return tuple(jnp.concatenate([u, w], axis=0) for u, w in zip(a, b))


class _WkvConsts:
    def __init__(self):
        c, g, hd = WKV_CHUNK, WKV_GROUP, B_HEAD_DIM
        row = lax.broadcasted_iota(jnp.int32, (g, g), 0)
        col = lax.broadcasted_iota(jnp.int32, (g, g), 1)
        self.same_head = (row // hd) == (col // hd)
        self.ones_bd = jnp.where(self.same_head, 1.0, 0.0).astype(BF16)
        t = lax.broadcasted_iota(jnp.int32, (c, g), 0)
        s = lax.broadcasted_iota(jnp.int32, (c, g), 1) % hd
        self.strict = s < t
        self.incl = s <= t
        self.eye = jnp.where(s == t, 1.0, 0.0).astype(F32)
        tr = lax.broadcasted_iota(jnp.int32, (c, c), 0)
        tc = lax.broadcasted_iota(jnp.int32, (c, c), 1)
        self.lower = jnp.where(tc <= tr, 1.0, 0.0).astype(BF16)
        self.row0 = lax.broadcasted_iota(jnp.int32, (c, g), 0) == 0

    def bd(self, pieces):
        n = WKV_HEADS_PER_GROUP
        return tuple(jnp.where(self.same_head, jnp.concatenate([p] * n, axis=0), jnp.zeros((), p.dtype)) for p in pieces)

    def segsum(self, x):
        return sum(jnp.dot(p, self.ones_bd, preferred_element_type=F32) for p in _split3(x))


def _wkv_chunk(rm, kmod, vm, avec, bvec, logdec, state, k):
    c = WKV_CHUNK
    cl = sum(jnp.dot(k.lower, p, preferred_element_type=F32) for p in _split3(logdec))
    cl_last = cl[c - 1:c, :]
    rt = _split2(rm * jnp.exp(cl))
    at = _split2(avec * jnp.exp(cl - logdec))
    e_neg = jnp.exp(-cl)
    bt = k.bd(_split2(bvec * e_neg))
    kt = k.bd(_split2(kmod * e_neg))
    e_rem = jnp.exp(cl_last - cl)
    bh = _split2(bvec * e_rem)
    kh = _split2(kmod * e_rem)

    lhs = _stack(at, rt)
    ab = _mm3(lhs, bt, NT)
    ak = _mm3(lhs, kt, NT)
    a_ab = jnp.where(k.strict, ab[:c], 0.0)
    a_rb = jnp.where(k.incl, ab[c:], 0.0)
    a_ak = jnp.where(k.strict, ak[:c], 0.0)
    a_rk = jnp.where(k.incl, ak[c:], 0.0)

    inv = k.eye + a_ab
    pw = _split2(a_ab)
    pw = _split2(_mm3(pw, k.bd(pw), NN))
    n_sq = int(math.log2(c)) - 2
    for _ in range(n_sq):
        tp = _mm3(_stack(_split2(inv), pw), k.bd(pw), NN)
        inv = inv + tp[:c]
        pw = _split2(tp[c:])
    inv = inv + _mm3(_split2(inv), k.bd(pw), NN)
    inv2 = _split2(inv)

    av = _mm3(_stack(_split2(a_ak), _split2(a_rk)), k.bd(_split2(vm)), NN)
    w = _mm3(inv2, k.bd(at), NN)
    u0 = _mm3(inv2, k.bd(_split2(av[:c])), NN)

    uy = _mm3(_stack(_split2(w), rt), _split2(state), NT)
    u = uy[:c] + u0
    u2 = _split2(u)
    y = uy[c:] + _mm3(_split2(a_rb), k.bd(u2), NN) + av[c:]
    upd = _mm3(_stack(u2, _split2(vm)), _stack(bh, kh), TN)
    new_state = state * jnp.exp(cl_last) + jnp.where(k.same_head, upd, 0.0)
    return y, new_state


def _rwkv_kernel(r_ref, k_ref, v_ref, lwla_ref, z_ref, mu_r_ref, mu_k_ref, mu_v_ref, mu_wa_ref, w0_ref, a0_ref,
                 kk_ref, ka_ref, rk_ref, lnw_ref, lnb_ref, wd_ref, wa_ref, o_ref, state_ref, prev_ref):
    c = WKV_CHUNK

    @pl.when(pl.program_id(2) == 0)
    def _():
        state_ref[...] = jnp.zeros_like(state_ref)
        prev_ref[...] = jnp.zeros_like(prev_ref)

    k = _WkvConsts()
    wd = _split2(wd_ref[...])
    wa = _split2(wa_ref[...])
    inv_hd = 1.0 / B_HEAD_DIM

    def mix(x, slot, mu_ref):
        prev = jnp.where(k.row0, prev_ref[slot:slot + 1, :], pltpu.roll(x, 1, 0))
        prev_ref[slot:slot + 1, :] = x[c - 1:c, :]
        return x + (prev - x) * mu_ref[...]

    def chunk(ci, carry):
        rows = pl.ds(pl.multiple_of(ci * c, c), c)
        rm = mix(r_ref[rows, :], 0, mu_r_ref)
        km = mix(k_ref[rows, :], 8, mu_k_ref)
        vm = mix(v_ref[rows, :], 16, mu_v_ref)
        lwla = mix(lwla_ref[rows, :], 24, mu_wa_ref)
        lw = _split2(jnp.tanh(lwla[:, :LORA_PAD]))
        la = _split2(lwla[:, LORA_PAD:])
        logdec = -math.exp(-0.5) * jax.nn.sigmoid(w0_ref[...] + _mm3(lw, wd, NN))
        a_lr = jax.nn.sigmoid(a0_ref[...] + _mm3(la, wa, NN))
        kk = km * kk_ref[...]
        kk = kk / jnp.maximum(jnp.sqrt(k.segsum(kk * kk)), 1e-12)
        kmod = km * (1.0 + (a_lr - 1.0) * ka_ref[...])
        y, new_state = _wkv_chunk(rm, kmod, vm, -kk, kk * a_lr, logdec, state_ref[...], k)
        state_ref[...] = new_state
        mean = k.segsum(y) * inv_hd
        yc = y - mean
        var = k.segsum(yc * yc) * inv_hd
        yn = yc * lax.rsqrt(var + LNX_EPS) * lnw_ref[...] + lnb_ref[...]
        bonus = k.segsum(rm * kmod * rk_ref[...]) * vm
        out = (yn + bonus) * _silu(z_ref[rows, :].astype(F32))
        o_ref[rows, :] = out.astype(o_ref.dtype)
        return carry

    lax.fori_loop(0, r_ref.shape[0] // c, chunk, 0)


def _rwkv(p2, p1, vecs, wd2, wa2, bsz, seq, z_col0, tt):
    g = WKV_GROUP
    ng = B_WIDTH // g
    nt = seq // tt
    zc = z_col0 // g
    row = lambda b, gi, t: b * nt + t
    vec_spec = pl.BlockSpec((1, g), lambda b, gi, t: (0, gi))
    lora_vec_spec = pl.BlockSpec((1, g), lambda b, gi, t: (0, 0))
    lora_w_spec = pl.BlockSpec((LORA_PAD, g), lambda b, gi, t: (0, gi))
    mu_r, mu_k, mu_v, mu_wa, w0, a0, k_k, k_a, r_k, lnw, lnb = vecs
    return pl.pallas_call(
        _rwkv_kernel,
        grid=(bsz, ng, nt),
        in_specs=[
            pl.BlockSpec((tt, g), lambda b, gi, t: (row(b, gi, t), gi)),
            pl.BlockSpec((tt, g), lambda b, gi, t: (row(b, gi, t), ng + gi)),
            pl.BlockSpec((tt, g), lambda b, gi, t: (row(b, gi, t), 2 * ng + gi)),
            pl.BlockSpec((tt, g), lambda b, gi, t: (row(b, gi, t), 3 * ng)),
            pl.BlockSpec((tt, g), lambda b, gi, t: (row(b, gi, t), zc + gi)),
            vec_spec, vec_spec, vec_spec, lora_vec_spec, vec_spec, vec_spec, vec_spec, vec_spec, vec_spec, vec_spec, vec_spec,
            lora_w_spec, lora_w_spec,
        ],
        out_specs=pl.BlockSpec((tt, g), lambda b, gi, t: (row(b, gi, t), gi)),
        out_shape=jax.ShapeDtypeStruct((bsz * seq, B_WIDTH), BF16),
        scratch_shapes=[pltpu.VMEM((g, g), F32), pltpu.VMEM((32, g), F32)],
        compiler_params=_params(3),
        name="rwkv7_time_mix",
    )(p2, p2, p2, p2, p1, mu_r, mu_k, mu_v, mu_wa, w0, a0, k_k, k_a, r_k, lnw, lnb, wd2, wa2)


def _merge_kernel(ya_ref, yb_ref, yc_ref, wa_ref, wb_ref, wc_ref, ga_ref, gb_ref, gc_ref, o_ref):
    def branch(y_ref, w_ref, g_ref):
        return jax.nn.sigmoid(g_ref[...].astype(F32)) * jnp.dot(y_ref[...], w_ref[...], preferred_element_type=F32)

    o_ref[...] = (branch(ya_ref, wa_ref, ga_ref) + branch(yb_ref, wb_ref, gb_ref) + branch(yc_ref, wc_ref, gc_ref)).astype(o_ref.dtype)


def _merge(ya, yb, yc, wa, wb, wc, p1, d_model, tm, tn):
    m = ya.shape[0]
    nj = d_model // tn
    y_spec = lambda width: pl.BlockSpec((tm, width), lambda j, i: (i, 0))
    w_spec = lambda width: pl.BlockSpec((width, tn), lambda j, i: (0, j))
    g_spec = lambda br: pl.BlockSpec((tm, tn), lambda j, i: (i, br * nj + j))
    return pl.pallas_call(
        _merge_kernel,
        grid=(nj, m // tm),
        in_specs=[y_spec(A_WIDTH), y_spec(B_WIDTH), y_spec(C_WIDTH), w_spec(A_WIDTH), w_spec(B_WIDTH), w_spec(C_WIDTH),
                  g_spec(0), g_spec(1), g_spec(2)],
        out_specs=pl.BlockSpec((tm, tn), lambda j, i: (i, j)),
        out_shape=jax.ShapeDtypeStruct((m, d_model), BF16),
        compiler_params=_params(2),
        name="gated_merge",
    )(ya, yb, yc, wa, wb, wc, p1, p1, p1)


def _out_kernel(m_ref, w_ref, x_ref, g_ref, o_ref, *, final_norm):
    y = x_ref[...] + jnp.dot(m_ref[...], w_ref[...], preferred_element_type=F32)
    if final_norm:
        y = y * lax.rsqrt(jnp.mean(y * y, axis=-1, keepdims=True) + RMS_EPS) * g_ref[...]
    o_ref[...] = y


def _out_proj(merged, w_out, x2d, g, final_norm, tm):
    m, d = x2d.shape
    return pl.pallas_call(
        functools.partial(_out_kernel, final_norm=final_norm),
        grid=(m // tm,),
        in_specs=[pl.BlockSpec((tm, d), lambda i: (i, 0)), pl.BlockSpec((d, d), lambda i: (0, 0)),
                  pl.BlockSpec((tm, d), lambda i: (i, 0)), pl.BlockSpec((1, d), lambda i: (0, 0))],
        out_specs=pl.BlockSpec((tm, d), lambda i: (i, 0)),
        out_shape=jax.ShapeDtypeStruct((m, d), F32),
        compiler_params=_params(1),
        name="out_proj",
    )(merged, w_out, x2d, g.reshape(1, d))


def _pick_tile(n, candidates):
    for t in candidates:
        if n % t == 0:
            return t
    raise ValueError(f"no tile for extent {n}")


def _pad_cols(w, width):
    return jnp.pad(w, ((0, 0), (0, width - w.shape[1])))


def _layer(x2d, mem2d, bsz, seq, mem_len, bias_tiles, rel_bias, norm_g, mem_norm_g, w_in, rw, w_mem_kv, w_proj_a, w_proj_b,
           w_proj_c, w_out, final_g):
    m, d = x2d.shape
    tm = _pick_tile(m, (1024, 512, 256))

    c_qkv_a = 0
    c_z_a = c_qkv_a + 3 * A_WIDTH
    c_rkv_b = c_z_a + A_WIDTH
    c_z_b = c_rkv_b + 3 * B_WIDTH
    c_lw = c_z_b + B_WIDTH
    c_la = c_lw + LORA
    c_q_c = c_la + LORA
    c_gates = c_q_c + 2 * C_WIDTH
    c_end = c_gates + N_BRANCHES * d

    w1 = jnp.concatenate([w_in[:, c_gates:c_end], w_in[:, c_qkv_a:c_rkv_b], w_in[:, c_z_b:c_lw], w_in[:, c_q_c:c_gates]], axis=1)
    p1_a = N_BRANCHES * d
    p1_zb = p1_a + 4 * A_WIDTH
    p1_c = p1_zb + B_WIDTH
    tn1 = 1024
    w1 = _pad_cols(w1, -(-w1.shape[1] // tn1) * tn1).astype(BF16)
    w2 = jnp.concatenate([w_in[:, c_rkv_b:c_z_b], _pad_cols(w_in[:, c_lw:c_la], LORA_PAD), _pad_cols(w_in[:, c_la:c_q_c], LORA_PAD)], axis=1)
    tn2 = 1024
    w2 = _pad_cols(w2, -(-w2.shape[1] // tn2) * tn2).astype(BF16)

    h = _rmsnorm(x2d, norm_g, BF16, tm)
    p1 = _matmul(h, w1, BF16, tm, tn1, "in_proj_bf16")
    p2 = _matmul(h, w2, F32, tm, tn2, "in_proj_f32")

    mem_n = _rmsnorm(mem2d, mem_norm_g, BF16, _pick_tile(mem2d.shape[0], (1024, 512, 256)))
    kv = _matmul(mem_n, w_mem_kv.astype(BF16), BF16, _pick_tile(mem2d.shape[0], (1024, 512, 256)), 1024, "mem_kv")

    ya = _moba(p1, rel_bias, bias_tiles, bsz, seq, p1_a)
    yc = _cross(p1, kv, bsz, seq, mem_len, p1_c, _pick_tile(seq, (1024, 512, 256)))

    (mu_r, mu_k, mu_v, mu_w, mu_a, w0, w_decay2, a0, w_aaa2, k_k, k_a, r_k, lnx_w, lnx_b) = rw
    vec = lambda v: v.reshape(1, B_WIDTH)
    lora_vec = lambda v: _pad_cols(v.reshape(1, LORA), LORA_PAD)
    mu_wa = jnp.concatenate([lora_vec(mu_w), lora_vec(mu_a)], axis=1)
    lora_w = lambda w: jnp.pad(w, ((0, LORA_PAD - LORA), (0, 0)))
    vecs = (vec(mu_r), vec(mu_k), vec(mu_v), mu_wa, vec(w0), vec(a0), vec(k_k), vec(k_a), vec(r_k), vec(lnx_w), vec(lnx_b))
    yb = _rwkv(p2, p1, vecs, lora_w(w_decay2), lora_w(w_aaa2), bsz, seq, p1_zb, _pick_tile(seq, (512, 256, 128, 64)))

    merged = _merge(ya, yb, yc, w_proj_a.astype(BF16), w_proj_b.astype(BF16), w_proj_c.astype(BF16), p1, d,
                    _pick_tile(m, (512, 256)), 1024)
    g = final_g if final_g is not None else jnp.ones((d,), F32)
    return _out_proj(merged, w_out.astype(BF16), x2d, g, final_g is not None, _pick_tile(m, (512, 256)))


def kernel(x, mem, rel_bias, norm_g, mem_norm_g, w_in, rw_mu_r, rw_mu_k, rw_mu_v, rw_mu_w, rw_mu_a, rw_w0, rw_w_decay2, rw_a0, rw_w_aaa2, rw_k_k, rw_k_a, rw_r_k, rw_lnx_w, rw_lnx_b, w_mem_kv, w_proj_a, w_proj_b, w_proj_c, w_out, final_norm_g):
    bsz, seq, d = x.shape
    mem_len = mem.shape[1]
    depth = norm_g.shape[0]
    x2d = x.reshape(bsz * seq, d)
    mem2d = mem.reshape(bsz * mem_len, d)
    bias_tiles = _bias_tiles(rel_bias)
    for l in range(depth):
        rw = (rw_mu_r[l], rw_mu_k[l], rw_mu_v[l], rw_mu_w[l], rw_mu_a[l], rw_w0[l], rw_w_decay2[l], rw_a0[l], rw_w_aaa2[l],
              rw_k_k[l], rw_k_a[l], rw_r_k[l], rw_lnx_w[l], rw_lnx_b[l])
        x2d = _layer(x2d, mem2d, bsz, seq, mem_len, bias_tiles, rel_bias, norm_g[l], mem_norm_g[l], w_in[l], rw, w_mem_kv[l],
                     w_proj_a[l], w_proj_b[l], w_proj_c[l], w_out[l], final_norm_g if l == depth - 1 else None)
    return x2d.reshape(bsz, seq, d)
```

```python
import functools
import math

import jax
import jax.numpy as jnp
from jax import lax
from jax.experimental import pallas as pl
from jax.experimental.pallas import tpu as pltpu

F32 = jnp.float32
BF16 = jnp.bfloat16

RMS_EPS = 1e-6

A_HEADS = 12
A_HEAD_DIM = 128
A_WIDTH = A_HEADS * A_HEAD_DIM
MOBA_BLOCK = 256
MOBA_TOPK = 3
MOBA_HEADS_PER_STEP = 2
MOBA_FAR_GROUP = 4
REL_BUCKETS = 32
REL_MAX_DIST = 128

B_HEAD_DIM = 64
B_WIDTH = 1536
LORA = 96
LORA_PAD = 128
LNX_EPS = 64e-5
WKV_CHUNK = 64
WKV_GROUP = 256
WKV_HEADS_PER_GROUP = WKV_GROUP // B_HEAD_DIM
WKV_GROUPS_PER_STEP = 6

C_HEADS = 4
C_HEAD_DIM = 256
C_WIDTH = C_HEADS * C_HEAD_DIM

N_BRANCHES = 3

VMEM_LIMIT = 48 * 1024 * 1024

NN = (((1,), (0,)), ((), ()))
NT = (((1,), (1,)), ((), ()))
TN = (((0,), (0,)), ((), ()))

MASKED = -1e30
LOG2E = math.log2(math.e)


def _params(n_axes):
    return pltpu.CompilerParams(dimension_semantics=("arbitrary",) * n_axes, vmem_limit_bytes=VMEM_LIMIT)


def _silu(z):
    return z * jax.nn.sigmoid(z)


def _bf(x):
    return x.astype(BF16)


def _mm(a, b, dn):
    return lax.dot_general(a, b, dn, preferred_element_type=F32)


def _each(f, *lists):
    return [f(*args) for args in zip(*lists)]


def _rmsnorm_kernel(x_ref, g_ref, o_ref):
    x = x_ref[...].astype(F32)
    y = x * lax.rsqrt(jnp.mean(x * x, axis=-1, keepdims=True) + RMS_EPS)
    o_ref[...] = (y * g_ref[...]).astype(o_ref.dtype)


def _rmsnorm(x2d, g, out_dtype, tm):
    m, d = x2d.shape
    return pl.pallas_call(
        _rmsnorm_kernel,
        grid=(m // tm,),
        in_specs=[pl.BlockSpec((tm, d), lambda i: (i, 0)), pl.BlockSpec((1, d), lambda i: (0, 0))],
        out_specs=pl.BlockSpec((tm, d), lambda i: (i, 0)),
        out_shape=jax.ShapeDtypeStruct((m, d), out_dtype),
        compiler_params=_params(1),
        name="rmsnorm",
    )(x2d, g.reshape(1, d))


def _matmul_kernel(a_ref, w_ref, o_ref):
    o_ref[...] = jnp.dot(a_ref[...], w_ref[...], preferred_element_type=F32).astype(o_ref.dtype)


def _matmul(a, w, out_dtype, tm, tn, name):
    m, k = a.shape
    n = w.shape[1]
    return pl.pallas_call(
        _matmul_kernel,
        grid=(n // tn, m // tm),
        in_specs=[pl.BlockSpec((tm, k), lambda j, i: (i, 0)), pl.BlockSpec((k, tn), lambda j, i: (0, j))],
        out_specs=pl.BlockSpec((tm, tn), lambda j, i: (i, j)),
        out_shape=jax.ShapeDtypeStruct((m, n), out_dtype),
        compiler_params=_params(2),
        name=name,
    )(a, w)


def _t5_bucket(dist):
    n = jnp.maximum(dist, 0)
    max_exact = REL_BUCKETS // 2
    nf = jnp.maximum(n, max_exact).astype(F32)
    large = max_exact + (jnp.log(nf / max_exact) / math.log(REL_MAX_DIST / max_exact) * (REL_BUCKETS - max_exact)).astype(jnp.int32)
    large = jnp.minimum(large, REL_BUCKETS - 1)
    return jnp.where(n < max_exact, n, large)


def _bias_kernel(rel_ref, bucket_ref, o_ref):
    h = pl.program_id(0)
    bucket = bucket_ref[...]
    acc = jnp.zeros(bucket.shape, F32)
    for b in range(REL_BUCKETS):
        acc = jnp.where(bucket == b, rel_ref[b, h], acc)
    qpos = lax.broadcasted_iota(jnp.int32, bucket.shape, 0) + MOBA_BLOCK
    kpos = lax.broadcasted_iota(jnp.int32, bucket.shape, 1)
    o_ref[0] = jnp.where(kpos <= qpos, acc * LOG2E, MASKED)


def _bias_tiles(rel_bias):
    blk = MOBA_BLOCK
    qpos = lax.broadcasted_iota(jnp.int32, (blk, 2 * blk), 0) + blk
    kpos = lax.broadcasted_iota(jnp.int32, (blk, 2 * blk), 1)
    bucket = _t5_bucket(qpos - kpos)
    return pl.pallas_call(
        _bias_kernel,
        grid=(A_HEADS,),
        in_specs=[pl.BlockSpec(memory_space=pltpu.SMEM), pl.BlockSpec((blk, 2 * blk), lambda h: (0, 0))],
        out_specs=pl.BlockSpec((1, blk, 2 * blk), lambda h: (h, 0, 0)),
        out_shape=jax.ShapeDtypeStruct((A_HEADS, blk, 2 * blk), F32),
        compiler_params=_params(1),
        name="t5_bias",
    )(rel_bias, bucket)


def _moba_kernel(rel_ref, q_ref, k_ref, v_ref, z_ref, bias_ref, o_ref, kmean_ref):
    hb = pl.program_id(1)
    qi = pl.program_id(2)
    blk, hd = MOBA_BLOCK, A_HEAD_DIM
    n_heads = q_ref.shape[1] // hd
    nb = k_ref.shape[0] // blk
    nbp = kmean_ref.shape[1]

    @pl.when(qi == 0)
    def _():
        kmean_ref[...] = jnp.zeros_like(kmean_ref)
        for j in range(nb):
            mean_j = jnp.sum(k_ref[j * blk:(j + 1) * blk, :].astype(F32), axis=0, keepdims=True) * (1.0 / blk)
            for hh in range(n_heads):
                kmean_ref[hh, j:j + 1, :] = mean_j[:, hh * hd:(hh + 1) * hd]

    c1 = hd ** -0.5 * LOG2E
    n_far = min(MOBA_FAR_GROUP, nb)
    never = 127
    blk_id = lax.broadcasted_iota(jnp.int32, (nbp, blk), 0)
    blk_f = blk_id.astype(F32)
    eye = jnp.where(lax.broadcasted_iota(jnp.int32, (nbp, 128), 0) == lax.broadcasted_iota(jnp.int32, (nbp, 128), 1), 1.0, 0.0).astype(BF16)
    lane = lax.broadcasted_iota(jnp.int32, (blk, 128), 1)
    j_prev = jnp.maximum(qi - 1, 0)
    own = pl.multiple_of(qi * blk, blk)
    prev = pl.multiple_of(j_prev * blk, blk)

    heads = [slice(hh * hd, (hh + 1) * hd) for hh in range(n_heads)]
    q = [q_ref[:, hl] for hl in heads]

    def gate_of(qh, hh):
        km = kmean_ref[hh]
        km_hi = km.astype(BF16)
        km_lo = (km - km_hi.astype(F32)).astype(BF16)
        return jnp.where(blk_id < qi, _mm(km_hi, qh, NT) + _mm(km_lo, qh, NT), -jnp.inf)

    g = _each(gate_of, q, range(n_heads))
    sel_t = [jnp.zeros((nbp, blk), F32) for _ in heads]
    for _ in range(MOBA_TOPK):
        gmax = _each(lambda x: jnp.max(x, axis=0, keepdims=True), g)
        first = _each(lambda x, mx: jnp.min(jnp.where(x == mx, blk_f, float(nbp)), axis=0, keepdims=True), g, gmax)
        pick = _each(lambda f, mx: (blk_f == f) & (mx > -jnp.inf), first, gmax)
        sel_t = _each(lambda p, s: jnp.where(p, 1.0, s), pick, sel_t)
        g = _each(lambda p, x: jnp.where(p, -jnp.inf, x), pick, g)
    sel = _each(lambda s: _mm(s.astype(BF16), eye, TN), sel_t)
    qm = _each(lambda qh, s: jnp.concatenate([qh, jnp.where(s > 0.0, 0.0, MASKED).astype(BF16)], axis=1), q, sel)

    def masked_logits(keys, key_blk):
        col = jnp.where(lax.broadcasted_iota(jnp.int32, key_blk.shape, 1) == key_blk, 1.0, 0.0).astype(BF16)
        return _each(lambda x, kk: _mm(x, jnp.concatenate([kk, col], axis=1), NT), qm, keys)

    def pv(p, values):
        ones = jnp.ones(values[0].shape, BF16)
        return _each(lambda x, v: jnp.dot(x.astype(BF16), jnp.concatenate([v, ones], axis=1), preferred_element_type=F32), p, values)

    cat0 = lambda ref, hl: jnp.concatenate([ref[pl.ds(prev, blk), hl], ref[pl.ds(own, blk), hl]], axis=0)
    key_row = lax.broadcasted_iota(jnp.int32, (2 * blk, 128), 0)
    raw = masked_logits([cat0(k_ref, hl) for hl in heads], jnp.where(key_row < blk, j_prev, -1))
    t = _each(lambda x, hh: x * c1 + bias_ref[hh], raw, range(n_heads))
    m = _each(lambda x: jnp.max(x, axis=-1, keepdims=True), t)
    p = _each(lambda x, mx: jnp.exp2(x - mx), t, m)
    acc = pv(p, [cat0(v_ref, hl) for hl in heads])

    bias_far = [rel_ref[REL_BUCKETS - 1, hb * n_heads + hh] * LOG2E for hh in range(n_heads)]
    far_row_blk = lax.broadcasted_iota(jnp.int32, (n_far * blk, 128), 0) // blk

    def body(gi, carry):
        m, acc = list(carry[:n_heads]), list(carry[n_heads:])
        rows = pl.ds(pl.multiple_of(gi * (n_far * blk), n_far * blk), n_far * blk)
        key_blk = gi * n_far + far_row_blk
        raw = masked_logits([k_ref[rows, hl] for hl in heads], jnp.where(key_blk < j_prev, key_blk, never))
        m_new = _each(lambda mx, x, b: jnp.maximum(mx, jnp.max(x, axis=-1, keepdims=True) * c1 + b), m, raw, bias_far)
        p = _each(lambda x, mn, b: jnp.exp2(x * c1 + (b - mn)), raw, m_new, bias_far)
        alpha = _each(lambda mx, mn: jnp.exp2(mx - mn), m, m_new)
        acc = _each(lambda a, c, x: a * c + x, alpha, acc, pv(p, [v_ref[rows, hl] for hl in heads]))
        return tuple(m_new + acc)

    acc = lax.fori_loop(0, (j_prev + n_far - 1) // n_far, body, tuple(m + acc))[n_heads:]
    for hh, hl in enumerate(heads):
        y = acc[hh][:, :hd] / acc[hh][:, hd:]
        o_ref[:, hl] = (y * _silu(z_ref[:, hl].astype(F32))).astype(o_ref.dtype)


def _moba(p1, rel_bias, bias_tiles, bsz, seq, col0):
    blk = MOBA_BLOCK
    nq = seq // blk
    hw = MOBA_HEADS_PER_STEP * A_HEAD_DIM
    ns = A_WIDTH // hw
    c = col0 // hw
    nbp = max(16, -(-nq // 8) * 8)
    return pl.pallas_call(
        _moba_kernel,
        grid=(bsz, ns, nq),
        in_specs=[
            pl.BlockSpec(memory_space=pltpu.SMEM),
            pl.BlockSpec((blk, hw), lambda b, h, i: (b * nq + i, c + h)),
            pl.BlockSpec((seq, hw), lambda b, h, i: (b, c + ns + h)),
            pl.BlockSpec((seq, hw), lambda b, h, i: (b, c + 2 * ns + h)),
            pl.BlockSpec((blk, hw), lambda b, h, i: (b * nq + i, c + 3 * ns + h)),
            pl.BlockSpec((MOBA_HEADS_PER_STEP, blk, 2 * blk), lambda b, h, i: (h, 0, 0)),
        ],
        out_specs=pl.BlockSpec((blk, hw), lambda b, h, i: (b * nq + i, h)),
        out_shape=jax.ShapeDtypeStruct((bsz * seq, A_WIDTH), BF16),
        scratch_shapes=[pltpu.VMEM((MOBA_HEADS_PER_STEP, nbp, A_HEAD_DIM), F32)],
        compiler_params=_params(3),
        name="moba_attention",
    )(rel_bias, p1, p1, p1, p1, bias_tiles)


def _cross_kernel(q_ref, k_ref, v_ref, z_ref, o_ref):
    s = lax.dot_general(q_ref[...], k_ref[...], NT, preferred_element_type=F32) * (C_HEAD_DIM ** -0.5)
    m = jnp.max(s, axis=-1, keepdims=True)
    p = jnp.exp(s - m)
    l = jnp.sum(p, axis=-1, keepdims=True)
    y = jnp.dot(p.astype(BF16), v_ref[...], preferred_element_type=F32) / l
    o_ref[...] = (y * _silu(z_ref[...].astype(F32))).astype(o_ref.dtype)


def _cross(p1, kv, bsz, seq, mem_len, col0, tq):
    hd = C_HEAD_DIM
    c = col0 // hd
    nt = seq // tq
    return pl.pallas_call(
        _cross_kernel,
        grid=(bsz, C_HEADS, nt),
        in_specs=[
            pl.BlockSpec((tq, hd), lambda b, h, i: (b * nt + i, c + h)),
            pl.BlockSpec((mem_len, hd), lambda b, h, i: (b, h)),
            pl.BlockSpec((mem_len, hd), lambda b, h, i: (b, C_HEADS + h)),
            pl.BlockSpec((tq, hd), lambda b, h, i: (b * nt + i, c + C_HEADS + h)),
        ],
        out_specs=pl.BlockSpec((tq, hd), lambda b, h, i: (b * nt + i, h)),
        out_shape=jax.ShapeDtypeStruct((bsz * seq, C_WIDTH), BF16),
        compiler_params=_params(3),
        name="memory_attention",
    )(p1, kv, kv, p1)


class _WkvConsts:
    def __init__(self):
        c, g, hd = WKV_CHUNK, WKV_GROUP, B_HEAD_DIM
        row = lax.broadcasted_iota(jnp.int32, (g, g), 0)
        col = lax.broadcasted_iota(jnp.int32, (g, g), 1)
        self.same_head = (row // hd) == (col // hd)
        self.ones_bd = jnp.where(self.same_head, 1.0, 0.0).astype(BF16)
        t = lax.broadcasted_iota(jnp.int32, (c, g), 0)
        s = lax.broadcasted_iota(jnp.int32, (c, g), 1) % hd
        self.strict = s < t
        self.incl = s <= t
        self.eye = jnp.where(s == t, 1.0, 0.0).astype(F32)
        tr = lax.broadcasted_iota(jnp.int32, (c, c), 0)
        tc = lax.broadcasted_iota(jnp.int32, (c, c), 1)
        self.lower = jnp.where(tc <= tr, 1.0, 0.0).astype(BF16)

    def bd(self, p):
        tiled = jnp.concatenate([p] * WKV_HEADS_PER_GROUP, axis=0)
        return jnp.where(self.same_head, tiled, jnp.zeros((), p.dtype))

    def segsum(self, x):
        return jnp.dot(_bf(x), self.ones_bd, preferred_element_type=F32)


def _wkv_chunk(rm, kmod, vm, avec, bvec, logdec, state, k):
    c = WKV_CHUNK
    cat = lambda u, w: jnp.concatenate([u, w], axis=0)
    ld_hi = _each(_bf, logdec)
    ld_lo = _each(lambda x, hi: _bf(x - hi.astype(F32)), logdec, ld_hi)
    cumsum = lambda hi, lo: jnp.dot(k.lower, hi, preferred_element_type=F32) + jnp.dot(k.lower, lo, preferred_element_type=F32)
    cl = _each(cumsum, ld_hi, ld_lo)
    cl_last = _each(lambda x: x[c - 1:c, :], cl)
    rt = _each(lambda r, x: _bf(r * jnp.exp(x)), rm, cl)
    at = _each(lambda a, x, ld: _bf(a * jnp.exp(x - ld)), avec, cl, logdec)
    e_neg = _each(lambda x: jnp.exp(-x), cl)
    bt = _each(lambda b, e: k.bd(_bf(b * e)), bvec, e_neg)
    kt = _each(lambda kk, e: k.bd(_bf(kk * e)), kmod, e_neg)
    e_rem = _each(lambda xl, x: jnp.exp(xl - x), cl_last, cl)
    bh = _each(lambda b, e: _bf(b * e), bvec, e_rem)
    kh = _each(lambda kk, e: _bf(kk * e), kmod, e_rem)
    vb = _each(_bf, vm)

    lhs = _each(cat, at, rt)
    ab = _each(lambda x, y: _mm(x, y, NT), lhs, bt)
    ak = _each(lambda x, y: _mm(x, y, NT), lhs, kt)
    a_ab = _each(lambda x: jnp.where(k.strict, x[:c], 0.0), ab)
    a_rb = _each(lambda x: jnp.where(k.incl, x[c:], 0.0), ab)
    a_ak = _each(lambda x: jnp.where(k.strict, x[:c], 0.0), ak)
    a_rk = _each(lambda x: jnp.where(k.incl, x[c:], 0.0), ak)

    inv = _each(lambda x: k.eye + x, a_ab)
    pw = _each(_bf, a_ab)
    pw = _each(lambda p: _bf(_mm(p, k.bd(p), NN)), pw)
    for _ in range(int(math.log2(c)) - 2):
        tp = _each(lambda i, p: _mm(cat(_bf(i), p), k.bd(p), NN), inv, pw)
        inv = _each(lambda i, t: i + t[:c], inv, tp)
        pw = _each(lambda t: _bf(t[c:]), tp)
    inv = _each(lambda i, p: i + _mm(_bf(i), k.bd(p), NN), inv, pw)
    invb = _each(_bf, inv)

    av = _each(lambda x, y, v: _mm(cat(_bf(x), _bf(y)), k.bd(v), NN), a_ak, a_rk, vb)
    w = _each(lambda i, a: _mm(i, k.bd(a), NN), invb, at)
    u0 = _each(lambda i, x: _mm(i, k.bd(_bf(x[:c])), NN), invb, av)

    uy = _each(lambda ww, r, s: _mm(cat(_bf(ww), r), _bf(s), NT), w, rt, state)
    u = _each(lambda x, y: x[:c] + y, uy, u0)
    ub = _each(_bf, u)
    y = _each(lambda x, a, uu, z: x[c:] + _mm(_bf(a), k.bd(uu), NN) + z[c:], uy, a_rb, ub, av)
    upd = _each(lambda uu, v, b, kk: _mm(cat(uu, v), cat(b, kk), TN), ub, vb, bh, kh)
    new_state = _each(lambda s, xl, d: s * jnp.exp(xl) + jnp.where(k.same_head, d, 0.0), state, cl_last, upd)
    return y, new_state


def _rwkv_kernel(r_ref, k_ref, v_ref, lwla_ref, z_ref, mu_r_ref, mu_k_ref, mu_v_ref, mu_wa_ref, w0_ref, a0_ref,
                 kk_ref, ka_ref, rk_ref, lnw_ref, lnb_ref, wd_ref, wa_ref, o_ref, state_ref, prev_ref, prev_lora_ref):
    c, g = WKV_CHUNK, WKV_GROUP
    n_groups = r_ref.shape[1] // g

    @pl.when(pl.program_id(2) == 0)
    def _():
        state_ref[...] = jnp.zeros_like(state_ref)
        prev_ref[...] = jnp.zeros_like(prev_ref)
        prev_lora_ref[...] = jnp.zeros_like(prev_lora_ref)

    k = _WkvConsts()
    wd = _bf(wd_ref[...])
    wa = _bf(wa_ref[...])
    inv_hd = 1.0 / B_HEAD_DIM
    groups = [slice(gi * g, (gi + 1) * g) for gi in range(n_groups)]
    split = lambda x: [x[:, ln] for ln in groups]

    def mix(x, p_ref, slot, mu_ref):
        first_row = lax.broadcasted_iota(jnp.int32, x.shape, 0) == 0
        prev = jnp.where(first_row, p_ref[slot:slot + 1, :], pltpu.roll(x, 1, 0))
        p_ref[slot:slot + 1, :] = x[c - 1:c, :]
        return x + (prev - x) * mu_ref[...]

    def chunk(ci, carry):
        rows = pl.ds(pl.multiple_of(ci * c, c), c)
        rm = mix(r_ref[rows, :], prev_ref, 0, mu_r_ref)
        km = mix(k_ref[rows, :], prev_ref, 8, mu_k_ref)
        vm = mix(v_ref[rows, :], prev_ref, 16, mu_v_ref)
        lwla = mix(lwla_ref[rows, :], prev_lora_ref, 0, mu_wa_ref)
        lw = _bf(jnp.tanh(lwla[:, :LORA_PAD]))
        la = _bf(lwla[:, LORA_PAD:])
        logdec = -math.exp(-0.5) * jax.nn.sigmoid(w0_ref[...] + jnp.dot(lw, wd, preferred_element_type=F32))
        a_lr = jax.nn.sigmoid(a0_ref[...] + jnp.dot(la, wa, preferred_element_type=F32))
        kmod = km * (1.0 + (a_lr - 1.0) * ka_ref[...])
        kk = split(km * kk_ref[...])
        kk = _each(lambda x, ss: x / jnp.maximum(jnp.sqrt(ss), 1e-12), kk, _each(lambda x: k.segsum(x * x), kk))
        bonus = _each(lambda x, v: k.segsum(x) * v, split(rm * kmod * rk_ref[...]), split(vm))
        state = [state_ref[gi] for gi in range(n_groups)]
        y, new_state = _wkv_chunk(split(rm), split(kmod), split(vm), _each(lambda x: -x, kk),
                                  _each(lambda x, a: x * a, kk, split(a_lr)), split(logdec), state, k)
        for gi in range(n_groups):
            state_ref[gi] = new_state[gi]
        yc = _each(lambda x, mean: x - mean * inv_hd, y, _each(k.segsum, y))
        var = _each(lambda x: k.segsum(x * x) * inv_hd, yc)
        gate = _silu(z_ref[rows, :].astype(F32))
        for gi, ln in enumerate(groups):
            yn = yc[gi] * lax.rsqrt(var[gi] + LNX_EPS) * lnw_ref[:, ln] + lnb_ref[:, ln]
            o_ref[rows, ln] = ((yn + bonus[gi]) * gate[:, ln]).astype(o_ref.dtype)
        return carry

    lax.fori_loop(0, r_ref.shape[0] // c, chunk, 0)


def _rwkv(p2, p1, vecs, wd2, wa2, bsz, seq, z_col0, tt):
    g = WKV_GROUP
    gw = WKV_GROUPS_PER_STEP * g
    ns = B_WIDTH // gw
    nt = seq // tt
    zc = z_col0 // gw
    row = lambda b, gi, t: b * nt + t
    vec_spec = pl.BlockSpec((1, gw), lambda b, gi, t: (0, gi))
    lora_vec_spec = pl.BlockSpec((1, 2 * LORA_PAD), lambda b, gi, t: (0, 0))
    lora_w_spec = pl.BlockSpec((LORA_PAD, gw), lambda b, gi, t: (0, gi))
    mu_r, mu_k, mu_v, mu_wa, w0, a0, k_k, k_a, r_k, lnw, lnb = vecs
    return pl.pallas_call(
        _rwkv_kernel,
        grid=(bsz, ns, nt),
        in_specs=[
            pl.BlockSpec((tt, gw), lambda b, gi, t: (row(b, gi, t), gi)),
            pl.BlockSpec((tt, gw), lambda b, gi, t: (row(b, gi, t), ns + gi)),
            pl.BlockSpec((tt, gw), lambda b, gi, t: (row(b, gi, t), 2 * ns + gi)),
            pl.BlockSpec((tt, 2 * LORA_PAD), lambda b, gi, t: (row(b, gi, t), 3 * B_WIDTH // (2 * LORA_PAD))),
            pl.BlockSpec((tt, gw), lambda b, gi, t: (row(b, gi, t), zc + gi)),
            vec_spec, vec_spec, vec_spec, lora_vec_spec, vec_spec, vec_spec, vec_spec, vec_spec, vec_spec, vec_spec, vec_spec,
            lora_w_spec, lora_w_spec,
        ],
        out_specs=pl.BlockSpec((tt, gw), lambda b, gi, t: (row(b, gi, t), gi)),
        out_shape=jax.ShapeDtypeStruct((bsz * seq, B_WIDTH), BF16),
        scratch_shapes=[pltpu.VMEM((WKV_GROUPS_PER_STEP, g, g), F32), pltpu.VMEM((24, gw), F32), pltpu.VMEM((8, 2 * LORA_PAD), F32)],
        compiler_params=_params(3),
        name="rwkv7_time_mix",
    )(p2, p2, p2, p2, p1, mu_r, mu_k, mu_v, mu_wa, w0, a0, k_k, k_a, r_k, lnw, lnb, wd2, wa2)


def _merge_kernel(ya_ref, yb_ref, yc_ref, wa_ref, wb_ref, wc_ref, ga_ref, gb_ref, gc_ref, o_ref):
    def branch(y_ref, w_ref, g_ref):
        return jax.nn.sigmoid(g_ref[...].astype(F32)) * jnp.dot(y_ref[...], w_ref[...], preferred_element_type=F32)

    o_ref[...] = (branch(ya_ref, wa_ref, ga_ref) + branch(yb_ref, wb_ref, gb_ref) + branch(yc_ref, wc_ref, gc_ref)).astype(o_ref.dtype)


def _merge(ya, yb, yc, wa, wb, wc, p1, d_model, tm, tn):
    m = ya.shape[0]
    nj = d_model // tn
    y_spec = lambda width: pl.BlockSpec((tm, width), lambda j, i: (i, 0))
    w_spec = lambda width: pl.BlockSpec((width, tn), lambda j, i: (0, j))
    g_spec = lambda br: pl.BlockSpec((tm, tn), lambda j, i: (i, br * nj + j))
    return pl.pallas_call(
        _merge_kernel,
        grid=(nj, m // tm),
        in_specs=[y_spec(A_WIDTH), y_spec(B_WIDTH), y_spec(C_WIDTH), w_spec(A_WIDTH), w_spec(B_WIDTH), w_spec(C_WIDTH),
                  g_spec(0), g_spec(1), g_spec(2)],
        out_specs=pl.BlockSpec((tm, tn), lambda j, i: (i, j)),
        out_shape=jax.ShapeDtypeStruct((m, d_model), BF16),
        compiler_params=_params(2),
        name="gated_merge",
    )(ya, yb, yc, wa, wb, wc, p1, p1, p1)


def _out_kernel(m_ref, w_ref, x_ref, g_ref, o_ref, *, final_norm):
    y = x_ref[...] + jnp.dot(m_ref[...], w_ref[...], preferred_element_type=F32)
    if final_norm:
        y = y * lax.rsqrt(jnp.mean(y * y, axis=-1, keepdims=True) + RMS_EPS) * g_ref[...]
    o_ref[...] = y


def _out_proj(merged, w_out, x2d, g, final_norm, tm):
    m, d = x2d.shape
    return pl.pallas_call(
        functools.partial(_out_kernel, final_norm=final_norm),
        grid=(m // tm,),
        in_specs=[pl.BlockSpec((tm, d), lambda i: (i, 0)), pl.BlockSpec((d, d), lambda i: (0, 0)),
                  pl.BlockSpec((tm, d), lambda i: (i, 0)), pl.BlockSpec((1, d), lambda i: (0, 0))],
        out_specs=pl.BlockSpec((tm, d), lambda i: (i, 0)),
        out_shape=jax.ShapeDtypeStruct((m, d), F32),
        compiler_params=_params(1),
        name="out_proj",
    )(merged, w_out, x2d, g.reshape(1, d))


def _pick_tile(n, candidates):
    for t in candidates:
        if n % t == 0:
            return t
    raise ValueError(f"no tile for extent {n}")


def _pad_cols(w, width):
    return jnp.pad(w, ((0, 0), (0, width - w.shape[1])))


def _layer(x2d, mem2d, bsz, seq, mem_len, bias_tiles, rel_bias, norm_g, mem_norm_g, w_in, rw, w_mem_kv, w_proj_a, w_proj_b,
           w_proj_c, w_out, final_g):
    m, d = x2d.shape
    tm = _pick_tile(m, (1024, 512, 256))

    c_qkv_a = 0
    c_z_a = c_qkv_a + 3 * A_WIDTH
    c_rkv_b = c_z_a + A_WIDTH
    c_z_b = c_rkv_b + 3 * B_WIDTH
    c_lw = c_z_b + B_WIDTH
    c_la = c_lw + LORA
    c_q_c = c_la + LORA
    c_gates = c_q_c + 2 * C_WIDTH
    c_end = c_gates + N_BRANCHES * d

    w1 = jnp.concatenate([w_in[:, c_gates:c_end], w_in[:, c_qkv_a:c_rkv_b], w_in[:, c_z_b:c_lw], w_in[:, c_q_c:c_gates]], axis=1)
    p1_a = N_BRANCHES * d
    p1_zb = p1_a + 4 * A_WIDTH
    p1_c = p1_zb + B_WIDTH
    tn1 = 1024
    w1 = _pad_cols(w1, -(-w1.shape[1] // tn1) * tn1).astype(BF16)
    w2 = jnp.concatenate([w_in[:, c_rkv_b:c_z_b], _pad_cols(w_in[:, c_lw:c_la], LORA_PAD), _pad_cols(w_in[:, c_la:c_q_c], LORA_PAD)], axis=1)
    tn2 = 1024
    w2 = _pad_cols(w2, -(-w2.shape[1] // tn2) * tn2).astype(BF16)

    h = _rmsnorm(x2d, norm_g, BF16, tm)
    p1 = _matmul(h, w1, BF16, tm, tn1, "in_proj_bf16")
    p2 = _matmul(h, w2, F32, tm, tn2, "in_proj_f32")

    mem_n = _rmsnorm(mem2d, mem_norm_g, BF16, _pick_tile(mem2d.shape[0], (1024, 512, 256)))
    kv = _matmul(mem_n, w_mem_kv.astype(BF16), BF16, _pick_tile(mem2d.shape[0], (1024, 512, 256)), 1024, "mem_kv")

    ya = _moba(p1, rel_bias, bias_tiles, bsz, seq, p1_a)
    yc = _cross(p1, kv, bsz, seq, mem_len, p1_c, _pick_tile(seq, (1024, 512, 256)))

    (mu_r, mu_k, mu_v, mu_w, mu_a, w0, w_decay2, a0, w_aaa2, k_k, k_a, r_k, lnx_w, lnx_b) = rw
    vec = lambda v: v.reshape(1, B_WIDTH)
    lora_vec = lambda v: _pad_cols(v.reshape(1, LORA), LORA_PAD)
    mu_wa = jnp.concatenate([lora_vec(mu_w), lora_vec(mu_a)], axis=1)
    lora_w = lambda w: jnp.pad(w, ((0, LORA_PAD - LORA), (0, 0)))
    vecs = (vec(mu_r), vec(mu_k), vec(mu_v), mu_wa, vec(w0), vec(a0), vec(k_k), vec(k_a), vec(r_k), vec(lnx_w), vec(lnx_b))
    yb = _rwkv(p2, p1, vecs, lora_w(w_decay2), lora_w(w_aaa2), bsz, seq, p1_zb, _pick_tile(seq, (512, 256, 128, 64)))

    merged = _merge(ya, yb, yc, w_proj_a.astype(BF16), w_proj_b.astype(BF16), w_proj_c.astype(BF16), p1, d,
                    _pick_tile(m, (512, 256)), 1024)
    g = final_g if final_g is not None else jnp.ones((d,), F32)
    return _out_proj(merged, w_out.astype(BF16), x2d, g, final_g is not None, _pick_tile(m, (512, 256)))


def kernel(x, mem, rel_bias, norm_g, mem_norm_g, w_in, rw_mu_r, rw_mu_k, rw_mu_v, rw_mu_w, rw_mu_a, rw_w0, rw_w_decay2, rw_a0, rw_w_aaa2, rw_k_k, rw_k_a, rw_r_k, rw_lnx_w, rw_lnx_b, w_mem_kv, w_proj_a, w_proj_b, w_proj_c, w_out, final_norm_g):
    bsz, seq, d = x.shape
    mem_len = mem.shape[1]
    depth = norm_g.shape[0]
    x2d = x.reshape(bsz * seq, d)
    mem2d = mem.reshape(bsz * mem_len, d)
    bias_tiles = _bias_tiles(rel_bias)
    for l in range(depth):
        rw = (rw_mu_r[l], rw_mu_k[l], rw_mu_v[l], rw_mu_w[l], rw_mu_a[l], rw_w0[l], rw_w_decay2[l], rw_a0[l], rw_w_aaa2[l],
              rw_k_k[l], rw_k_a[l], rw_r_k[l], rw_lnx_w[l], rw_lnx_b[l])
        x2d = _layer(x2d, mem2d, bsz, seq, mem_len, bias_tiles, rel_bias, norm_g[l], mem_norm_g[l], w_in[l], rw, w_mem_kv[l],
                     w_proj_a[l], w_proj_b[l], w_proj_c[l], w_out[l], final_norm_g if l == depth - 1 else None)
    return x2d.reshape(bsz, seq, d)
```

```python
import functools
import math

import jax
import jax.numpy as jnp
from jax import lax
from jax.experimental import pallas as pl
from jax.experimental.pallas import tpu as pltpu

F32 = jnp.float32
BF16 = jnp.bfloat16

RMS_EPS = 1e-6

A_HEADS = 12
A_HEAD_DIM = 128
A_WIDTH = A_HEADS * A_HEAD_DIM
MOBA_BLOCK = 256
MOBA_TOPK = 3
MOBA_HEADS_PER_STEP = 2
MOBA_FAR_GROUP = 4
MOBA_Q_SCALE = A_HEAD_DIM ** -0.5 * math.log2(math.e)
REL_BUCKETS = 32
REL_MAX_DIST = 128

B_HEAD_DIM = 64
B_WIDTH = 1536
LORA = 96
LORA_PAD = 128
LNX_EPS = 64e-5
WKV_CHUNK = 64
WKV_GROUP = 256
WKV_HEADS_PER_GROUP = WKV_GROUP // B_HEAD_DIM
WKV_GROUPS_PER_STEP = 6

C_HEADS = 4
C_HEAD_DIM = 256
C_WIDTH = C_HEADS * C_HEAD_DIM

N_BRANCHES = 3

VMEM_LIMIT = 48 * 1024 * 1024

NN = (((1,), (0,)), ((), ()))
NT = (((1,), (1,)), ((), ()))
TN = (((0,), (0,)), ((), ()))

MASKED = -1e30
LOG2E = math.log2(math.e)


def _params(n_axes):
    return pltpu.CompilerParams(dimension_semantics=("arbitrary",) * n_axes, vmem_limit_bytes=VMEM_LIMIT)


def _silu(z):
    return z * jax.nn.sigmoid(z)


def _bf(x):
    return x.astype(BF16)


def _mm(a, b, dn):
    return lax.dot_general(a, b, dn, preferred_element_type=F32)


def _each(f, *lists):
    return [f(*args) for args in zip(*lists)]


def _rmsnorm_kernel(x_ref, g_ref, o_ref):
    x = x_ref[...].astype(F32)
    y = x * lax.rsqrt(jnp.mean(x * x, axis=-1, keepdims=True) + RMS_EPS)
    o_ref[...] = (y * g_ref[...]).astype(o_ref.dtype)


def _rmsnorm(x2d, g, out_dtype, tm):
    m, d = x2d.shape
    return pl.pallas_call(
        _rmsnorm_kernel,
        grid=(m // tm,),
        in_specs=[pl.BlockSpec((tm, d), lambda i: (i, 0)), pl.BlockSpec((1, d), lambda i: (0, 0))],
        out_specs=pl.BlockSpec((tm, d), lambda i: (i, 0)),
        out_shape=jax.ShapeDtypeStruct((m, d), out_dtype),
        compiler_params=_params(1),
        name="rmsnorm",
    )(x2d, g.reshape(1, d))


def _matmul_kernel(a_ref, w_ref, o_ref):
    o_ref[...] = jnp.dot(a_ref[...], w_ref[...], preferred_element_type=F32).astype(o_ref.dtype)


def _matmul(a, w, out_dtype, tm, tn, name):
    m, k = a.shape
    n = w.shape[1]
    return pl.pallas_call(
        _matmul_kernel,
        grid=(n // tn, m // tm),
        in_specs=[pl.BlockSpec((tm, k), lambda j, i: (i, 0)), pl.BlockSpec((k, tn), lambda j, i: (0, j))],
        out_specs=pl.BlockSpec((tm, tn), lambda j, i: (i, j)),
        out_shape=jax.ShapeDtypeStruct((m, n), out_dtype),
        compiler_params=_params(2),
        name=name,
    )(a, w)


def _matmul_w32_kernel(a_ref, w_ref, s_ref, o_ref, wb_ref):
    @pl.when(pl.program_id(1) == 0)
    def _():
        wb_ref[...] = w_ref[...].astype(BF16)

    acc = jnp.dot(a_ref[...], wb_ref[...], preferred_element_type=F32)
    o_ref[...] = (acc * s_ref[...]).astype(o_ref.dtype)


def _matmul_w32(a, w, col0, n, col_scale, out_dtype, tm, tn, name):
    m, k = a.shape
    j0 = col0 // tn
    return pl.pallas_call(
        _matmul_w32_kernel,
        grid=(n // tn, m // tm),
        in_specs=[pl.BlockSpec((tm, k), lambda j, i: (i, 0)), pl.BlockSpec((k, tn), lambda j, i: (0, j0 + j)),
                  pl.BlockSpec((1, tn), lambda j, i: (0, j))],
        out_specs=pl.BlockSpec((tm, tn), lambda j, i: (i, j)),
        out_shape=jax.ShapeDtypeStruct((m, n), out_dtype),
        scratch_shapes=[pltpu.VMEM((k, tn), BF16)],
        compiler_params=_params(2),
        name=name,
    )(a, w, col_scale)


def _t5_bucket(dist):
    n = jnp.maximum(dist, 0)
    max_exact = REL_BUCKETS // 2
    nf = jnp.maximum(n, max_exact).astype(F32)
    large = max_exact + (jnp.log(nf / max_exact) / math.log(REL_MAX_DIST / max_exact) * (REL_BUCKETS - max_exact)).astype(jnp.int32)
    large = jnp.minimum(large, REL_BUCKETS - 1)
    return jnp.where(n < max_exact, n, large)


def _bias_kernel(rel_ref, bucket_ref, o_ref):
    h = pl.program_id(0)
    bucket = bucket_ref[...]
    acc = jnp.zeros(bucket.shape, F32)
    for b in range(REL_BUCKETS):
        acc = jnp.where(bucket == b, rel_ref[b, h], acc)
    qpos = lax.broadcasted_iota(jnp.int32, bucket.shape, 0) + MOBA_BLOCK
    kpos = lax.broadcasted_iota(jnp.int32, bucket.shape, 1)
    o_ref[0] = jnp.where(kpos <= qpos, acc * LOG2E, MASKED)


def _bias_tiles(rel_bias):
    blk = MOBA_BLOCK
    qpos = lax.broadcasted_iota(jnp.int32, (blk, 2 * blk), 0) + blk
    kpos = lax.broadcasted_iota(jnp.int32, (blk, 2 * blk), 1)
    bucket = _t5_bucket(qpos - kpos)
    return pl.pallas_call(
        _bias_kernel,
        grid=(A_HEADS,),
        in_specs=[pl.BlockSpec(memory_space=pltpu.SMEM), pl.BlockSpec((blk, 2 * blk), lambda h: (0, 0))],
        out_specs=pl.BlockSpec((1, blk, 2 * blk), lambda h: (h, 0, 0)),
        out_shape=jax.ShapeDtypeStruct((A_HEADS, blk, 2 * blk), F32),
        compiler_params=_params(1),
        name="t5_bias",
    )(rel_bias, bucket)


def _moba_kernel(rel_ref, q_ref, k_ref, v_ref, z_ref, bias_ref, o_ref, kmean_ref):
    hb = pl.program_id(1)
    qi = pl.program_id(2)
    blk, hd = MOBA_BLOCK, A_HEAD_DIM
    n_heads = q_ref.shape[1] // hd
    nb = k_ref.shape[0] // blk
    nbp = kmean_ref.shape[1]

    @pl.when(qi == 0)
    def _():
        kmean_ref[...] = jnp.zeros_like(kmean_ref)
        for j in range(nb):
            mean_j = jnp.sum(k_ref[j * blk:(j + 1) * blk, :].astype(F32), axis=0, keepdims=True) * (1.0 / blk)
            for hh in range(n_heads):
                kmean_ref[hh, j:j + 1, :] = mean_j[:, hh * hd:(hh + 1) * hd]

    n_far = min(MOBA_FAR_GROUP, nb)
    never = 127
    blk_id = lax.broadcasted_iota(jnp.int32, (nbp, blk), 0)
    blk_f = blk_id.astype(F32)
    eye = jnp.where(lax.broadcasted_iota(jnp.int32, (nbp, 128), 0) == lax.broadcasted_iota(jnp.int32, (nbp, 128), 1), 1.0, 0.0).astype(BF16)
    lane = lax.broadcasted_iota(jnp.int32, (blk, 128), 1)
    j_prev = jnp.maximum(qi - 1, 0)
    own = pl.multiple_of(qi * blk, blk)
    prev = pl.multiple_of(j_prev * blk, blk)

    heads = [slice(hh * hd, (hh + 1) * hd) for hh in range(n_heads)]
    q = [q_ref[:, hl] for hl in heads]

    def gate_of(qh, hh):
        km = kmean_ref[hh]
        km_hi = km.astype(BF16)
        km_lo = (km - km_hi.astype(F32)).astype(BF16)
        return jnp.where(blk_id < qi, _mm(km_hi, qh, NT) + _mm(km_lo, qh, NT), -jnp.inf)

    g = _each(gate_of, q, range(n_heads))
    sel_t = [jnp.zeros((nbp, blk), F32) for _ in heads]
    for _ in range(MOBA_TOPK):
        gmax = _each(lambda x: jnp.max(x, axis=0, keepdims=True), g)
        first = _each(lambda x, mx: jnp.min(jnp.where(x == mx, blk_f, float(nbp)), axis=0, keepdims=True), g, gmax)
        pick = _each(lambda f, mx: (blk_f == f) & (mx > -jnp.inf), first, gmax)
        sel_t = _each(lambda p, s: jnp.where(p, 1.0, s), pick, sel_t)
        g = _each(lambda p, x: jnp.where(p, -jnp.inf, x), pick, g)
    sel = _each(lambda s: _mm(s.astype(BF16), eye, TN), sel_t)
    qm = _each(lambda qh, s: jnp.concatenate([qh, jnp.where(s > 0.0, 0.0, MASKED).astype(BF16)], axis=1), q, sel)

    def masked_logits(keys, key_blk):
        col = jnp.where(lax.broadcasted_iota(jnp.int32, key_blk.shape, 1) == key_blk, 1.0, 0.0).astype(BF16)
        return _each(lambda x, kk: _mm(x, jnp.concatenate([kk, col], axis=1), NT), qm, keys)

    def pv(p, values):
        ones = jnp.ones(values[0].shape, BF16)
        return _each(lambda x, v: jnp.dot(x.astype(BF16), jnp.concatenate([v, ones], axis=1), preferred_element_type=F32), p, values)

    cat0 = lambda ref, hl: jnp.concatenate([ref[pl.ds(prev, blk), hl], ref[pl.ds(own, blk), hl]], axis=0)
    key_row = lax.broadcasted_iota(jnp.int32, (2 * blk, 128), 0)
    raw = masked_logits([cat0(k_ref, hl) for hl in heads], jnp.where(key_row < blk, j_prev, -1))
    t = _each(lambda x, hh: x + bias_ref[hh], raw, range(n_heads))
    m = _each(lambda x: jnp.max(x, axis=-1, keepdims=True), t)
    p = _each(lambda x, mx: jnp.exp2(x - mx), t, m)
    acc = pv(p, [cat0(v_ref, hl) for hl in heads])

    bias_far = [rel_ref[REL_BUCKETS - 1, hb * n_heads + hh] * LOG2E for hh in range(n_heads)]
    far_row_blk = lax.broadcasted_iota(jnp.int32, (n_far * blk, 128), 0) // blk

    def body(gi, carry):
        m, acc = list(carry[:n_heads]), list(carry[n_heads:])
        rows = pl.ds(pl.multiple_of(gi * (n_far * blk), n_far * blk), n_far * blk)
        key_blk = gi * n_far + far_row_blk
        raw = masked_logits([k_ref[rows, hl] for hl in heads], jnp.where(key_blk < j_prev, key_blk, never))
        m_new = _each(lambda mx, x, b: jnp.maximum(mx, jnp.max(x, axis=-1, keepdims=True) + b), m, raw, bias_far)
        p = _each(lambda x, mn, b: jnp.exp2(x + (b - mn)), raw, m_new, bias_far)
        alpha = _each(lambda mx, mn: jnp.exp2(mx - mn), m, m_new)
        acc = _each(lambda a, c, x: a * c + x, alpha, acc, pv(p, [v_ref[rows, hl] for hl in heads]))
        return tuple(m_new + acc)

    acc = lax.fori_loop(0, (j_prev + n_far - 1) // n_far, body, tuple(m + acc))[n_heads:]
    for hh, hl in enumerate(heads):
        y = acc[hh][:, :hd] / acc[hh][:, hd:]
        o_ref[:, hl] = (y * _silu(z_ref[:, hl].astype(F32))).astype(o_ref.dtype)


def _moba(p_a, rel_bias, bias_tiles, bsz, seq):
    blk = MOBA_BLOCK
    nq = seq // blk
    hw = MOBA_HEADS_PER_STEP * A_HEAD_DIM
    ns = A_WIDTH // hw
    nbp = max(16, -(-nq // 8) * 8)
    return pl.pallas_call(
        _moba_kernel,
        grid=(bsz, ns, nq),
        in_specs=[
            pl.BlockSpec(memory_space=pltpu.SMEM),
            pl.BlockSpec((blk, hw), lambda b, h, i: (b * nq + i, h)),
            pl.BlockSpec((seq, hw), lambda b, h, i: (b, ns + h)),
            pl.BlockSpec((seq, hw), lambda b, h, i: (b, 2 * ns + h)),
            pl.BlockSpec((blk, hw), lambda b, h, i: (b * nq + i, 3 * ns + h)),
            pl.BlockSpec((MOBA_HEADS_PER_STEP, blk, 2 * blk), lambda b, h, i: (h, 0, 0)),
        ],
        out_specs=pl.BlockSpec((blk, hw), lambda b, h, i: (b * nq + i, h)),
        out_shape=jax.ShapeDtypeStruct((bsz * seq, A_WIDTH), BF16),
        scratch_shapes=[pltpu.VMEM((MOBA_HEADS_PER_STEP, nbp, A_HEAD_DIM), F32)],
        compiler_params=_params(3),
        name="moba_attention",
    )(rel_bias, p_a, p_a, p_a, p_a, bias_tiles)


def _cross_kernel(q_ref, k_ref, v_ref, z_ref, o_ref):
    s = lax.dot_general(q_ref[...], k_ref[...], NT, preferred_element_type=F32) * (C_HEAD_DIM ** -0.5)
    m = jnp.max(s, axis=-1, keepdims=True)
    p = jnp.exp(s - m)
    l = jnp.sum(p, axis=-1, keepdims=True)
    y = jnp.dot(p.astype(BF16), v_ref[...], preferred_element_type=F32) / l
    o_ref[...] = (y * _silu(z_ref[...].astype(F32))).astype(o_ref.dtype)


def _cross(p_c, kv, bsz, seq, mem_len, tq):
    hd = C_HEAD_DIM
    nt = seq // tq
    return pl.pallas_call(
        _cross_kernel,
        grid=(bsz, C_HEADS, nt),
        in_specs=[
            pl.BlockSpec((tq, hd), lambda b, h, i: (b * nt + i, h)),
            pl.BlockSpec((mem_len, hd), lambda b, h, i: (b, h)),
            pl.BlockSpec((mem_len, hd), lambda b, h, i: (b, C_HEADS + h)),
            pl.BlockSpec((tq, hd), lambda b, h, i: (b * nt + i, C_HEADS + h)),
        ],
        out_specs=pl.BlockSpec((tq, hd), lambda b, h, i: (b * nt + i, h)),
        out_shape=jax.ShapeDtypeStruct((bsz * seq, C_WIDTH), BF16),
        compiler_params=_params(3),
        name="memory_attention",
    )(p_c, kv, kv, p_c)


class _WkvConsts:
    def __init__(self):
        c, g, hd = WKV_CHUNK, WKV_GROUP, B_HEAD_DIM
        row = lax.broadcasted_iota(jnp.int32, (g, g), 0)
        col = lax.broadcasted_iota(jnp.int32, (g, g), 1)
        self.same_head = (row // hd) == (col // hd)
        self.ones_bd = jnp.where(self.same_head, 1.0, 0.0).astype(BF16)
        t = lax.broadcasted_iota(jnp.int32, (c, g), 0)
        s = lax.broadcasted_iota(jnp.int32, (c, g), 1) % hd
        self.strict = s < t
        self.incl = s <= t
        self.eye = jnp.where(s == t, 1.0, 0.0).astype(F32)
        tr = lax.broadcasted_iota(jnp.int32, (c, c), 0)
        tc = lax.broadcasted_iota(jnp.int32, (c, c), 1)
        self.lower = jnp.where(tc <= tr, 1.0, 0.0).astype(BF16)

    def bd(self, p):
        tiled = jnp.concatenate([p] * WKV_HEADS_PER_GROUP, axis=0)
        return jnp.where(self.same_head, tiled, jnp.zeros((), p.dtype))

    def segsum(self, x):
        return jnp.dot(_bf(x), self.ones_bd, preferred_element_type=F32)


def _wkv_chunk(rm, kmod, vm, avec, bvec, logdec, state, k):
    c = WKV_CHUNK
    cat = lambda u, w: jnp.concatenate([u, w], axis=0)
    ld_hi = _each(_bf, logdec)
    ld_lo = _each(lambda x, hi: _bf(x - hi.astype(F32)), logdec, ld_hi)
    cumsum = lambda hi, lo: jnp.dot(k.lower, hi, preferred_element_type=F32) + jnp.dot(k.lower, lo, preferred_element_type=F32)
    cl = _each(cumsum, ld_hi, ld_lo)
    cl_last = _each(lambda x: x[c - 1:c, :], cl)
    rt = _each(lambda r, x: _bf(r * jnp.exp(x)), rm, cl)
    at = _each(lambda a, x, ld: _bf(a * jnp.exp(x - ld)), avec, cl, logdec)
    e_neg = _each(lambda x: jnp.exp(-x), cl)
    bt = _each(lambda b, e: k.bd(_bf(b * e)), bvec, e_neg)
    kt = _each(lambda kk, e: k.bd(_bf(kk * e)), kmod, e_neg)
    e_rem = _each(lambda xl, x: jnp.exp(xl - x), cl_last, cl)
    bh = _each(lambda b, e: _bf(b * e), bvec, e_rem)
    kh = _each(lambda kk, e: _bf(kk * e), kmod, e_rem)
    vb = _each(_bf, vm)

    lhs = _each(cat, at, rt)
    ab = _each(lambda x, y: _mm(x, y, NT), lhs, bt)
    ak = _each(lambda x, y: _mm(x, y, NT), lhs, kt)
    a_ab = _each(lambda x: jnp.where(k.strict, x[:c], 0.0), ab)
    a_rb = _each(lambda x: jnp.where(k.incl, x[c:], 0.0), ab)
    a_ak = _each(lambda x: jnp.where(k.strict, x[:c], 0.0), ak)
    a_rk = _each(lambda x: jnp.where(k.incl, x[c:], 0.0), ak)

    inv = _each(lambda x: k.eye + x, a_ab)
    pw = _each(_bf, a_ab)
    pw = _each(lambda p: _bf(_mm(p, k.bd(p), NN)), pw)
    for _ in range(int(math.log2(c)) - 2):
        tp = _each(lambda i, p: _mm(cat(_bf(i), p), k.bd(p), NN), inv, pw)
        inv = _each(lambda i, t: i + t[:c], inv, tp)
        pw = _each(lambda t: _bf(t[c:]), tp)
    inv = _each(lambda i, p: i + _mm(_bf(i), k.bd(p), NN), inv, pw)
    invb = _each(_bf, inv)

    av = _each(lambda x, y, v: _mm(cat(_bf(x), _bf(y)), k.bd(v), NN), a_ak, a_rk, vb)
    w = _each(lambda i, a: _mm(i, k.bd(a), NN), invb, at)
    u0 = _each(lambda i, x: _mm(i, k.bd(_bf(x[:c])), NN), invb, av)

    uy = _each(lambda ww, r, s: _mm(cat(_bf(ww), r), _bf(s), NT), w, rt, state)
    u = _each(lambda x, y: x[:c] + y, uy, u0)
    ub = _each(_bf, u)
    y = _each(lambda x, a, uu, z: x[c:] + _mm(_bf(a), k.bd(uu), NN) + z[c:], uy, a_rb, ub, av)
    upd = _each(lambda uu, v, b, kk: _mm(cat(uu, v), cat(b, kk), TN), ub, vb, bh, kh)
    new_state = _each(lambda s, xl, d: s * jnp.exp(xl) + jnp.where(k.same_head, d, 0.0), state, cl_last, upd)
    return y, new_state


def _rwkv_kernel(r_ref, k_ref, v_ref, lwla_ref, z_ref, mu_r_ref, mu_k_ref, mu_v_ref, mu_wa_ref, w0_ref, a0_ref,
                 kk_ref, ka_ref, rk_ref, lnw_ref, lnb_ref, wd_ref, wa_ref, o_ref, state_ref, prev_ref, prev_lora_ref):
    c, g = WKV_CHUNK, WKV_GROUP
    n_groups = r_ref.shape[1] // g

    @pl.when(pl.program_id(2) == 0)
    def _():
        state_ref[...] = jnp.zeros_like(state_ref)
        prev_ref[...] = jnp.zeros_like(prev_ref)
        prev_lora_ref[...] = jnp.zeros_like(prev_lora_ref)

    k = _WkvConsts()
    wd = _bf(wd_ref[...])
    wa = _bf(wa_ref[...])
    inv_hd = 1.0 / B_HEAD_DIM
    groups = [slice(gi * g, (gi + 1) * g) for gi in range(n_groups)]
    split = lambda x: [x[:, ln] for ln in groups]

    def mix(x, p_ref, slot, mu_ref):
        first_row = lax.broadcasted_iota(jnp.int32, x.shape, 0) == 0
        prev = jnp.where(first_row, p_ref[slot:slot + 1, :], pltpu.roll(x, 1, 0))
        p_ref[slot:slot + 1, :] = x[c - 1:c, :]
        return x + (prev - x) * mu_ref[...]

    def chunk(ci, carry):
        rows = pl.ds(pl.multiple_of(ci * c, c), c)
        rm = mix(r_ref[rows, :], prev_ref, 0, mu_r_ref)
        km = mix(k_ref[rows, :], prev_ref, 8, mu_k_ref)
        vm = mix(v_ref[rows, :], prev_ref, 16, mu_v_ref)
        lwla = mix(lwla_ref[rows, :], prev_lora_ref, 0, mu_wa_ref)
        lw = _bf(jnp.tanh(lwla[:, :LORA_PAD]))
        la = _bf(lwla[:, LORA_PAD:])
        logdec = -math.exp(-0.5) * jax.nn.sigmoid(w0_ref[...] + jnp.dot(lw, wd, preferred_element_type=F32))
        a_lr = jax.nn.sigmoid(a0_ref[...] + jnp.dot(la, wa, preferred_element_type=F32))
        kmod = km * (1.0 + (a_lr - 1.0) * ka_ref[...])
        kk = split(km * kk_ref[...])
        kk = _each(lambda x, ss: x / jnp.maximum(jnp.sqrt(ss), 1e-12), kk, _each(lambda x: k.segsum(x * x), kk))
        bonus = _each(lambda x, v: k.segsum(x) * v, split(rm * kmod * rk_ref[...]), split(vm))
        state = [state_ref[gi] for gi in range(n_groups)]
        y, new_state = _wkv_chunk(split(rm), split(kmod), split(vm), _each(lambda x: -x, kk),
                                  _each(lambda x, a: x * a, kk, split(a_lr)), split(logdec), state, k)
        for gi in range(n_groups):
            state_ref[gi] = new_state[gi]
        yc = _each(lambda x, mean: x - mean * inv_hd, y, _each(k.segsum, y))
        var = _each(lambda x: k.segsum(x * x) * inv_hd, yc)
        gate = _silu(z_ref[rows, :].astype(F32))
        for gi, ln in enumerate(groups):
            yn = yc[gi] * lax.rsqrt(var[gi] + LNX_EPS) * lnw_ref[:, ln] + lnb_ref[:, ln]
            o_ref[rows, ln] = ((yn + bonus[gi]) * gate[:, ln]).astype(o_ref.dtype)
        return carry

    lax.fori_loop(0, r_ref.shape[0] // c, chunk, 0)


def _rwkv(p_b, p_lora, p_zb, vecs, wd2, wa2, bsz, seq, tt):
    g = WKV_GROUP
    gw = WKV_GROUPS_PER_STEP * g
    ns = B_WIDTH // gw
    nt = seq // tt
    row = lambda b, gi, t: b * nt + t
    vec_spec = pl.BlockSpec((1, gw), lambda b, gi, t: (0, gi))
    lora_vec_spec = pl.BlockSpec((1, 2 * LORA_PAD), lambda b, gi, t: (0, 0))
    lora_w_spec = pl.BlockSpec((LORA_PAD, gw), lambda b, gi, t: (0, gi))
    mu_r, mu_k, mu_v, mu_wa, w0, a0, k_k, k_a, r_k, lnw, lnb = vecs
    return pl.pallas_call(
        _rwkv_kernel,
        grid=(bsz, ns, nt),
        in_specs=[
            pl.BlockSpec((tt, gw), lambda b, gi, t: (row(b, gi, t), gi)),
            pl.BlockSpec((tt, gw), lambda b, gi, t: (row(b, gi, t), ns + gi)),
            pl.BlockSpec((tt, gw), lambda b, gi, t: (row(b, gi, t), 2 * ns + gi)),
            pl.BlockSpec((tt, 2 * LORA_PAD), lambda b, gi, t: (row(b, gi, t), 0)),
            pl.BlockSpec((tt, gw), lambda b, gi, t: (row(b, gi, t), gi)),
            vec_spec, vec_spec, vec_spec, lora_vec_spec, vec_spec, vec_spec, vec_spec, vec_spec, vec_spec, vec_spec, vec_spec,
            lora_w_spec, lora_w_spec,
        ],
        out_specs=pl.BlockSpec((tt, gw), lambda b, gi, t: (row(b, gi, t), gi)),
        out_shape=jax.ShapeDtypeStruct((bsz * seq, B_WIDTH), BF16),
        scratch_shapes=[pltpu.VMEM((WKV_GROUPS_PER_STEP, g, g), F32), pltpu.VMEM((24, gw), F32), pltpu.VMEM((8, 2 * LORA_PAD), F32)],
        compiler_params=_params(3),
        name="rwkv7_time_mix",
    )(p_b, p_b, p_b, p_lora, p_zb, mu_r, mu_k, mu_v, mu_wa, w0, a0, k_k, k_a, r_k, lnw, lnb, wd2, wa2)


def _merge_kernel(ya_ref, yb_ref, yc_ref, wa_ref, wb_ref, wc_ref, ga_ref, gb_ref, gc_ref, o_ref):
    def branch(y_ref, w_ref, g_ref):
        return jax.nn.sigmoid(g_ref[...].astype(F32)) * jnp.dot(y_ref[...], w_ref[...], preferred_element_type=F32)

    o_ref[...] = (branch(ya_ref, wa_ref, ga_ref) + branch(yb_ref, wb_ref, gb_ref) + branch(yc_ref, wc_ref, gc_ref)).astype(o_ref.dtype)


def _merge(ya, yb, yc, wa, wb, wc, p_c, gate_col0, d_model, tm, tn):
    m = ya.shape[0]
    nj = d_model // tn
    j0 = gate_col0 // tn
    y_spec = lambda width: pl.BlockSpec((tm, width), lambda j, i: (i, 0))
    w_spec = lambda width: pl.BlockSpec((width, tn), lambda j, i: (0, j))
    g_spec = lambda br: pl.BlockSpec((tm, tn), lambda j, i: (i, j0 + br * nj + j))
    return pl.pallas_call(
        _merge_kernel,
        grid=(nj, m // tm),
        in_specs=[y_spec(A_WIDTH), y_spec(B_WIDTH), y_spec(C_WIDTH), w_spec(A_WIDTH), w_spec(B_WIDTH), w_spec(C_WIDTH),
                  g_spec(0), g_spec(1), g_spec(2)],
        out_specs=pl.BlockSpec((tm, tn), lambda j, i: (i, j)),
        out_shape=jax.ShapeDtypeStruct((m, d_model), BF16),
        compiler_params=_params(2),
        name="gated_merge",
    )(ya, yb, yc, wa, wb, wc, p_c, p_c, p_c)


def _out_kernel(m_ref, w_ref, x_ref, g_ref, o_ref, *, final_norm):
    y = x_ref[...] + jnp.dot(m_ref[...], w_ref[...], preferred_element_type=F32)
    if final_norm:
        y = y * lax.rsqrt(jnp.mean(y * y, axis=-1, keepdims=True) + RMS_EPS) * g_ref[...]
    o_ref[...] = y


def _out_proj(merged, w_out, x2d, g, final_norm, tm):
    m, d = x2d.shape
    return pl.pallas_call(
        functools.partial(_out_kernel, final_norm=final_norm),
        grid=(m // tm,),
        in_specs=[pl.BlockSpec((tm, d), lambda i: (i, 0)), pl.BlockSpec((d, d), lambda i: (0, 0)),
                  pl.BlockSpec((tm, d), lambda i: (i, 0)), pl.BlockSpec((1, d), lambda i: (0, 0))],
        out_specs=pl.BlockSpec((tm, d), lambda i: (i, 0)),
        out_shape=jax.ShapeDtypeStruct((m, d), F32),
        compiler_params=_params(1),
        name="out_proj",
    )(merged, w_out, x2d, g.reshape(1, d))


def _pick_tile(n, candidates):
    for t in candidates:
        if n % t == 0:
            return t
    raise ValueError(f"no tile for extent {n}")


def _pad_cols(w, width):
    return jnp.pad(w, ((0, 0), (0, width - w.shape[1])))


def _layer(x2d, mem2d, bsz, seq, mem_len, bias_tiles, rel_bias, norm_g, mem_norm_g, w_in, rw, w_mem_kv, w_proj_a, w_proj_b,
           w_proj_c, w_out, final_g):
    m, d = x2d.shape
    tm = _pick_tile(m, (1024, 512, 256))

    c_z_a = 3 * A_WIDTH
    c_rkv_b = c_z_a + A_WIDTH
    c_z_b = c_rkv_b + 3 * B_WIDTH
    c_lw = c_z_b + B_WIDTH
    c_la = c_lw + LORA
    c_q_c = c_la + LORA
    c_gates = c_q_c + 2 * C_WIDTH

    h = _rmsnorm(x2d, norm_g, BF16, tm)
    ones = lambda n: jnp.ones((1, n), F32)
    scale_a = jnp.concatenate([jnp.full((1, A_WIDTH), MOBA_Q_SCALE, F32), ones(3 * A_WIDTH)], axis=1)
    p_a = _matmul_w32(h, w_in, 0, 4 * A_WIDTH, scale_a, BF16, tm, 1024, "in_proj_a")
    p_b = _matmul_w32(h, w_in, c_rkv_b, 3 * B_WIDTH, ones(3 * B_WIDTH), F32, tm, 768, "in_proj_b")
    p_zb = _matmul_w32(h, w_in, c_z_b, B_WIDTH, ones(B_WIDTH), BF16, tm, 768, "in_proj_zb")
    w_lora = jnp.concatenate([_pad_cols(w_in[:, c_lw:c_la], LORA_PAD), _pad_cols(w_in[:, c_la:c_q_c], LORA_PAD)], axis=1)
    p_lora = _matmul_w32(h, w_lora, 0, 2 * LORA_PAD, ones(2 * LORA_PAD), F32, tm, 2 * LORA_PAD, "in_proj_lora")
    p_c = _matmul_w32(h, w_in[:, c_q_c:], 0, 2 * C_WIDTH + N_BRANCHES * d, ones(2 * C_WIDTH + N_BRANCHES * d), BF16, tm, 1024, "in_proj_c")

    mem_n = _rmsnorm(mem2d, mem_norm_g, BF16, _pick_tile(mem2d.shape[0], (1024, 512, 256)))
    kv = _matmul(mem_n, w_mem_kv.astype(BF16), BF16, _pick_tile(mem2d.shape[0], (1024, 512, 256)), 1024, "mem_kv")

    ya = _moba(p_a, rel_bias, bias_tiles, bsz, seq)
    yc = _cross(p_c, kv, bsz, seq, mem_len, _pick_tile(seq, (1024, 512, 256)))

    (mu_r, mu_k, mu_v, mu_w, mu_a, w0, w_decay2, a0, w_aaa2, k_k, k_a, r_k, lnx_w, lnx_b) = rw
    vec = lambda v: v.reshape(1, B_WIDTH)
    lora_vec = lambda v: _pad_cols(v.reshape(1, LORA), LORA_PAD)
    mu_wa = jnp.concatenate([lora_vec(mu_w), lora_vec(mu_a)], axis=1)
    lora_w = lambda w: jnp.pad(w, ((0, LORA_PAD - LORA), (0, 0)))
    vecs = (vec(mu_r), vec(mu_k), vec(mu_v), mu_wa, vec(w0), vec(a0), vec(k_k), vec(k_a), vec(r_k), vec(lnx_w), vec(lnx_b))
    yb = _rwkv(p_b, p_lora, p_zb, vecs, lora_w(w_decay2), lora_w(w_aaa2), bsz, seq, _pick_tile(seq, (512, 256, 128, 64)))

    merged = _merge(ya, yb, yc, w_proj_a.astype(BF16), w_proj_b.astype(BF16), w_proj_c.astype(BF16), p_c, 2 * C_WIDTH, d,
                    _pick_tile(m, (512, 256)), 1024)
    g = final_g if final_g is not None else jnp.ones((d,), F32)
    return _out_proj(merged, w_out.astype(BF16), x2d, g, final_g is not None, _pick_tile(m, (512, 256)))


def kernel(x, mem, rel_bias, norm_g, mem_norm_g, w_in, rw_mu_r, rw_mu_k, rw_mu_v, rw_mu_w, rw_mu_a, rw_w0, rw_w_decay2, rw_a0, rw_w_aaa2, rw_k_k, rw_k_a, rw_r_k, rw_lnx_w, rw_lnx_b, w_mem_kv, w_proj_a, w_proj_b, w_proj_c, w_out, final_norm_g):
    bsz, seq, d = x.shape
    mem_len = mem.shape[1]
    depth = norm_g.shape[0]
    x2d = x.reshape(bsz * seq, d)
    mem2d = mem.reshape(bsz * mem_len, d)
    bias_tiles = _bias_tiles(rel_bias)
    for l in range(depth):
        rw = (rw_mu_r[l], rw_mu_k[l], rw_mu_v[l], rw_mu_w[l], rw_mu_a[l], rw_w0[l], rw_w_decay2[l], rw_a0[l], rw_w_aaa2[l],
              rw_k_k[l], rw_k_a[l], rw_r_k[l], rw_lnx_w[l], rw_lnx_b[l])
        x2d = _layer(x2d, mem2d, bsz, seq, mem_len, bias_tiles, rel_bias, norm_g[l], mem_norm_g[l], w_in[l], rw, w_mem_kv[l],
                     w_proj_a[l], w_proj_b[l], w_proj_c[l], w_out[l], final_norm_g if l == depth - 1 else None)
    return x2d.reshape(bsz, seq, d)
```

```python
import functools
import math

import jax
import jax.numpy as jnp
from jax import lax
from jax.experimental import pallas as pl
from jax.experimental.pallas import tpu as pltpu

F32 = jnp.float32
BF16 = jnp.bfloat16

RMS_EPS = 1e-6

A_HEADS = 12
A_HEAD_DIM = 128
A_WIDTH = A_HEADS * A_HEAD_DIM
MOBA_BLOCK = 256
MOBA_TOPK = 3
MOBA_HEADS_PER_STEP = 2
MOBA_FAR_GROUP = 4
MOBA_Q_SCALE = A_HEAD_DIM ** -0.5 * math.log2(math.e)
REL_BUCKETS = 32
REL_MAX_DIST = 128

B_HEAD_DIM = 64
B_WIDTH = 1536
LORA = 96
LORA_PAD = 128
LNX_EPS = 64e-5
WKV_CHUNK = 64
WKV_GROUP = 256
WKV_HEADS_PER_GROUP = WKV_GROUP // B_HEAD_DIM
WKV_GROUPS_PER_STEP = 6

C_HEADS = 4
C_HEAD_DIM = 256
C_WIDTH = C_HEADS * C_HEAD_DIM

N_BRANCHES = 3

VMEM_LIMIT = 48 * 1024 * 1024

NN = (((1,), (0,)), ((), ()))
NT = (((1,), (1,)), ((), ()))
TN = (((0,), (0,)), ((), ()))

MASKED = -1e30
LOG2E = math.log2(math.e)


def _params(n_axes):
    return pltpu.CompilerParams(dimension_semantics=("arbitrary",) * n_axes, vmem_limit_bytes=VMEM_LIMIT)


def _silu(z):
    return z * jax.nn.sigmoid(z)


def _bf(x):
    return x.astype(BF16)


def _mm(a, b, dn):
    return lax.dot_general(a, b, dn, preferred_element_type=F32)


def _each(f, *lists):
    return [f(*args) for args in zip(*lists)]


def _rmsnorm_kernel(x_ref, g_ref, o_ref):
    x = x_ref[...].astype(F32)
    y = x * lax.rsqrt(jnp.mean(x * x, axis=-1, keepdims=True) + RMS_EPS)
    o_ref[...] = (y * g_ref[...]).astype(o_ref.dtype)


def _rmsnorm(x2d, g, out_dtype, tm):
    m, d = x2d.shape
    return pl.pallas_call(
        _rmsnorm_kernel,
        grid=(m // tm,),
        in_specs=[pl.BlockSpec((tm, d), lambda i: (i, 0)), pl.BlockSpec((1, d), lambda i: (0, 0))],
        out_specs=pl.BlockSpec((tm, d), lambda i: (i, 0)),
        out_shape=jax.ShapeDtypeStruct((m, d), out_dtype),
        compiler_params=_params(1),
        name="rmsnorm",
    )(x2d, g.reshape(1, d))


def _matmul_kernel(a_ref, w_ref, o_ref):
    o_ref[...] = jnp.dot(a_ref[...], w_ref[...], preferred_element_type=F32).astype(o_ref.dtype)


def _matmul(a, w, out_dtype, tm, tn, name):
    m, k = a.shape
    n = w.shape[1]
    return pl.pallas_call(
        _matmul_kernel,
        grid=(n // tn, m // tm),
        in_specs=[pl.BlockSpec((tm, k), lambda j, i: (i, 0)), pl.BlockSpec((k, tn), lambda j, i: (0, j))],
        out_specs=pl.BlockSpec((tm, tn), lambda j, i: (i, j)),
        out_shape=jax.ShapeDtypeStruct((m, n), out_dtype),
        compiler_params=_params(2),
        name=name,
    )(a, w)


def _matmul_wt_kernel(a_ref, wt_ref, s_ref, o_ref, wb_ref):
    @pl.when(pl.program_id(1) == 0)
    def _():
        wb_ref[...] = wt_ref[...].astype(BF16)

    acc = lax.dot_general(a_ref[...], wb_ref[...], NT, preferred_element_type=F32)
    o_ref[...] = (acc * s_ref[...]).astype(o_ref.dtype)


def _matmul_wt(a, wt, row0, n, col_scale, out_dtype, tm, tn, name):
    m, k = a.shape
    return pl.pallas_call(
        _matmul_wt_kernel,
        grid=(n // tn, m // tm),
        in_specs=[pl.BlockSpec((tm, k), lambda j, i: (i, 0)),
                  pl.BlockSpec((pl.Element(tn), pl.Element(k)), lambda j, i: (pl.multiple_of(row0 + j * tn, 8), 0)),
                  pl.BlockSpec((1, tn), lambda j, i: (0, j))],
        out_specs=pl.BlockSpec((tm, tn), lambda j, i: (i, j)),
        out_shape=jax.ShapeDtypeStruct((m, n), out_dtype),
        scratch_shapes=[pltpu.VMEM((tn, k), BF16)],
        compiler_params=_params(2),
        name=name,
    )(a, wt, col_scale)


def _t5_bucket(dist):
    n = jnp.maximum(dist, 0)
    max_exact = REL_BUCKETS // 2
    nf = jnp.maximum(n, max_exact).astype(F32)
    large = max_exact + (jnp.log(nf / max_exact) / math.log(REL_MAX_DIST / max_exact) * (REL_BUCKETS - max_exact)).astype(jnp.int32)
    large = jnp.minimum(large, REL_BUCKETS - 1)
    return jnp.where(n < max_exact, n, large)


def _bias_kernel(rel_ref, bucket_ref, o_ref):
    h = pl.program_id(0)
    bucket = bucket_ref[...]
    acc = jnp.zeros(bucket.shape, F32)
    for b in range(REL_BUCKETS):
        acc = jnp.where(bucket == b, rel_ref[b, h], acc)
    qpos = lax.broadcasted_iota(jnp.int32, bucket.shape, 0) + MOBA_BLOCK
    kpos = lax.broadcasted_iota(jnp.int32, bucket.shape, 1)
    o_ref[0] = jnp.where(kpos <= qpos, acc * LOG2E, MASKED)


def _bias_tiles(rel_bias):
    blk = MOBA_BLOCK
    qpos = lax.broadcasted_iota(jnp.int32, (blk, 2 * blk), 0) + blk
    kpos = lax.broadcasted_iota(jnp.int32, (blk, 2 * blk), 1)
    bucket = _t5_bucket(qpos - kpos)
    return pl.pallas_call(
        _bias_kernel,
        grid=(A_HEADS,),
        in_specs=[pl.BlockSpec(memory_space=pltpu.SMEM), pl.BlockSpec((blk, 2 * blk), lambda h: (0, 0))],
        out_specs=pl.BlockSpec((1, blk, 2 * blk), lambda h: (h, 0, 0)),
        out_shape=jax.ShapeDtypeStruct((A_HEADS, blk, 2 * blk), F32),
        compiler_params=_params(1),
        name="t5_bias",
    )(rel_bias, bucket)


def _moba_kernel(rel_ref, q_ref, k_ref, v_ref, z_ref, bias_ref, o_ref, kmean_ref):
    hb = pl.program_id(1)
    qi = pl.program_id(2)
    blk, hd = MOBA_BLOCK, A_HEAD_DIM
    n_heads = q_ref.shape[1] // hd
    nb = k_ref.shape[0] // blk
    nbp = kmean_ref.shape[1]

    @pl.when(qi == 0)
    def _():
        kmean_ref[...] = jnp.zeros_like(kmean_ref)
        for j in range(nb):
            mean_j = jnp.sum(k_ref[j * blk:(j + 1) * blk, :].astype(F32), axis=0, keepdims=True) * (1.0 / blk)
            for hh in range(n_heads):
                kmean_ref[hh, j:j + 1, :] = mean_j[:, hh * hd:(hh + 1) * hd]

    n_far = min(MOBA_FAR_GROUP, nb)
    never = 127
    blk_id = lax.broadcasted_iota(jnp.int32, (nbp, blk), 0)
    blk_f = blk_id.astype(F32)
    eye = jnp.where(lax.broadcasted_iota(jnp.int32, (nbp, 128), 0) == lax.broadcasted_iota(jnp.int32, (nbp, 128), 1), 1.0, 0.0).astype(BF16)
    lane = lax.broadcasted_iota(jnp.int32, (blk, 128), 1)
    j_prev = jnp.maximum(qi - 1, 0)
    own = pl.multiple_of(qi * blk, blk)
    prev = pl.multiple_of(j_prev * blk, blk)

    heads = [slice(hh * hd, (hh + 1) * hd) for hh in range(n_heads)]
    q = [q_ref[:, hl] for hl in heads]

    def gate_of(qh, hh):
        km = kmean_ref[hh]
        km_hi = km.astype(BF16)
        km_lo = (km - km_hi.astype(F32)).astype(BF16)
        return jnp.where(blk_id < qi, _mm(km_hi, qh, NT) + _mm(km_lo, qh, NT), -jnp.inf)

    g = _each(gate_of, q, range(n_heads))
    sel_t = [jnp.zeros((nbp, blk), F32) for _ in heads]
    for _ in range(MOBA_TOPK):
        gmax = _each(lambda x: jnp.max(x, axis=0, keepdims=True), g)
        first = _each(lambda x, mx: jnp.min(jnp.where(x == mx, blk_f, float(nbp)), axis=0, keepdims=True), g, gmax)
        pick = _each(lambda f, mx: (blk_f == f) & (mx > -jnp.inf), first, gmax)
        sel_t = _each(lambda p, s: jnp.where(p, 1.0, s), pick, sel_t)
        g = _each(lambda p, x: jnp.where(p, -jnp.inf, x), pick, g)
    sel = _each(lambda s: _mm(s.astype(BF16), eye, TN), sel_t)
    qm = _each(lambda qh, s: jnp.concatenate([qh, jnp.where(s > 0.0, 0.0, MASKED).astype(BF16)], axis=1), q, sel)

    def masked_logits(keys, key_blk):
        col = jnp.where(lax.broadcasted_iota(jnp.int32, key_blk.shape, 1) == key_blk, 1.0, 0.0).astype(BF16)
        return _each(lambda x, kk: _mm(x, jnp.concatenate([kk, col], axis=1), NT), qm, keys)

    def pv(p, values):
        ones = jnp.ones(values[0].shape, BF16)
        return _each(lambda x, v: jnp.dot(x.astype(BF16), jnp.concatenate([v, ones], axis=1), preferred_element_type=F32), p, values)

    cat0 = lambda ref, hl: jnp.concatenate([ref[pl.ds(prev, blk), hl], ref[pl.ds(own, blk), hl]], axis=0)
    key_row = lax.broadcasted_iota(jnp.int32, (2 * blk, 128), 0)
    raw = masked_logits([cat0(k_ref, hl) for hl in heads], jnp.where(key_row < blk, j_prev, -1))
    t = _each(lambda x, hh: x + bias_ref[hh], raw, range(n_heads))
    m = _each(lambda x: jnp.max(x, axis=-1, keepdims=True), t)
    p = _each(lambda x, mx: jnp.exp2(x - mx), t, m)
    acc = pv(p, [cat0(v_ref, hl) for hl in heads])

    bias_far = [rel_ref[REL_BUCKETS - 1, hb * n_heads + hh] * LOG2E for hh in range(n_heads)]
    far_row_blk = lax.broadcasted_iota(jnp.int32, (n_far * blk, 128), 0) // blk

    def body(gi, carry):
        m, acc = list(carry[:n_heads]), list(carry[n_heads:])
        rows = pl.ds(pl.multiple_of(gi * (n_far * blk), n_far * blk), n_far * blk)
        key_blk = gi * n_far + far_row_blk
        raw = masked_logits([k_ref[rows, hl] for hl in heads], jnp.where(key_blk < j_prev, key_blk, never))
        m_new = _each(lambda mx, x, b: jnp.maximum(mx, jnp.max(x, axis=-1, keepdims=True) + b), m, raw, bias_far)
        p = _each(lambda x, mn, b: jnp.exp2(x + (b - mn)), raw, m_new, bias_far)
        alpha = _each(lambda mx, mn: jnp.exp2(mx - mn), m, m_new)
        acc = _each(lambda a, c, x: a * c + x, alpha, acc, pv(p, [v_ref[rows, hl] for hl in heads]))
        return tuple(m_new + acc)

    acc = lax.fori_loop(0, (j_prev + n_far - 1) // n_far, body, tuple(m + acc))[n_heads:]
    for hh, hl in enumerate(heads):
        y = acc[hh][:, :hd] / acc[hh][:, hd:]
        o_ref[:, hl] = (y * _silu(z_ref[:, hl].astype(F32))).astype(o_ref.dtype)


def _moba(p_a, rel_bias, bias_tiles, bsz, seq):
    blk = MOBA_BLOCK
    nq = seq // blk
    hw = MOBA_HEADS_PER_STEP * A_HEAD_DIM
    ns = A_WIDTH // hw
    nbp = max(16, -(-nq // 8) * 8)
    return pl.pallas_call(
        _moba_kernel,
        grid=(bsz, ns, nq),
        in_specs=[
            pl.BlockSpec(memory_space=pltpu.SMEM),
            pl.BlockSpec((blk, hw), lambda b, h, i: (b * nq + i, h)),
            pl.BlockSpec((seq, hw), lambda b, h, i: (b, ns + h)),
            pl.BlockSpec((seq, hw), lambda b, h, i: (b, 2 * ns + h)),
            pl.BlockSpec((blk, hw), lambda b, h, i: (b * nq + i, 3 * ns + h)),
            pl.BlockSpec((MOBA_HEADS_PER_STEP, blk, 2 * blk), lambda b, h, i: (h, 0, 0)),
        ],
        out_specs=pl.BlockSpec((blk, hw), lambda b, h, i: (b * nq + i, h)),
        out_shape=jax.ShapeDtypeStruct((bsz * seq, A_WIDTH), BF16),
        scratch_shapes=[pltpu.VMEM((MOBA_HEADS_PER_STEP, nbp, A_HEAD_DIM), F32)],
        compiler_params=_params(3),
        name="moba_attention",
    )(rel_bias, p_a, p_a, p_a, p_a, bias_tiles)


def _cross_kernel(q_ref, k_ref, v_ref, z_ref, o_ref):
    s = lax.dot_general(q_ref[...], k_ref[...], NT, preferred_element_type=F32) * (C_HEAD_DIM ** -0.5)
    m = jnp.max(s, axis=-1, keepdims=True)
    p = jnp.exp(s - m)
    l = jnp.sum(p, axis=-1, keepdims=True)
    y = jnp.dot(p.astype(BF16), v_ref[...], preferred_element_type=F32) / l
    o_ref[...] = (y * _silu(z_ref[...].astype(F32))).astype(o_ref.dtype)


def _cross(p_c, kv, bsz, seq, mem_len, tq):
    hd = C_HEAD_DIM
    nt = seq // tq
    return pl.pallas_call(
        _cross_kernel,
        grid=(bsz, C_HEADS, nt),
        in_specs=[
            pl.BlockSpec((tq, hd), lambda b, h, i: (b * nt + i, h)),
            pl.BlockSpec((mem_len, hd), lambda b, h, i: (b, h)),
            pl.BlockSpec((mem_len, hd), lambda b, h, i: (b, C_HEADS + h)),
            pl.BlockSpec((tq, hd), lambda b, h, i: (b * nt + i, C_HEADS + h)),
        ],
        out_specs=pl.BlockSpec((tq, hd), lambda b, h, i: (b * nt + i, h)),
        out_shape=jax.ShapeDtypeStruct((bsz * seq, C_WIDTH), BF16),
        compiler_params=_params(3),
        name="memory_attention",
    )(p_c, kv, kv, p_c)


class _WkvConsts:
    def __init__(self):
        c, g, hd = WKV_CHUNK, WKV_GROUP, B_HEAD_DIM
        row = lax.broadcasted_iota(jnp.int32, (g, g), 0)
        col = lax.broadcasted_iota(jnp.int32, (g, g), 1)
        self.same_head = (row // hd) == (col // hd)
        self.ones_bd = jnp.where(self.same_head, 1.0, 0.0).astype(BF16)
        t = lax.broadcasted_iota(jnp.int32, (c, g), 0)
        s = lax.broadcasted_iota(jnp.int32, (c, g), 1) % hd
        self.strict = s < t
        self.incl = s <= t
        self.eye = jnp.where(s == t, 1.0, 0.0).astype(F32)
        tr = lax.broadcasted_iota(jnp.int32, (c, c), 0)
        tc = lax.broadcasted_iota(jnp.int32, (c, c), 1)
        self.lower = jnp.where(tc <= tr, 1.0, 0.0).astype(BF16)

    def bd(self, p):
        tiled = jnp.concatenate([p] * WKV_HEADS_PER_GROUP, axis=0)
        return jnp.where(self.same_head, tiled, jnp.zeros((), p.dtype))

    def segsum(self, x):
        return jnp.dot(_bf(x), self.ones_bd, preferred_element_type=F32)


def _wkv_chunk(rm, kmod, vm, avec, bvec, logdec, state, k):
    c = WKV_CHUNK
    cat = lambda u, w: jnp.concatenate([u, w], axis=0)
    ld_hi = _each(_bf, logdec)
    ld_lo = _each(lambda x, hi: _bf(x - hi.astype(F32)), logdec, ld_hi)
    cumsum = lambda hi, lo: jnp.dot(k.lower, hi, preferred_element_type=F32) + jnp.dot(k.lower, lo, preferred_element_type=F32)
    cl = _each(cumsum, ld_hi, ld_lo)
    cl_last = _each(lambda x: x[c - 1:c, :], cl)
    rt = _each(lambda r, x: _bf(r * jnp.exp(x)), rm, cl)
    at = _each(lambda a, x, ld: _bf(a * jnp.exp(x - ld)), avec, cl, logdec)
    e_neg = _each(lambda x: jnp.exp(-x), cl)
    bt = _each(lambda b, e: k.bd(_bf(b * e)), bvec, e_neg)
    kt = _each(lambda kk, e: k.bd(_bf(kk * e)), kmod, e_neg)
    e_rem = _each(lambda xl, x: jnp.exp(xl - x), cl_last, cl)
    bh = _each(lambda b, e: _bf(b * e), bvec, e_rem)
    kh = _each(lambda kk, e: _bf(kk * e), kmod, e_rem)
    vb = _each(_bf, vm)

    lhs = _each(cat, at, rt)
    ab = _each(lambda x, y: _mm(x, y, NT), lhs, bt)
    ak = _each(lambda x, y: _mm(x, y, NT), lhs, kt)
    a_ab = _each(lambda x: jnp.where(k.strict, x[:c], 0.0), ab)
    a_rb = _each(lambda x: jnp.where(k.incl, x[c:], 0.0), ab)
    a_ak = _each(lambda x: jnp.where(k.strict, x[:c], 0.0), ak)
    a_rk = _each(lambda x: jnp.where(k.incl, x[c:], 0.0), ak)

    inv = _each(lambda x: k.eye + x, a_ab)
    pw = _each(_bf, a_ab)
    pw = _each(lambda p: _bf(_mm(p, k.bd(p), NN)), pw)
    for _ in range(int(math.log2(c)) - 2):
        tp = _each(lambda i, p: _mm(cat(_bf(i), p), k.bd(p), NN), inv, pw)
        inv = _each(lambda i, t: i + t[:c], inv, tp)
        pw = _each(lambda t: _bf(t[c:]), tp)
    inv = _each(lambda i, p: i + _mm(_bf(i), k.bd(p), NN), inv, pw)
    invb = _each(_bf, inv)

    av = _each(lambda x, y, v: _mm(cat(_bf(x), _bf(y)), k.bd(v), NN), a_ak, a_rk, vb)
    w = _each(lambda i, a: _mm(i, k.bd(a), NN), invb, at)
    u0 = _each(lambda i, x: _mm(i, k.bd(_bf(x[:c])), NN), invb, av)

    uy = _each(lambda ww, r, s: _mm(cat(_bf(ww), r), _bf(s), NT), w, rt, state)
    u = _each(lambda x, y: x[:c] + y, uy, u0)
    ub = _each(_bf, u)
    y = _each(lambda x, a, uu, z: x[c:] + _mm(_bf(a), k.bd(uu), NN) + z[c:], uy, a_rb, ub, av)
    upd = _each(lambda uu, v, b, kk: _mm(cat(uu, v), cat(b, kk), TN), ub, vb, bh, kh)
    new_state = _each(lambda s, xl, d: s * jnp.exp(xl) + jnp.where(k.same_head, d, 0.0), state, cl_last, upd)
    return y, new_state


def _rwkv_kernel(r_ref, k_ref, v_ref, lwla_ref, z_ref, mu_r_ref, mu_k_ref, mu_v_ref, mu_wa_ref, w0_ref, a0_ref,
                 kk_ref, ka_ref, rk_ref, lnw_ref, lnb_ref, wd_ref, wa_ref, o_ref, state_ref, prev_ref, prev_lora_ref):
    c, g = WKV_CHUNK, WKV_GROUP
    n_groups = r_ref.shape[1] // g

    @pl.when(pl.program_id(2) == 0)
    def _():
        state_ref[...] = jnp.zeros_like(state_ref)
        prev_ref[...] = jnp.zeros_like(prev_ref)
        prev_lora_ref[...] = jnp.zeros_like(prev_lora_ref)

    k = _WkvConsts()
    wd = _bf(wd_ref[...])
    wa = _bf(wa_ref[...])
    inv_hd = 1.0 / B_HEAD_DIM
    groups = [slice(gi * g, (gi + 1) * g) for gi in range(n_groups)]
    split = lambda x: [x[:, ln] for ln in groups]

    def mix(x, p_ref, slot, mu_ref):
        first_row = lax.broadcasted_iota(jnp.int32, x.shape, 0) == 0
        prev = jnp.where(first_row, p_ref[slot:slot + 1, :], pltpu.roll(x, 1, 0))
        p_ref[slot:slot + 1, :] = x[c - 1:c, :]
        return x + (prev - x) * mu_ref[...]

    def chunk(ci, carry):
        rows = pl.ds(pl.multiple_of(ci * c, c), c)
        rm = mix(r_ref[rows, :], prev_ref, 0, mu_r_ref)
        km = mix(k_ref[rows, :], prev_ref, 8, mu_k_ref)
        vm = mix(v_ref[rows, :], prev_ref, 16, mu_v_ref)
        lwla = mix(lwla_ref[rows, :], prev_lora_ref, 0, mu_wa_ref)
        lw = _bf(jnp.tanh(lwla[:, :LORA_PAD]))
        la = _bf(lwla[:, LORA_PAD:])
        logdec = -math.exp(-0.5) * jax.nn.sigmoid(w0_ref[...] + jnp.dot(lw, wd, preferred_element_type=F32))
        a_lr = jax.nn.sigmoid(a0_ref[...] + jnp.dot(la, wa, preferred_element_type=F32))
        kmod = km * (1.0 + (a_lr - 1.0) * ka_ref[...])
        kk = split(km * kk_ref[...])
        kk = _each(lambda x, ss: x / jnp.maximum(jnp.sqrt(ss), 1e-12), kk, _each(lambda x: k.segsum(x * x), kk))
        bonus = _each(lambda x, v: k.segsum(x) * v, split(rm * kmod * rk_ref[...]), split(vm))
        state = [state_ref[gi] for gi in range(n_groups)]
        y, new_state = _wkv_chunk(split(rm), split(kmod), split(vm), _each(lambda x: -x, kk),
                                  _each(lambda x, a: x * a, kk, split(a_lr)), split(logdec), state, k)
        for gi in range(n_groups):
            state_ref[gi] = new_state[gi]
        yc = _each(lambda x, mean: x - mean * inv_hd, y, _each(k.segsum, y))
        var = _each(lambda x: k.segsum(x * x) * inv_hd, yc)
        gate = _silu(z_ref[rows, :].astype(F32))
        for gi, ln in enumerate(groups):
            yn = yc[gi] * lax.rsqrt(var[gi] + LNX_EPS) * lnw_ref[:, ln] + lnb_ref[:, ln]
            o_ref[rows, ln] = ((yn + bonus[gi]) * gate[:, ln]).astype(o_ref.dtype)
        return carry

    lax.fori_loop(0, r_ref.shape[0] // c, chunk, 0)


def _rwkv(p_b, p_lora, p_zb, vecs, wd2, wa2, bsz, seq, tt):
    g = WKV_GROUP
    gw = WKV_GROUPS_PER_STEP * g
    ns = B_WIDTH // gw
    nt = seq // tt
    row = lambda b, gi, t: b * nt + t
    vec_spec = pl.BlockSpec((1, gw), lambda b, gi, t: (0, gi))
    lora_vec_spec = pl.BlockSpec((1, 2 * LORA_PAD), lambda b, gi, t: (0, 0))
    lora_w_spec = pl.BlockSpec((LORA_PAD, gw), lambda b, gi, t: (0, gi))
    mu_r, mu_k, mu_v, mu_wa, w0, a0, k_k, k_a, r_k, lnw, lnb = vecs
    return pl.pallas_call(
        _rwkv_kernel,
        grid=(bsz, ns, nt),
        in_specs=[
            pl.BlockSpec((tt, gw), lambda b, gi, t: (row(b, gi, t), gi)),
            pl.BlockSpec((tt, gw), lambda b, gi, t: (row(b, gi, t), ns + gi)),
            pl.BlockSpec((tt, gw), lambda b, gi, t: (row(b, gi, t), 2 * ns + gi)),
            pl.BlockSpec((tt, 2 * LORA_PAD), lambda b, gi, t: (row(b, gi, t), 0)),
            pl.BlockSpec((tt, gw), lambda b, gi, t: (row(b, gi, t), gi)),
            vec_spec, vec_spec, vec_spec, lora_vec_spec, vec_spec, vec_spec, vec_spec, vec_spec, vec_spec, vec_spec, vec_spec,
            lora_w_spec, lora_w_spec,
        ],
        out_specs=pl.BlockSpec((tt, gw), lambda b, gi, t: (row(b, gi, t), gi)),
        out_shape=jax.ShapeDtypeStruct((bsz * seq, B_WIDTH), BF16),
        scratch_shapes=[pltpu.VMEM((WKV_GROUPS_PER_STEP, g, g), F32), pltpu.VMEM((24, gw), F32), pltpu.VMEM((8, 2 * LORA_PAD), F32)],
        compiler_params=_params(3),
        name="rwkv7_time_mix",
    )(p_b, p_b, p_b, p_lora, p_zb, mu_r, mu_k, mu_v, mu_wa, w0, a0, k_k, k_a, r_k, lnw, lnb, wd2, wa2)


def _merge_kernel(ya_ref, yb_ref, yc_ref, wa_ref, wb_ref, wc_ref, ga_ref, gb_ref, gc_ref, o_ref):
    def branch(y_ref, w_ref, g_ref):
        return jax.nn.sigmoid(g_ref[...].astype(F32)) * jnp.dot(y_ref[...], w_ref[...], preferred_element_type=F32)

    o_ref[...] = (branch(ya_ref, wa_ref, ga_ref) + branch(yb_ref, wb_ref, gb_ref) + branch(yc_ref, wc_ref, gc_ref)).astype(o_ref.dtype)


def _merge(ya, yb, yc, wa, wb, wc, p_c, gate_col0, d_model, tm, tn):
    m = ya.shape[0]
    nj = d_model // tn
    j0 = gate_col0 // tn
    y_spec = lambda width: pl.BlockSpec((tm, width), lambda j, i: (i, 0))
    w_spec = lambda width: pl.BlockSpec((width, tn), lambda j, i: (0, j))
    g_spec = lambda br: pl.BlockSpec((tm, tn), lambda j, i: (i, j0 + br * nj + j))
    return pl.pallas_call(
        _merge_kernel,
        grid=(nj, m // tm),
        in_specs=[y_spec(A_WIDTH), y_spec(B_WIDTH), y_spec(C_WIDTH), w_spec(A_WIDTH), w_spec(B_WIDTH), w_spec(C_WIDTH),
                  g_spec(0), g_spec(1), g_spec(2)],
        out_specs=pl.BlockSpec((tm, tn), lambda j, i: (i, j)),
        out_shape=jax.ShapeDtypeStruct((m, d_model), BF16),
        compiler_params=_params(2),
        name="gated_merge",
    )(ya, yb, yc, wa, wb, wc, p_c, p_c, p_c)


def _out_kernel(m_ref, w_ref, x_ref, g_ref, o_ref, *, final_norm):
    y = x_ref[...] + jnp.dot(m_ref[...], w_ref[...], preferred_element_type=F32)
    if final_norm:
        y = y * lax.rsqrt(jnp.mean(y * y, axis=-1, keepdims=True) + RMS_EPS) * g_ref[...]
    o_ref[...] = y


def _out_proj(merged, w_out, x2d, g, final_norm, tm):
    m, d = x2d.shape
    return pl.pallas_call(
        functools.partial(_out_kernel, final_norm=final_norm),
        grid=(m // tm,),
        in_specs=[pl.BlockSpec((tm, d), lambda i: (i, 0)), pl.BlockSpec((d, d), lambda i: (0, 0)),
                  pl.BlockSpec((tm, d), lambda i: (i, 0)), pl.BlockSpec((1, d), lambda i: (0, 0))],
        out_specs=pl.BlockSpec((tm, d), lambda i: (i, 0)),
        out_shape=jax.ShapeDtypeStruct((m, d), F32),
        compiler_params=_params(1),
        name="out_proj",
    )(merged, w_out, x2d, g.reshape(1, d))


def _pick_tile(n, candidates):
    for t in candidates:
        if n % t == 0:
            return t
    raise ValueError(f"no tile for extent {n}")


def _pad_cols(w, width):
    return jnp.pad(w, ((0, 0), (0, width - w.shape[1])))


def _layer(x2d, mem2d, bsz, seq, mem_len, bias_tiles, rel_bias, norm_g, mem_norm_g, w_in, rw, w_mem_kv, w_proj_a, w_proj_b,
           w_proj_c, w_out, final_g):
    m, d = x2d.shape
    tm = _pick_tile(m, (1024, 512, 256))

    c_z_a = 3 * A_WIDTH
    c_rkv_b = c_z_a + A_WIDTH
    c_z_b = c_rkv_b + 3 * B_WIDTH
    c_lw = c_z_b + B_WIDTH
    c_la = c_lw + LORA
    c_q_c = c_la + LORA
    c_gates = c_q_c + 2 * C_WIDTH

    h = _rmsnorm(x2d, norm_g, BF16, tm)
    ones = lambda n: jnp.ones((1, n), F32)
    wt = w_in.T
    scale_a = jnp.concatenate([jnp.full((1, A_WIDTH), MOBA_Q_SCALE, F32), ones(3 * A_WIDTH)], axis=1)
    p_a = _matmul_wt(h, wt, 0, 4 * A_WIDTH, scale_a, BF16, tm, 1024, "in_proj_a")
    p_b = _matmul_wt(h, wt, c_rkv_b, 3 * B_WIDTH, ones(3 * B_WIDTH), F32, tm, 768, "in_proj_b")
    p_zb = _matmul_wt(h, wt, c_z_b, B_WIDTH, ones(B_WIDTH), BF16, tm, 768, "in_proj_zb")
    pad_rows = lambda w: jnp.pad(w, ((0, LORA_PAD - LORA), (0, 0)))
    wt_lora = jnp.concatenate([pad_rows(wt[c_lw:c_la]), pad_rows(wt[c_la:c_q_c])], axis=0)
    p_lora = _matmul_wt(h, wt_lora, 0, 2 * LORA_PAD, ones(2 * LORA_PAD), F32, tm, 2 * LORA_PAD, "in_proj_lora")
    n_c = 2 * C_WIDTH + N_BRANCHES * d
    p_c = _matmul_wt(h, wt, c_q_c, n_c, ones(n_c), BF16, tm, 1024, "in_proj_c")

    mem_n = _rmsnorm(mem2d, mem_norm_g, BF16, _pick_tile(mem2d.shape[0], (1024, 512, 256)))
    kv = _matmul(mem_n, w_mem_kv.astype(BF16), BF16, _pick_tile(mem2d.shape[0], (1024, 512, 256)), 1024, "mem_kv")

    ya = _moba(p_a, rel_bias, bias_tiles, bsz, seq)
    yc = _cross(p_c, kv, bsz, seq, mem_len, _pick_tile(seq, (1024, 512, 256)))

    (mu_r, mu_k, mu_v, mu_w, mu_a, w0, w_decay2, a0, w_aaa2, k_k, k_a, r_k, lnx_w, lnx_b) = rw
    vec = lambda v: v.reshape(1, B_WIDTH)
    lora_vec = lambda v: _pad_cols(v.reshape(1, LORA), LORA_PAD)
    mu_wa = jnp.concatenate([lora_vec(mu_w), lora_vec(mu_a)], axis=1)
    lora_w = lambda w: jnp.pad(w, ((0, LORA_PAD - LORA), (0, 0)))
    vecs = (vec(mu_r), vec(mu_k), vec(mu_v), mu_wa, vec(w0), vec(a0), vec(k_k), vec(k_a), vec(r_k), vec(lnx_w), vec(lnx_b))
    yb = _rwkv(p_b, p_lora, p_zb, vecs, lora_w(w_decay2), lora_w(w_aaa2), bsz, seq, _pick_tile(seq, (512, 256, 128, 64)))

    merged = _merge(ya, yb, yc, w_proj_a.astype(BF16), w_proj_b.astype(BF16), w_proj_c.astype(BF16), p_c, 2 * C_WIDTH, d,
                    _pick_tile(m, (512, 256)), 1024)
    g = final_g if final_g is not None else jnp.ones((d,), F32)
    return _out_proj(merged, w_out.astype(BF16), x2d, g, final_g is not None, _pick_tile(m, (512, 256)))


def kernel(x, mem, rel_bias, norm_g, mem_norm_g, w_in, rw_mu_r, rw_mu_k, rw_mu_v, rw_mu_w, rw_mu_a, rw_w0, rw_w_decay2, rw_a0, rw_w_aaa2, rw_k_k, rw_k_a, rw_r_k, rw_lnx_w, rw_lnx_b, w_mem_kv, w_proj_a, w_proj_b, w_proj_c, w_out, final_norm_g):
    bsz, seq, d = x.shape
    mem_len = mem.shape[1]
    depth = norm_g.shape[0]
    x2d = x.reshape(bsz * seq, d)
    mem2d = mem.reshape(bsz * mem_len, d)
    bias_tiles = _bias_tiles(rel_bias)
    for l in range(depth):
        rw = (rw_mu_r[l], rw_mu_k[l], rw_mu_v[l], rw_mu_w[l], rw_mu_a[l], rw_w0[l], rw_w_decay2[l], rw_a0[l], rw_w_aaa2[l],
              rw_k_k[l], rw_k_a[l], rw_r_k[l], rw_lnx_w[l], rw_lnx_b[l])
        x2d = _layer(x2d, mem2d, bsz, seq, mem_len, bias_tiles, rel_bias, norm_g[l], mem_norm_g[l], w_in[l], rw, w_mem_kv[l],
                     w_proj_a[l], w_proj_b[l], w_proj_c[l], w_out[l], final_norm_g if l == depth - 1 else None)
    return x2d.reshape(bsz, seq, d)
```

```python
import functools
import math

import jax
import jax.numpy as jnp
from jax import lax
from jax.experimental import pallas as pl
from jax.experimental.pallas import tpu as pltpu

F32 = jnp.float32
BF16 = jnp.bfloat16

RMS_EPS = 1e-6

A_HEADS = 12
A_HEAD_DIM = 128
A_WIDTH = A_HEADS * A_HEAD_DIM
MOBA_BLOCK = 256
MOBA_TOPK = 3
MOBA_HEADS_PER_STEP = 2
MOBA_FAR_GROUP = 4
MOBA_Q_SCALE = A_HEAD_DIM ** -0.5 * math.log2(math.e)
REL_BUCKETS = 32
REL_MAX_DIST = 128

B_HEAD_DIM = 64
B_WIDTH = 1536
LORA = 96
LORA_PAD = 128
LNX_EPS = 64e-5
WKV_CHUNK = 64
WKV_GROUP = 256
WKV_HEADS_PER_GROUP = WKV_GROUP // B_HEAD_DIM
WKV_GROUPS_PER_STEP = 6
WKV_CHUNKS_PER_STEP = 2

C_HEADS = 4
C_HEAD_DIM = 256
C_WIDTH = C_HEADS * C_HEAD_DIM

N_BRANCHES = 3

VMEM_LIMIT = 48 * 1024 * 1024

NN = (((1,), (0,)), ((), ()))
NT = (((1,), (1,)), ((), ()))
TN = (((0,), (0,)), ((), ()))

MASKED = -1e30
LOG2E = math.log2(math.e)


def _params(n_axes):
    return pltpu.CompilerParams(dimension_semantics=("arbitrary",) * n_axes, vmem_limit_bytes=VMEM_LIMIT)


def _sigmoid(x):
    return 0.5 * jnp.tanh(0.5 * x) + 0.5


def _silu(z):
    return z * _sigmoid(z)


def _bf(x):
    return x.astype(BF16)


def _mm(a, b, dn):
    return lax.dot_general(a, b, dn, preferred_element_type=F32)


def _each(f, *lists):
    return [f(*args) for args in zip(*lists)]


def _rmsnorm_kernel(x_ref, g_ref, o_ref):
    x = x_ref[...].astype(F32)
    y = x * lax.rsqrt(jnp.mean(x * x, axis=-1, keepdims=True) + RMS_EPS)
    o_ref[...] = (y * g_ref[...]).astype(o_ref.dtype)


def _rmsnorm(x2d, g, out_dtype, tm):
    m, d = x2d.shape
    return pl.pallas_call(
        _rmsnorm_kernel,
        grid=(m // tm,),
        in_specs=[pl.BlockSpec((tm, d), lambda i: (i, 0)), pl.BlockSpec((1, d), lambda i: (0, 0))],
        out_specs=pl.BlockSpec((tm, d), lambda i: (i, 0)),
        out_shape=jax.ShapeDtypeStruct((m, d), out_dtype),
        compiler_params=_params(1),
        name="rmsnorm",
    )(x2d, g.reshape(1, d))


def _matmul_kernel(a_ref, w_ref, o_ref):
    o_ref[...] = jnp.dot(a_ref[...], w_ref[...], preferred_element_type=F32).astype(o_ref.dtype)


def _matmul(a, w, out_dtype, tm, tn, name):
    m, k = a.shape
    n = w.shape[1]
    return pl.pallas_call(
        _matmul_kernel,
        grid=(n // tn, m // tm),
        in_specs=[pl.BlockSpec((tm, k), lambda j, i: (i, 0)), pl.BlockSpec((k, tn), lambda j, i: (0, j))],
        out_specs=pl.BlockSpec((tm, tn), lambda j, i: (i, j)),
        out_shape=jax.ShapeDtypeStruct((m, n), out_dtype),
        compiler_params=_params(2),
        name=name,
    )(a, w)


def _matmul_wt_kernel(a_ref, wt_ref, s_ref, o_ref, wb_ref, *carry_ref, seq_rows):
    i = pl.program_id(1)

    @pl.when(i == 0)
    def _():
        wb_ref[...] = wt_ref[...].astype(BF16)
        for ref in carry_ref:
            ref[...] = jnp.zeros_like(ref)

    x = lax.dot_general(a_ref[...], wb_ref[...], NT, preferred_element_type=F32)
    if seq_rows is None:
        o_ref[...] = (x * s_ref[...]).astype(o_ref.dtype)
    else:
        tm = x.shape[0]
        last, = carry_ref
        first_row = lax.broadcasted_iota(jnp.int32, x.shape, 0) == 0
        carried = jnp.where((i * tm) % seq_rows == 0, 0.0, last[0:1, :])
        prev = jnp.where(first_row, carried, pltpu.roll(x, 1, 0))
        last[0:1, :] = x[tm - 1:tm, :]
        o_ref[...] = (x + (prev - x) * s_ref[...]).astype(o_ref.dtype)


def _matmul_wt(a, wt, row0, n, col_vec, out_dtype, tm, tn, name, seq_rows=None):
    m, k = a.shape
    scratch = [pltpu.VMEM((tn, k), BF16)]
    if seq_rows is not None:
        assert seq_rows % tm == 0
        scratch.append(pltpu.VMEM((8, tn), F32))
    return pl.pallas_call(
        functools.partial(_matmul_wt_kernel, seq_rows=seq_rows),
        grid=(n // tn, m // tm),
        in_specs=[pl.BlockSpec((tm, k), lambda j, i: (i, 0)),
                  pl.BlockSpec((pl.Element(tn), pl.Element(k)), lambda j, i: (pl.multiple_of(row0 + j * tn, 8), 0)),
                  pl.BlockSpec((1, tn), lambda j, i: (0, j))],
        out_specs=pl.BlockSpec((tm, tn), lambda j, i: (i, j)),
        out_shape=jax.ShapeDtypeStruct((m, n), out_dtype),
        scratch_shapes=scratch,
        compiler_params=_params(2),
        name=name,
    )(a, wt, col_vec)


def _t5_bucket(dist):
    n = jnp.maximum(dist, 0)
    max_exact = REL_BUCKETS // 2
    nf = jnp.maximum(n, max_exact).astype(F32)
    large = max_exact + (jnp.log(nf / max_exact) / math.log(REL_MAX_DIST / max_exact) * (REL_BUCKETS - max_exact)).astype(jnp.int32)
    large = jnp.minimum(large, REL_BUCKETS - 1)
    return jnp.where(n < max_exact, n, large)


def _bias_kernel(rel_ref, bucket_ref, o_ref):
    h = pl.program_id(0)
    bucket = bucket_ref[...]
    acc = jnp.zeros(bucket.shape, F32)
    for b in range(REL_BUCKETS):
        acc = jnp.where(bucket == b, rel_ref[b, h], acc)
    qpos = lax.broadcasted_iota(jnp.int32, bucket.shape, 0) + MOBA_BLOCK
    kpos = lax.broadcasted_iota(jnp.int32, bucket.shape, 1)
    o_ref[0] = jnp.where(kpos <= qpos, acc * LOG2E, MASKED)


def _bias_tiles(rel_bias):
    blk = MOBA_BLOCK
    qpos = lax.broadcasted_iota(jnp.int32, (blk, 2 * blk), 0) + blk
    kpos = lax.broadcasted_iota(jnp.int32, (blk, 2 * blk), 1)
    bucket = _t5_bucket(qpos - kpos)
    return pl.pallas_call(
        _bias_kernel,
        grid=(A_HEADS,),
        in_specs=[pl.BlockSpec(memory_space=pltpu.SMEM), pl.BlockSpec((blk, 2 * blk), lambda h: (0, 0))],
        out_specs=pl.BlockSpec((1, blk, 2 * blk), lambda h: (h, 0, 0)),
        out_shape=jax.ShapeDtypeStruct((A_HEADS, blk, 2 * blk), F32),
        compiler_params=_params(1),
        name="t5_bias",
    )(rel_bias, bucket)


def _moba_kernel(rel_ref, q_ref, k_ref, v_ref, z_ref, bias_ref, o_ref, kmean_ref):
    hb = pl.program_id(1)
    qi = pl.program_id(2)
    blk, hd = MOBA_BLOCK, A_HEAD_DIM
    n_heads = q_ref.shape[1] // hd
    nb = k_ref.shape[0] // blk
    nbp = kmean_ref.shape[1]

    @pl.when(qi == 0)
    def _():
        kmean_ref[...] = jnp.zeros_like(kmean_ref)
        for j in range(nb):
            mean_j = jnp.sum(k_ref[j * blk:(j + 1) * blk, :].astype(F32), axis=0, keepdims=True) * (1.0 / blk)
            for hh in range(n_heads):
                kmean_ref[hh, j:j + 1, :] = mean_j[:, hh * hd:(hh + 1) * hd]

    n_far = min(MOBA_FAR_GROUP, nb)
    never = 127
    blk_id = lax.broadcasted_iota(jnp.int32, (nbp, blk), 0)
    blk_f = blk_id.astype(F32)
    eye = jnp.where(lax.broadcasted_iota(jnp.int32, (nbp, 128), 0) == lax.broadcasted_iota(jnp.int32, (nbp, 128), 1), 1.0, 0.0).astype(BF16)
    lane = lax.broadcasted_iota(jnp.int32, (blk, 128), 1)
    j_prev = jnp.maximum(qi - 1, 0)
    own = pl.multiple_of(qi * blk, blk)
    prev = pl.multiple_of(j_prev * blk, blk)

    heads = [slice(hh * hd, (hh + 1) * hd) for hh in range(n_heads)]
    q = [q_ref[:, hl] for hl in heads]

    def gate_of(qh, hh):
        km = kmean_ref[hh]
        km_hi = km.astype(BF16)
        km_lo = (km - km_hi.astype(F32)).astype(BF16)
        return jnp.where(blk_id < qi, _mm(km_hi, qh, NT) + _mm(km_lo, qh, NT), -jnp.inf)

    g = _each(gate_of, q, range(n_heads))
    sel_t = [jnp.zeros((nbp, blk), F32) for _ in heads]
    for _ in range(MOBA_TOPK):
        gmax = _each(lambda x: jnp.max(x, axis=0, keepdims=True), g)
        first = _each(lambda x, mx: jnp.min(jnp.where(x == mx, blk_f, float(nbp)), axis=0, keepdims=True), g, gmax)
        pick = _each(lambda f, mx: (blk_f == f) & (mx > -jnp.inf), first, gmax)
        sel_t = _each(lambda p, s: jnp.where(p, 1.0, s), pick, sel_t)
        g = _each(lambda p, x: jnp.where(p, -jnp.inf, x), pick, g)
    sel = _each(lambda s: _mm(s.astype(BF16), eye, TN), sel_t)
    qm = _each(lambda qh, s: jnp.concatenate([qh, jnp.where(s > 0.0, 0.0, MASKED).astype(BF16)], axis=1), q, sel)

    def masked_logits(keys, key_blk):
        col = jnp.where(lax.broadcasted_iota(jnp.int32, key_blk.shape, 1) == key_blk, 1.0, 0.0).astype(BF16)
        return _each(lambda x, kk: _mm(x, jnp.concatenate([kk, col], axis=1), NT), qm, keys)

    def pv(p, values):
        ones = jnp.ones(values[0].shape, BF16)
        return _each(lambda x, v: jnp.dot(x.astype(BF16), jnp.concatenate([v, ones], axis=1), preferred_element_type=F32), p, values)

    cat0 = lambda ref, hl: jnp.concatenate([ref[pl.ds(prev, blk), hl], ref[pl.ds(own, blk), hl]], axis=0)
    key_row = lax.broadcasted_iota(jnp.int32, (2 * blk, 128), 0)
    raw = masked_logits([cat0(k_ref, hl) for hl in heads], jnp.where(key_row < blk, j_prev, -1))
    t = _each(lambda x, hh: x + bias_ref[hh], raw, range(n_heads))
    m = _each(lambda x: jnp.max(x, axis=-1, keepdims=True), t)
    p = _each(lambda x, mx: jnp.exp2(x - mx), t, m)
    acc = pv(p, [cat0(v_ref, hl) for hl in heads])

    bias_far = [rel_ref[REL_BUCKETS - 1, hb * n_heads + hh] * LOG2E for hh in range(n_heads)]
    far_row_blk = lax.broadcasted_iota(jnp.int32, (n_far * blk, 128), 0) // blk

    def body(gi, carry):
        m, acc = list(carry[:n_heads]), list(carry[n_heads:])
        rows = pl.ds(pl.multiple_of(gi * (n_far * blk), n_far * blk), n_far * blk)
        key_blk = gi * n_far + far_row_blk
        raw = masked_logits([k_ref[rows, hl] for hl in heads], jnp.where(key_blk < j_prev, key_blk, never))
        m_new = _each(lambda mx, x, b: jnp.maximum(mx, jnp.max(x, axis=-1, keepdims=True) + b), m, raw, bias_far)
        p = _each(lambda x, mn, b: jnp.exp2(x + (b - mn)), raw, m_new, bias_far)
        alpha = _each(lambda mx, mn: jnp.exp2(mx - mn), m, m_new)
        acc = _each(lambda a, c, x: a * c + x, alpha, acc, pv(p, [v_ref[rows, hl] for hl in heads]))
        return tuple(m_new + acc)

    acc = lax.fori_loop(0, (j_prev + n_far - 1) // n_far, body, tuple(m + acc))[n_heads:]
    for hh, hl in enumerate(heads):
        y = acc[hh][:, :hd] / acc[hh][:, hd:]
        o_ref[:, hl] = (y * _silu(z_ref[:, hl].astype(F32))).astype(o_ref.dtype)


def _moba(p_a, rel_bias, bias_tiles, bsz, seq):
    blk = MOBA_BLOCK
    nq = seq // blk
    hw = MOBA_HEADS_PER_STEP * A_HEAD_DIM
    ns = A_WIDTH // hw
    nbp = max(16, -(-nq // 8) * 8)
    return pl.pallas_call(
        _moba_kernel,
        grid=(bsz, ns, nq),
        in_specs=[
            pl.BlockSpec(memory_space=pltpu.SMEM),
            pl.BlockSpec((blk, hw), lambda b, h, i: (b * nq + i, h)),
            pl.BlockSpec((seq, hw), lambda b, h, i: (b, ns + h)),
            pl.BlockSpec((seq, hw), lambda b, h, i: (b, 2 * ns + h)),
            pl.BlockSpec((blk, hw), lambda b, h, i: (b * nq + i, 3 * ns + h)),
            pl.BlockSpec((MOBA_HEADS_PER_STEP, blk, 2 * blk), lambda b, h, i: (h, 0, 0)),
        ],
        out_specs=pl.BlockSpec((blk, hw), lambda b, h, i: (b * nq + i, h)),
        out_shape=jax.ShapeDtypeStruct((bsz * seq, A_WIDTH), BF16),
        scratch_shapes=[pltpu.VMEM((MOBA_HEADS_PER_STEP, nbp, A_HEAD_DIM), F32)],
        compiler_params=_params(3),
        name="moba_attention",
    )(rel_bias, p_a, p_a, p_a, p_a, bias_tiles)


def _cross_kernel(q_ref, k_ref, v_ref, z_ref, o_ref):
    s = lax.dot_general(q_ref[...], k_ref[...], NT, preferred_element_type=F32) * (C_HEAD_DIM ** -0.5)
    m = jnp.max(s, axis=-1, keepdims=True)
    p = jnp.exp(s - m)
    l = jnp.sum(p, axis=-1, keepdims=True)
    y = jnp.dot(p.astype(BF16), v_ref[...], preferred_element_type=F32) / l
    o_ref[...] = (y * _silu(z_ref[...].astype(F32))).astype(o_ref.dtype)


def _cross(p_c, kv, bsz, seq, mem_len, tq):
    hd = C_HEAD_DIM
    nt = seq // tq
    return pl.pallas_call(
        _cross_kernel,
        grid=(bsz, C_HEADS, nt),
        in_specs=[
            pl.BlockSpec((tq, hd), lambda b, h, i: (b * nt + i, h)),
            pl.BlockSpec((mem_len, hd), lambda b, h, i: (b, h)),
            pl.BlockSpec((mem_len, hd), lambda b, h, i: (b, C_HEADS + h)),
            pl.BlockSpec((tq, hd), lambda b, h, i: (b * nt + i, C_HEADS + h)),
        ],
        out_specs=pl.BlockSpec((tq, hd), lambda b, h, i: (b * nt + i, h)),
        out_shape=jax.ShapeDtypeStruct((bsz * seq, C_WIDTH), BF16),
        compiler_params=_params(3),
        name="memory_attention",
    )(p_c, kv, kv, p_c)


class _WkvConsts:
    def __init__(self):
        c, g, hd = WKV_CHUNK, WKV_GROUP, B_HEAD_DIM
        row = lax.broadcasted_iota(jnp.int32, (g, g), 0)
        col = lax.broadcasted_iota(jnp.int32, (g, g), 1)
        self.same_head = jnp.where((row // hd) == (col // hd), 1.0, 0.0)
        self.ones_bd = self.same_head.astype(BF16)
        t = lax.broadcasted_iota(jnp.int32, (c, g), 0)
        s = lax.broadcasted_iota(jnp.int32, (c, g), 1) % hd
        self.strict = s < t
        self.incl = s <= t
        self.strict_incl = jnp.concatenate([self.strict, self.incl], axis=0)
        self.eye = jnp.where(s == t, 1.0, 0.0).astype(F32)
        tr = lax.broadcasted_iota(jnp.int32, (c, c), 0)
        tc = lax.broadcasted_iota(jnp.int32, (c, c), 1)
        self.lower = jnp.where(tc <= tr, 1.0, 0.0).astype(BF16)

    def bd(self, p):
        return jnp.concatenate([p] * WKV_HEADS_PER_GROUP, axis=0) * self.ones_bd

    def segsum(self, x):
        return jnp.dot(_bf(x), self.ones_bd, preferred_element_type=F32)


def _wkv_prepare(rm, kmod, vm, avec, bvec, logdec, k):
    c = WKV_CHUNK
    cat = lambda u, w: jnp.concatenate([u, w], axis=0)
    ld_hi = _each(_bf, logdec)
    ld_lo = _each(lambda x, hi: _bf(x - hi.astype(F32)), logdec, ld_hi)
    cumsum = lambda hi, lo: jnp.dot(k.lower, hi, preferred_element_type=F32) + jnp.dot(k.lower, lo, preferred_element_type=F32)
    cl = _each(cumsum, ld_hi, ld_lo)
    cl_last = _each(lambda x: x[c - 1:c, :], cl)
    rt = _each(lambda r, x: _bf(r * jnp.exp(x)), rm, cl)
    at = _each(lambda a, x, ld: _bf(a * jnp.exp(x - ld)), avec, cl, logdec)
    e_neg = _each(lambda x: jnp.exp(-x), cl)
    bt = _each(lambda b, e: k.bd(_bf(b * e)), bvec, e_neg)
    kt = _each(lambda kk, e: k.bd(_bf(kk * e)), kmod, e_neg)
    e_rem = _each(lambda xl, x: jnp.exp(xl - x), cl_last, cl)
    bkh = _each(lambda b, kk, e: cat(_bf(b * e), _bf(kk * e)), bvec, kmod, e_rem)
    vb = _each(_bf, vm)

    lhs = _each(cat, at, rt)
    ab = _each(lambda x, y: _mm(x, y, NT), lhs, bt)
    ak = _each(lambda x, y: _mm(x, y, NT), lhs, kt)
    a_ab = _each(lambda x: jnp.where(k.strict, x[:c], 0.0), ab)
    a_rb = _each(lambda x: _bf(jnp.where(k.incl, x[c:], 0.0)), ab)
    a_k = _each(lambda x: _bf(jnp.where(k.strict_incl, x, 0.0)), ak)

    inv = _each(lambda x: k.eye + x, a_ab)
    pw = _each(_bf, a_ab)
    pw = _each(lambda p: _bf(_mm(p, k.bd(p), NN)), pw)
    for _ in range(int(math.log2(c)) - 2):
        tp = _each(lambda i, p: _mm(cat(_bf(i), p), k.bd(p), NN), inv, pw)
        inv = _each(lambda i, t: i + t[:c], inv, tp)
        pw = _each(lambda t: _bf(t[c:]), tp)
    inv = _each(lambda i, p: _bf(i + _mm(_bf(i), k.bd(p), NN)), inv, pw)

    av = _each(lambda x, v: _mm(x, k.bd(v), NN), a_k, vb)
    w = _each(lambda i, a: _mm(i, k.bd(a), NN), inv, at)
    u0 = _each(lambda i, x: _mm(i, k.bd(_bf(x[:c])), NN), inv, av)
    wr = _each(lambda ww, r: cat(_bf(ww), r), w, rt)
    y0 = _each(lambda x: x[c:], av)
    decay = _each(jnp.exp, cl_last)
    return list(zip(wr, u0, y0, a_rb, vb, bkh, decay))


def _wkv_apply(prep, state, k):
    c = WKV_CHUNK
    wr, u0, y0, a_rb, vb, bkh, decay = (list(x) for x in zip(*prep))
    uy = _each(lambda x, s: _mm(x, _bf(s), NT), wr, state)
    ub = _each(lambda x, y: _bf(x[:c] + y), uy, u0)
    y = _each(lambda x, a, uu, z: x[c:] + _mm(a, k.bd(uu), NN) + z, uy, a_rb, ub, y0)
    upd = _each(lambda uu, v, bk: _mm(jnp.concatenate([uu, v], axis=0), bk, TN), ub, vb, bkh)
    new_state = _each(lambda s, dc, d: s * dc + d * k.same_head, state, decay, upd)
    return y, new_state


def _rwkv_kernel(r_ref, k_ref, v_ref, lwla_ref, z_ref, w0_ref, a0_ref, kk_ref, ka_ref, rk_ref, lnw_ref, lnb_ref,
                 wd_ref, wa_ref, o_ref, state_ref):
    c, g = WKV_CHUNK, WKV_GROUP
    n_groups = r_ref.shape[1] // g
    n_chunks = WKV_CHUNKS_PER_STEP

    @pl.when(pl.program_id(2) == 0)
    def _():
        state_ref[...] = jnp.zeros_like(state_ref)

    k = _WkvConsts()
    wd = _bf(wd_ref[...])
    wa = _bf(wa_ref[...])
    inv_hd = 1.0 / B_HEAD_DIM
    groups = [slice(gi * g, (gi + 1) * g) for gi in range(n_groups)]
    split = lambda x: [x[ch * c:(ch + 1) * c, ln] for ch in range(n_chunks) for ln in groups]

    def segsums(xs):
        s = k.segsum(jnp.concatenate(xs, axis=0))
        return [s[i * c:(i + 1) * c] for i in range(len(xs))]

    def step(ci, carry):
        rows = pl.ds(pl.multiple_of(ci * (n_chunks * c), n_chunks * c), n_chunks * c)
        rm, km, vm = r_ref[rows, :], k_ref[rows, :], v_ref[rows, :]
        lwla = lwla_ref[rows, :]
        lw = _bf(jnp.tanh(lwla[:, :LORA_PAD]))
        la = _bf(lwla[:, LORA_PAD:])
        logdec = -math.exp(-0.5) * _sigmoid(w0_ref[...] + jnp.dot(lw, wd, preferred_element_type=F32))
        a_lr = _sigmoid(a0_ref[...] + jnp.dot(la, wa, preferred_element_type=F32))
        kmod = km * (1.0 + (a_lr - 1.0) * ka_ref[...])
        kk = split(km * kk_ref[...])
        n = len(kk)
        sums = segsums(_each(lambda x: x * x, kk) + split(rm * kmod * rk_ref[...]))
        kk = _each(lambda x, ss: x * jnp.minimum(lax.rsqrt(ss), 1e12), kk, sums[:n])
        bonus = _each(lambda s, v: s * v, sums[n:], split(vm))
        prep = _wkv_prepare(split(rm), split(kmod), split(vm), _each(lambda x: -x, kk),
                            _each(lambda x, a: x * a, kk, split(a_lr)), split(logdec), k)
        state = [state_ref[gi] for gi in range(n_groups)]
        y = []
        for ch in range(n_chunks):
            y_ch, state = _wkv_apply(prep[ch * n_groups:(ch + 1) * n_groups], state, k)
            y += y_ch
        for gi in range(n_groups):
            state_ref[gi] = state[gi]
        yc = _each(lambda x, s: x - s * inv_hd, y, segsums(y))
        var = _each(lambda s: s * inv_hd, segsums(_each(lambda x: x * x, yc)))
        gate = _silu(z_ref[rows, :].astype(F32))
        for ch in range(n_chunks):
            out_rows = pl.ds(pl.multiple_of(ci * (n_chunks * c) + ch * c, c), c)
            for gi, ln in enumerate(groups):
                i = ch * n_groups + gi
                yn = yc[i] * lax.rsqrt(var[i] + LNX_EPS) * lnw_ref[:, ln] + lnb_ref[:, ln]
                o_ref[out_rows, ln] = ((yn + bonus[i]) * gate[ch * c:(ch + 1) * c, ln]).astype(o_ref.dtype)
        return carry

    lax.fori_loop(0, r_ref.shape[0] // (n_chunks * c), step, 0)


def _rwkv(p_b, p_lora, p_zb, vecs, wd2, wa2, bsz, seq, tt):
    g = WKV_GROUP
    gw = WKV_GROUPS_PER_STEP * g
    ns = B_WIDTH // gw
    nt = seq // tt
    row = lambda b, gi, t: b * nt + t
    vec_spec = pl.BlockSpec((1, gw), lambda b, gi, t: (0, gi))
    lora_w_spec = pl.BlockSpec((LORA_PAD, gw), lambda b, gi, t: (0, gi))
    return pl.pallas_call(
        _rwkv_kernel,
        grid=(bsz, ns, nt),
        in_specs=[
            pl.BlockSpec((tt, gw), lambda b, gi, t: (row(b, gi, t), gi)),
            pl.BlockSpec((tt, gw), lambda b, gi, t: (row(b, gi, t), ns + gi)),
            pl.BlockSpec((tt, gw), lambda b, gi, t: (row(b, gi, t), 2 * ns + gi)),
            pl.BlockSpec((tt, 2 * LORA_PAD), lambda b, gi, t: (row(b, gi, t), 0)),
            pl.BlockSpec((tt, gw), lambda b, gi, t: (row(b, gi, t), gi)),
        ] + [vec_spec] * len(vecs) + [lora_w_spec, lora_w_spec],
        out_specs=pl.BlockSpec((tt, gw), lambda b, gi, t: (row(b, gi, t), gi)),
        out_shape=jax.ShapeDtypeStruct((bsz * seq, B_WIDTH), BF16),
        scratch_shapes=[pltpu.VMEM((WKV_GROUPS_PER_STEP, g, g), F32)],
        compiler_params=_params(3),
        name="rwkv7_time_mix",
    )(p_b, p_b, p_b, p_lora, p_zb, *vecs, wd2, wa2)


def _merge_kernel(ya_ref, yb_ref, yc_ref, wa_ref, wb_ref, wc_ref, ga_ref, gb_ref, gc_ref, o_ref):
    def branch(y_ref, w_ref, g_ref):
        return _sigmoid(g_ref[...].astype(F32)) * jnp.dot(y_ref[...], w_ref[...], preferred_element_type=F32)

    o_ref[...] = (branch(ya_ref, wa_ref, ga_ref) + branch(yb_ref, wb_ref, gb_ref) + branch(yc_ref, wc_ref, gc_ref)).astype(o_ref.dtype)


def _merge(ya, yb, yc, wa, wb, wc, p_c, gate_col0, d_model, tm, tn):
    m = ya.shape[0]
    nj = d_model // tn
    j0 = gate_col0 // tn
    y_spec = lambda width: pl.BlockSpec((tm, width), lambda j, i: (i, 0))
    w_spec = lambda width: pl.BlockSpec((width, tn), lambda j, i: (0, j))
    g_spec = lambda br: pl.BlockSpec((tm, tn), lambda j, i: (i, j0 + br * nj + j))
    return pl.pallas_call(
        _merge_kernel,
        grid=(nj, m // tm),
        in_specs=[y_spec(A_WIDTH), y_spec(B_WIDTH), y_spec(C_WIDTH), w_spec(A_WIDTH), w_spec(B_WIDTH), w_spec(C_WIDTH),
                  g_spec(0), g_spec(1), g_spec(2)],
        out_specs=pl.BlockSpec((tm, tn), lambda j, i: (i, j)),
        out_shape=jax.ShapeDtypeStruct((m, d_model), BF16),
        compiler_params=_params(2),
        name="gated_merge",
    )(ya, yb, yc, wa, wb, wc, p_c, p_c, p_c)


def _out_kernel(m_ref, w_ref, x_ref, g_ref, o_ref, *, final_norm):
    y = x_ref[...] + jnp.dot(m_ref[...], w_ref[...], preferred_element_type=F32)
    if final_norm:
        y = y * lax.rsqrt(jnp.mean(y * y, axis=-1, keepdims=True) + RMS_EPS) * g_ref[...]
    o_ref[...] = y


def _out_proj(merged, w_out, x2d, g, final_norm, tm):
    m, d = x2d.shape
    return pl.pallas_call(
        functools.partial(_out_kernel, final_norm=final_norm),
        grid=(m // tm,),
        in_specs=[pl.BlockSpec((tm, d), lambda i: (i, 0)), pl.BlockSpec((d, d), lambda i: (0, 0)),
                  pl.BlockSpec((tm, d), lambda i: (i, 0)), pl.BlockSpec((1, d), lambda i: (0, 0))],
        out_specs=pl.BlockSpec((tm, d), lambda i: (i, 0)),
        out_shape=jax.ShapeDtypeStruct((m, d), F32),
        compiler_params=_params(1),
        name="out_proj",
    )(merged, w_out, x2d, g.reshape(1, d))


def _pick_tile(n, candidates):
    for t in candidates:
        if n % t == 0:
            return t
    raise ValueError(f"no tile for extent {n}")


def _pad_cols(w, width):
    return jnp.pad(w, ((0, 0), (0, width - w.shape[1])))


def _layer(x2d, mem2d, bsz, seq, mem_len, bias_tiles, rel_bias, norm_g, mem_norm_g, w_in, rw, w_mem_kv, w_proj_a, w_proj_b,
           w_proj_c, w_out, final_g):
    m, d = x2d.shape
    tm = _pick_tile(m, (1024, 512, 256))

    c_z_a = 3 * A_WIDTH
    c_rkv_b = c_z_a + A_WIDTH
    c_z_b = c_rkv_b + 3 * B_WIDTH
    c_lw = c_z_b + B_WIDTH
    c_la = c_lw + LORA
    c_q_c = c_la + LORA

    h = _rmsnorm(x2d, norm_g, BF16, tm)
    ones = lambda n: jnp.ones((1, n), F32)
    wt = w_in.T
    scale_a = jnp.concatenate([jnp.full((1, A_WIDTH), MOBA_Q_SCALE, F32), ones(3 * A_WIDTH)], axis=1)
    p_a = _matmul_wt(h, wt, 0, 4 * A_WIDTH, scale_a, BF16, tm, 1024, "in_proj_a")
    (mu_r, mu_k, mu_v, mu_w, mu_a, w0, w_decay2, a0, w_aaa2, k_k, k_a, r_k, lnx_w, lnx_b) = rw
    vec = lambda v: v.reshape(1, -1)
    p_b = _matmul_wt(h, wt, c_rkv_b, 3 * B_WIDTH, jnp.concatenate([vec(mu_r), vec(mu_k), vec(mu_v)], axis=1), F32, tm, 768,
                     "in_proj_b", seq_rows=seq)
    p_zb = _matmul_wt(h, wt, c_z_b, B_WIDTH, ones(B_WIDTH), BF16, tm, 768, "in_proj_zb")
    pad_rows = lambda w: jnp.pad(w, ((0, LORA_PAD - LORA), (0, 0)))
    wt_lora = jnp.concatenate([pad_rows(wt[c_lw:c_la]), pad_rows(wt[c_la:c_q_c])], axis=0)
    mu_wa = jnp.concatenate([_pad_cols(vec(mu_w), LORA_PAD), _pad_cols(vec(mu_a), LORA_PAD)], axis=1)
    p_lora = _matmul_wt(h, wt_lora, 0, 2 * LORA_PAD, mu_wa, F32, tm, 2 * LORA_PAD, "in_proj_lora", seq_rows=seq)
    n_c = 2 * C_WIDTH + N_BRANCHES * d
    p_c = _matmul_wt(h, wt, c_q_c, n_c, ones(n_c), BF16, tm, 1024, "in_proj_c")

    mem_n = _rmsnorm(mem2d, mem_norm_g, BF16, _pick_tile(mem2d.shape[0], (1024, 512, 256)))
    kv = _matmul(mem_n, w_mem_kv.astype(BF16), BF16, _pick_tile(mem2d.shape[0], (1024, 512, 256)), 1024, "mem_kv")

    ya = _moba(p_a, rel_bias, bias_tiles, bsz, seq)
    yc = _cross(p_c, kv, bsz, seq, mem_len, _pick_tile(seq, (1024, 512, 256)))

    vecs = (vec(w0), vec(a0), vec(k_k), vec(k_a), vec(r_k), vec(lnx_w), vec(lnx_b))
    yb = _rwkv(p_b, p_lora, p_zb, vecs, pad_rows(w_decay2), pad_rows(w_aaa2), bsz, seq, _pick_tile(seq, (512, 256, 128, 64)))

    merged = _merge(ya, yb, yc, w_proj_a.astype(BF16), w_proj_b.astype(BF16), w_proj_c.astype(BF16), p_c, 2 * C_WIDTH, d,
                    _pick_tile(m, (512, 256)), 1024)
    g = final_g if final_g is not None else jnp.ones((d,), F32)
    return _out_proj(merged, w_out.astype(BF16), x2d, g, final_g is not None, _pick_tile(m, (512, 256)))


def kernel(x, mem, rel_bias, norm_g, mem_norm_g, w_in, rw_mu_r, rw_mu_k, rw_mu_v, rw_mu_w, rw_mu_a, rw_w0, rw_w_decay2, rw_a0, rw_w_aaa2, rw_k_k, rw_k_a, rw_r_k, rw_lnx_w, rw_lnx_b, w_mem_kv, w_proj_a, w_proj_b, w_proj_c, w_out, final_norm_g):
    bsz, seq, d = x.shape
    mem_len = mem.shape[1]
    depth = norm_g.shape[0]
    x2d = x.reshape(bsz * seq, d)
    mem2d = mem.reshape(bsz * mem_len, d)
    bias_tiles = _bias_tiles(rel_bias)
    for l in range(depth):
        rw = (rw_mu_r[l], rw_mu_k[l], rw_mu_v[l], rw_mu_w[l], rw_mu_a[l], rw_w0[l], rw_w_decay2[l], rw_a0[l], rw_w_aaa2[l],
              rw_k_k[l], rw_k_a[l], rw_r_k[l], rw_lnx_w[l], rw_lnx_b[l])
        x2d = _layer(x2d, mem2d, bsz, seq, mem_len, bias_tiles, rel_bias, norm_g[l], mem_norm_g[l], w_in[l], rw, w_mem_kv[l],
                     w_proj_a[l], w_proj_b[l], w_proj_c[l], w_out[l], final_norm_g if l == depth - 1 else None)
    return x2d.reshape(bsz, seq, d)
```

```python
import functools
import math

import jax
import jax.numpy as jnp
from jax import lax
from jax.experimental import pallas as pl
from jax.experimental.pallas import tpu as pltpu

F32 = jnp.float32
BF16 = jnp.bfloat16

RMS_EPS = 1e-6

A_HEADS = 12
A_HEAD_DIM = 128
A_WIDTH = A_HEADS * A_HEAD_DIM
MOBA_BLOCK = 256
MOBA_TOPK = 3
MOBA_HEADS_PER_STEP = 4
MOBA_FAR_GROUP = 4
MOBA_Q_SCALE = A_HEAD_DIM ** -0.5 * math.log2(math.e)
REL_BUCKETS = 32
REL_MAX_DIST = 128

B_HEAD_DIM = 64
B_WIDTH = 1536
LORA = 96
LORA_PAD = 128
LNX_EPS = 64e-5
WKV_CHUNK = 64
WKV_GROUP = 256
WKV_HEADS_PER_GROUP = WKV_GROUP // B_HEAD_DIM
WKV_GROUPS_PER_STEP = 6
WKV_CHUNKS_PER_STEP = 2

C_HEADS = 4
C_HEAD_DIM = 256
C_WIDTH = C_HEADS * C_HEAD_DIM

N_BRANCHES = 3

VMEM_LIMIT = 48 * 1024 * 1024
VMEM_LIMIT_PROJ = 58 * 1024 * 1024

NN = (((1,), (0,)), ((), ()))
NT = (((1,), (1,)), ((), ()))
TN = (((0,), (0,)), ((), ()))

MASKED = -1e30
LOG2E = math.log2(math.e)


def _params(n_axes, vmem_limit=VMEM_LIMIT):
    return pltpu.CompilerParams(dimension_semantics=("arbitrary",) * n_axes, vmem_limit_bytes=vmem_limit)


def _sigmoid(x):
    return 0.5 * jnp.tanh(0.5 * x) + 0.5


def _silu(z):
    return z * _sigmoid(z)


def _bf(x):
    return x.astype(BF16)


def _mm(a, b, dn):
    return lax.dot_general(a, b, dn, preferred_element_type=F32)


def _each(f, *lists):
    return [f(*args) for args in zip(*lists)]


def _rmsnorm_kernel(x_ref, g_ref, o_ref):
    x = x_ref[...].astype(F32)
    y = x * lax.rsqrt(jnp.mean(x * x, axis=-1, keepdims=True) + RMS_EPS)
    o_ref[...] = (y * g_ref[...]).astype(o_ref.dtype)


def _rmsnorm(x2d, g, out_dtype, tm):
    m, d = x2d.shape
    return pl.pallas_call(
        _rmsnorm_kernel,
        grid=(m // tm,),
        in_specs=[pl.BlockSpec((tm, d), lambda i: (i, 0)), pl.BlockSpec((1, d), lambda i: (0, 0))],
        out_specs=pl.BlockSpec((tm, d), lambda i: (i, 0)),
        out_shape=jax.ShapeDtypeStruct((m, d), out_dtype),
        compiler_params=_params(1),
        name="rmsnorm",
    )(x2d, g.reshape(1, d))


def _matmul_kernel(a_ref, w_ref, o_ref):
    o_ref[...] = jnp.dot(a_ref[...], w_ref[...], preferred_element_type=F32).astype(o_ref.dtype)


def _matmul(a, w, out_dtype, tm, tn, name):
    m, k = a.shape
    n = w.shape[1]
    return pl.pallas_call(
        _matmul_kernel,
        grid=(n // tn, m // tm),
        in_specs=[pl.BlockSpec((tm, k), lambda j, i: (i, 0)), pl.BlockSpec((k, tn), lambda j, i: (0, j))],
        out_specs=pl.BlockSpec((tm, tn), lambda j, i: (i, j)),
        out_shape=jax.ShapeDtypeStruct((m, n), out_dtype),
        compiler_params=_params(2),
        name=name,
    )(a, w)


def _matmul_wt_kernel(a_ref, wt_ref, s_ref, o_ref, wb_ref, *carry_ref, seq_rows):
    i = pl.program_id(1)

    @pl.when(i == 0)
    def _():
        wb_ref[...] = wt_ref[...].astype(BF16)
        for ref in carry_ref:
            ref[...] = jnp.zeros_like(ref)

    x = lax.dot_general(a_ref[...], wb_ref[...], NT, preferred_element_type=F32)
    if seq_rows is None:
        o_ref[...] = (x * s_ref[...]).astype(o_ref.dtype)
    else:
        tm = x.shape[0]
        last, = carry_ref
        first_row = lax.broadcasted_iota(jnp.int32, x.shape, 0) == 0
        carried = jnp.where((i * tm) % seq_rows == 0, 0.0, last[0:1, :])
        prev = jnp.where(first_row, carried, pltpu.roll(x, 1, 0))
        last[0:1, :] = x[tm - 1:tm, :]
        o_ref[...] = (x + (prev - x) * s_ref[...]).astype(o_ref.dtype)


def _matmul_wt(a, wt, row0, n, col_vec, out_dtype, tm, tn, name, seq_rows=None):
    m, k = a.shape
    scratch = [pltpu.VMEM((tn, k), BF16)]
    if seq_rows is not None:
        assert seq_rows % tm == 0
        scratch.append(pltpu.VMEM((8, tn), F32))
    return pl.pallas_call(
        functools.partial(_matmul_wt_kernel, seq_rows=seq_rows),
        grid=(n // tn, m // tm),
        in_specs=[pl.BlockSpec((tm, k), lambda j, i: (i, 0)),
                  pl.BlockSpec((pl.Element(tn), pl.Element(k)), lambda j, i: (pl.multiple_of(row0 + j * tn, 8), 0)),
                  pl.BlockSpec((1, tn), lambda j, i: (0, j))],
        out_specs=pl.BlockSpec((tm, tn), lambda j, i: (i, j)),
        out_shape=jax.ShapeDtypeStruct((m, n), out_dtype),
        scratch_shapes=scratch,
        compiler_params=_params(2, VMEM_LIMIT_PROJ),
        name=name,
    )(a, wt, col_vec)


def _t5_bucket(dist):
    n = jnp.maximum(dist, 0)
    max_exact = REL_BUCKETS // 2
    nf = jnp.maximum(n, max_exact).astype(F32)
    large = max_exact + (jnp.log(nf / max_exact) / math.log(REL_MAX_DIST / max_exact) * (REL_BUCKETS - max_exact)).astype(jnp.int32)
    large = jnp.minimum(large, REL_BUCKETS - 1)
    return jnp.where(n < max_exact, n, large)


def _bias_kernel(rel_ref, bucket_ref, o_ref):
    h = pl.program_id(0)
    bucket = bucket_ref[...]
    acc = jnp.zeros(bucket.shape, F32)
    for b in range(REL_BUCKETS):
        acc = jnp.where(bucket == b, rel_ref[b, h], acc)
    qpos = lax.broadcasted_iota(jnp.int32, bucket.shape, 0) + MOBA_BLOCK
    kpos = lax.broadcasted_iota(jnp.int32, bucket.shape, 1)
    o_ref[0] = jnp.where(kpos <= qpos, acc * LOG2E, MASKED)


def _bias_tiles(rel_bias):
    blk = MOBA_BLOCK
    qpos = lax.broadcasted_iota(jnp.int32, (blk, 2 * blk), 0) + blk
    kpos = lax.broadcasted_iota(jnp.int32, (blk, 2 * blk), 1)
    bucket = _t5_bucket(qpos - kpos)
    return pl.pallas_call(
        _bias_kernel,
        grid=(A_HEADS,),
        in_specs=[pl.BlockSpec(memory_space=pltpu.SMEM), pl.BlockSpec((blk, 2 * blk), lambda h: (0, 0))],
        out_specs=pl.BlockSpec((1, blk, 2 * blk), lambda h: (h, 0, 0)),
        out_shape=jax.ShapeDtypeStruct((A_HEADS, blk, 2 * blk), F32),
        compiler_params=_params(1),
        name="t5_bias",
    )(rel_bias, bucket)


def _moba_kernel(rel_ref, q_ref, k_ref, v_ref, z_ref, bias_ref, o_ref, kmean_ref):
    hb = pl.program_id(1)
    qi = pl.program_id(2)
    blk, hd = MOBA_BLOCK, A_HEAD_DIM
    n_heads = q_ref.shape[1] // hd
    nb = k_ref.shape[0] // blk
    nbp = kmean_ref.shape[1]

    @pl.when(qi == 0)
    def _():
        kmean_ref[...] = jnp.zeros_like(kmean_ref)
        for j in range(nb):
            mean_j = jnp.sum(k_ref[j * blk:(j + 1) * blk, :].astype(F32), axis=0, keepdims=True) * (1.0 / blk)
            for hh in range(n_heads):
                kmean_ref[hh, j:j + 1, :] = mean_j[:, hh * hd:(hh + 1) * hd]

    n_far = min(MOBA_FAR_GROUP, nb)
    never = 127
    blk_id = lax.broadcasted_iota(jnp.int32, (nbp, blk), 0)
    blk_f = blk_id.astype(F32)
    eye = jnp.where(lax.broadcasted_iota(jnp.int32, (nbp, 128), 0) == lax.broadcasted_iota(jnp.int32, (nbp, 128), 1), 1.0, 0.0).astype(BF16)
    lane = lax.broadcasted_iota(jnp.int32, (blk, 128), 1)
    j_prev = jnp.maximum(qi - 1, 0)
    own = pl.multiple_of(qi * blk, blk)
    prev = pl.multiple_of(j_prev * blk, blk)

    heads = [slice(hh * hd, (hh + 1) * hd) for hh in range(n_heads)]
    q = [q_ref[:, hl] for hl in heads]

    def gate_of(qh, hh):
        km = kmean_ref[hh]
        km_hi = km.astype(BF16)
        km_lo = (km - km_hi.astype(F32)).astype(BF16)
        return jnp.where(blk_id < qi, _mm(km_hi, qh, NT) + _mm(km_lo, qh, NT), -jnp.inf)

    g = _each(gate_of, q, range(n_heads))
    sel_t = [jnp.zeros((nbp, blk), F32) for _ in heads]
    for _ in range(MOBA_TOPK):
        gmax = _each(lambda x: jnp.max(x, axis=0, keepdims=True), g)
        first = _each(lambda x, mx: jnp.min(jnp.where(x == mx, blk_f, float(nbp)), axis=0, keepdims=True), g, gmax)
        pick = _each(lambda f, mx: (blk_f == f) & (mx > -jnp.inf), first, gmax)
        sel_t = _each(lambda p, s: jnp.where(p, 1.0, s), pick, sel_t)
        g = _each(lambda p, x: jnp.where(p, -jnp.inf, x), pick, g)
    sel = _each(lambda s: _mm(s.astype(BF16), eye, TN), sel_t)
    qm = _each(lambda qh, s: jnp.concatenate([qh, jnp.where(s > 0.0, 0.0, MASKED).astype(BF16)], axis=1), q, sel)

    def masked_logits(keys, key_blk):
        col = jnp.where(lax.broadcasted_iota(jnp.int32, key_blk.shape, 1) == key_blk, 1.0, 0.0).astype(BF16)
        return _each(lambda x, kk: _mm(x, jnp.concatenate([kk, col], axis=1), NT), qm, keys)

    def pv(p, values):
        ones = jnp.ones(values[0].shape, BF16)
        return _each(lambda x, v: jnp.dot(x.astype(BF16), jnp.concatenate([v, ones], axis=1), preferred_element_type=F32), p, values)

    cat0 = lambda ref, hl: jnp.concatenate([ref[pl.ds(prev, blk), hl], ref[pl.ds(own, blk), hl]], axis=0)
    key_row = lax.broadcasted_iota(jnp.int32, (2 * blk, 128), 0)
    raw = masked_logits([cat0(k_ref, hl) for hl in heads], jnp.where(key_row < blk, j_prev, -1))
    t = _each(lambda x, hh: x + bias_ref[hh], raw, range(n_heads))
    m = _each(lambda x: jnp.max(x, axis=-1, keepdims=True), t)
    p = _each(lambda x, mx: jnp.exp2(x - mx), t, m)
    acc = pv(p, [cat0(v_ref, hl) for hl in heads])

    bias_far = [rel_ref[REL_BUCKETS - 1, hb * n_heads + hh] * LOG2E for hh in range(n_heads)]
    far_row_blk = lax.broadcasted_iota(jnp.int32, (n_far * blk, 128), 0) // blk

    def body(gi, carry):
        m, acc = list(carry[:n_heads]), list(carry[n_heads:])
        rows = pl.ds(pl.multiple_of(gi * (n_far * blk), n_far * blk), n_far * blk)
        key_blk = gi * n_far + far_row_blk
        raw = masked_logits([k_ref[rows, hl] for hl in heads], jnp.where(key_blk < j_prev, key_blk, never))
        m_new = _each(lambda mx, x, b: jnp.maximum(mx, jnp.max(x, axis=-1, keepdims=True) + b), m, raw, bias_far)
        p = _each(lambda x, mn, b: jnp.exp2(x + (b - mn)), raw, m_new, bias_far)
        alpha = _each(lambda mx, mn: jnp.exp2(mx - mn), m, m_new)
        acc = _each(lambda a, c, x: a * c + x, alpha, acc, pv(p, [v_ref[rows, hl] for hl in heads]))
        return tuple(m_new + acc)

    acc = lax.fori_loop(0, (j_prev + n_far - 1) // n_far, body, tuple(m + acc))[n_heads:]
    for hh, hl in enumerate(heads):
        y = acc[hh][:, :hd] / acc[hh][:, hd:]
        o_ref[:, hl] = (y * _silu(z_ref[:, hl].astype(F32))).astype(o_ref.dtype)


def _moba(p_a, rel_bias, bias_tiles, bsz, seq):
    blk = MOBA_BLOCK
    nq = seq // blk
    hw = MOBA_HEADS_PER_STEP * A_HEAD_DIM
    ns = A_WIDTH // hw
    nbp = max(16, -(-nq // 8) * 8)
    return pl.pallas_call(
        _moba_kernel,
        grid=(bsz, ns, nq),
        in_specs=[
            pl.BlockSpec(memory_space=pltpu.SMEM),
            pl.BlockSpec((blk, hw), lambda b, h, i: (b * nq + i, h)),
            pl.BlockSpec((seq, hw), lambda b, h, i: (b, ns + h)),
            pl.BlockSpec((seq, hw), lambda b, h, i: (b, 2 * ns + h)),
            pl.BlockSpec((blk, hw), lambda b, h, i: (b * nq + i, 3 * ns + h)),
            pl.BlockSpec((MOBA_HEADS_PER_STEP, blk, 2 * blk), lambda b, h, i: (h, 0, 0)),
        ],
        out_specs=pl.BlockSpec((blk, hw), lambda b, h, i: (b * nq + i, h)),
        out_shape=jax.ShapeDtypeStruct((bsz * seq, A_WIDTH), BF16),
        scratch_shapes=[pltpu.VMEM((MOBA_HEADS_PER_STEP, nbp, A_HEAD_DIM), F32)],
        compiler_params=_params(3),
        name="moba_attention",
    )(rel_bias, p_a, p_a, p_a, p_a, bias_tiles)


def _cross_kernel(q_ref, k_ref, v_ref, z_ref, o_ref):
    hd = C_HEAD_DIM
    heads = [slice(h * hd, (h + 1) * hd) for h in range(C_HEADS)]
    s = _each(lambda hl: _mm(q_ref[:, hl], k_ref[:, hl], NT) * (hd ** -0.5), heads)
    m = _each(lambda x: jnp.max(x, axis=-1, keepdims=True), s)
    p = _each(lambda x, mx: jnp.exp(x - mx), s, m)
    l = _each(lambda x: jnp.sum(x, axis=-1, keepdims=True), p)
    y = _each(lambda x, hl: jnp.dot(x.astype(BF16), v_ref[:, hl], preferred_element_type=F32), p, heads)
    for hl, yh, lh in zip(heads, y, l):
        o_ref[:, hl] = (yh / lh * _silu(z_ref[:, hl].astype(F32))).astype(o_ref.dtype)


def _cross(p_c, kv, bsz, seq, mem_len, tq):
    nt = seq // tq
    return pl.pallas_call(
        _cross_kernel,
        grid=(bsz, nt),
        in_specs=[
            pl.BlockSpec((tq, C_WIDTH), lambda b, i: (b * nt + i, 0)),
            pl.BlockSpec((mem_len, C_WIDTH), lambda b, i: (b, 0)),
            pl.BlockSpec((mem_len, C_WIDTH), lambda b, i: (b, 1)),
            pl.BlockSpec((tq, C_WIDTH), lambda b, i: (b * nt + i, 1)),
        ],
        out_specs=pl.BlockSpec((tq, C_WIDTH), lambda b, i: (b * nt + i, 0)),
        out_shape=jax.ShapeDtypeStruct((bsz * seq, C_WIDTH), BF16),
        compiler_params=_params(2),
        name="memory_attention",
    )(p_c, kv, kv, p_c)


class _WkvConsts:
    def __init__(self):
        c, g, hd = WKV_CHUNK, WKV_GROUP, B_HEAD_DIM
        row = lax.broadcasted_iota(jnp.int32, (g, g), 0)
        col = lax.broadcasted_iota(jnp.int32, (g, g), 1)
        self.same_head = jnp.where((row // hd) == (col // hd), 1.0, 0.0)
        self.ones_bd = self.same_head.astype(BF16)
        t = lax.broadcasted_iota(jnp.int32, (c, g), 0)
        s = lax.broadcasted_iota(jnp.int32, (c, g), 1) % hd
        self.strict = s < t
        self.incl = s <= t
        self.strict_incl = jnp.concatenate([self.strict, self.incl], axis=0)
        self.eye = jnp.where(s == t, 1.0, 0.0).astype(F32)
        tr = lax.broadcasted_iota(jnp.int32, (c, c), 0)
        tc = lax.broadcasted_iota(jnp.int32, (c, c), 1)
        self.lower = jnp.where(tc <= tr, 1.0, 0.0).astype(BF16)

    def bd(self, p):
        return jnp.concatenate([p] * WKV_HEADS_PER_GROUP, axis=0) * self.ones_bd

    def segsum(self, x):
        return jnp.dot(_bf(x), self.ones_bd, preferred_element_type=F32)


def _wkv_prepare(rm, kmod, vm, avec, bvec, logdec, k):
    c = WKV_CHUNK
    cat = lambda u, w: jnp.concatenate([u, w], axis=0)
    ld_hi = _each(_bf, logdec)
    ld_lo = _each(lambda x, hi: _bf(x - hi.astype(F32)), logdec, ld_hi)
    cumsum = lambda hi, lo: jnp.dot(k.lower, hi, preferred_element_type=F32) + jnp.dot(k.lower, lo, preferred_element_type=F32)
    cl = _each(cumsum, ld_hi, ld_lo)
    cl_last = _each(lambda x: x[c - 1:c, :], cl)
    rt = _each(lambda r, x: _bf(r * jnp.exp(x)), rm, cl)
    at = _each(lambda a, x, ld: _bf(a * jnp.exp(x - ld)), avec, cl, logdec)
    e_neg = _each(lambda x: jnp.exp(-x), cl)
    bt = _each(lambda b, e: k.bd(_bf(b * e)), bvec, e_neg)
    kt = _each(lambda kk, e: k.bd(_bf(kk * e)), kmod, e_neg)
    e_rem = _each(lambda xl, x: jnp.exp(xl - x), cl_last, cl)
    bkh = _each(lambda b, kk, e: cat(_bf(b * e), _bf(kk * e)), bvec, kmod, e_rem)
    vb = _each(_bf, vm)

    lhs = _each(cat, at, rt)
    ab = _each(lambda x, y: _mm(x, y, NT), lhs, bt)
    ak = _each(lambda x, y: _mm(x, y, NT), lhs, kt)
    a_ab = _each(lambda x: jnp.where(k.strict, x[:c], 0.0), ab)
    a_rb = _each(lambda x: _bf(jnp.where(k.incl, x[c:], 0.0)), ab)
    a_k = _each(lambda x: _bf(jnp.where(k.strict_incl, x, 0.0)), ak)

    inv = _each(lambda x: k.eye + x, a_ab)
    pw = _each(_bf, a_ab)
    pw = _each(lambda p: _bf(_mm(p, k.bd(p), NN)), pw)
    for _ in range(int(math.log2(c)) - 2):
        tp = _each(lambda i, p: _mm(cat(_bf(i), p), k.bd(p), NN), inv, pw)
        inv = _each(lambda i, t: i + t[:c], inv, tp)
        pw = _each(lambda t: _bf(t[c:]), tp)
    inv = _each(lambda i, p: _bf(i + _mm(_bf(i), k.bd(p), NN)), inv, pw)

    av = _each(lambda x, v: _mm(x, k.bd(v), NN), a_k, vb)
    w = _each(lambda i, a: _mm(i, k.bd(a), NN), inv, at)
    u0 = _each(lambda i, x: _mm(i, k.bd(_bf(x[:c])), NN), inv, av)
    wr = _each(lambda ww, r: cat(_bf(ww), r), w, rt)
    y0 = _each(lambda x: x[c:], av)
    decay = _each(jnp.exp, cl_last)
    return list(zip(wr, u0, y0, a_rb, vb, bkh, decay))


def _wkv_apply(prep, state, k):
    c = WKV_CHUNK
    wr, u0, y0, a_rb, vb, bkh, decay = (list(x) for x in zip(*prep))
    uy = _each(lambda x, s: _mm(x, _bf(s), NT), wr, state)
    ub = _each(lambda x, y: _bf(x[:c] + y), uy, u0)
    y = _each(lambda x, a, uu, z: x[c:] + _mm(a, k.bd(uu), NN) + z, uy, a_rb, ub, y0)
    upd = _each(lambda uu, v, bk: _mm(jnp.concatenate([uu, v], axis=0), bk, TN), ub, vb, bkh)
    new_state = _each(lambda s, dc, d: s * dc + d * k.same_head, state, decay, upd)
    return y, new_state


def _rwkv_kernel(r_ref, k_ref, v_ref, lwla_ref, z_ref, w0_ref, a0_ref, kk_ref, ka_ref, rk_ref, lnw_ref, lnb_ref,
                 wd_ref, wa_ref, o_ref, state_ref):
    c, g = WKV_CHUNK, WKV_GROUP
    n_groups = r_ref.shape[1] // g
    n_chunks = WKV_CHUNKS_PER_STEP

    @pl.when(pl.program_id(2) == 0)
    def _():
        state_ref[...] = jnp.zeros_like(state_ref)

    k = _WkvConsts()
    wd = _bf(wd_ref[...])
    wa = _bf(wa_ref[...])
    inv_hd = 1.0 / B_HEAD_DIM
    groups = [slice(gi * g, (gi + 1) * g) for gi in range(n_groups)]
    split = lambda x: [x[ch * c:(ch + 1) * c, ln] for ch in range(n_chunks) for ln in groups]

    def segsums(xs):
        s = k.segsum(jnp.concatenate(xs, axis=0))
        return [s[i * c:(i + 1) * c] for i in range(len(xs))]

    def step(ci, carry):
        rows = pl.ds(pl.multiple_of(ci * (n_chunks * c), n_chunks * c), n_chunks * c)
        rm, km, vm = r_ref[rows, :], k_ref[rows, :], v_ref[rows, :]
        lwla = lwla_ref[rows, :]
        lw = _bf(jnp.tanh(lwla[:, :LORA_PAD]))
        la = _bf(lwla[:, LORA_PAD:])
        logdec = -math.exp(-0.5) * _sigmoid(w0_ref[...] + jnp.dot(lw, wd, preferred_element_type=F32))
        a_lr = _sigmoid(a0_ref[...] + jnp.dot(la, wa, preferred_element_type=F32))
        kmod = km * (1.0 + (a_lr - 1.0) * ka_ref[...])
        kk = split(km * kk_ref[...])
        n = len(kk)
        sums = segsums(_each(lambda x: x * x, kk) + split(rm * kmod * rk_ref[...]))
        kk = _each(lambda x, ss: x * jnp.minimum(lax.rsqrt(ss), 1e12), kk, sums[:n])
        bonus = _each(lambda s, v: s * v, sums[n:], split(vm))
        prep = _wkv_prepare(split(rm), split(kmod), split(vm), _each(lambda x: -x, kk),
                            _each(lambda x, a: x * a, kk, split(a_lr)), split(logdec), k)
        state = [state_ref[gi] for gi in range(n_groups)]
        y = []
        for ch in range(n_chunks):
            y_ch, state = _wkv_apply(prep[ch * n_groups:(ch + 1) * n_groups], state, k)
            y += y_ch
        for gi in range(n_groups):
            state_ref[gi] = state[gi]
        yc = _each(lambda x, s: x - s * inv_hd, y, segsums(y))
        var = _each(lambda s: s * inv_hd, segsums(_each(lambda x: x * x, yc)))
        gate = _silu(z_ref[rows, :].astype(F32))
        for ch in range(n_chunks):
            out_rows = pl.ds(pl.multiple_of(ci * (n_chunks * c) + ch * c, c), c)
            for gi, ln in enumerate(groups):
                i = ch * n_groups + gi
                yn = yc[i] * lax.rsqrt(var[i] + LNX_EPS) * lnw_ref[:, ln] + lnb_ref[:, ln]
                o_ref[out_rows, ln] = ((yn + bonus[i]) * gate[ch * c:(ch + 1) * c, ln]).astype(o_ref.dtype)
        return carry

    lax.fori_loop(0, r_ref.shape[0] // (n_chunks * c), step, 0)


def _rwkv(p_b, p_lora, p_zb, vecs, wd2, wa2, bsz, seq, tt):
    g = WKV_GROUP
    gw = WKV_GROUPS_PER_STEP * g
    ns = B_WIDTH // gw
    nt = seq // tt
    row = lambda b, gi, t: b * nt + t
    vec_spec = pl.BlockSpec((1, gw), lambda b, gi, t: (0, gi))
    lora_w_spec = pl.BlockSpec((LORA_PAD, gw), lambda b, gi, t: (0, gi))
    return pl.pallas_call(
        _rwkv_kernel,
        grid=(bsz, ns, nt),
        in_specs=[
            pl.BlockSpec((tt, gw), lambda b, gi, t: (row(b, gi, t), gi)),
            pl.BlockSpec((tt, gw), lambda b, gi, t: (row(b, gi, t), ns + gi)),
            pl.BlockSpec((tt, gw), lambda b, gi, t: (row(b, gi, t), 2 * ns + gi)),
            pl.BlockSpec((tt, 2 * LORA_PAD), lambda b, gi, t: (row(b, gi, t), 0)),
            pl.BlockSpec((tt, gw), lambda b, gi, t: (row(b, gi, t), gi)),
        ] + [vec_spec] * len(vecs) + [lora_w_spec, lora_w_spec],
        out_specs=pl.BlockSpec((tt, gw), lambda b, gi, t: (row(b, gi, t), gi)),
        out_shape=jax.ShapeDtypeStruct((bsz * seq, B_WIDTH), BF16),
        scratch_shapes=[pltpu.VMEM((WKV_GROUPS_PER_STEP, g, g), F32)],
        compiler_params=_params(3),
        name="rwkv7_time_mix",
    )(p_b, p_b, p_b, p_lora, p_zb, *vecs, wd2, wa2)


def _merge_kernel(ya_ref, yb_ref, yc_ref, wa_ref, wb_ref, wc_ref, ga_ref, gb_ref, gc_ref, o_ref):
    def branch(y_ref, w_ref, g_ref):
        return _sigmoid(g_ref[...].astype(F32)) * jnp.dot(y_ref[...], w_ref[...], preferred_element_type=F32)

    o_ref[...] = (branch(ya_ref, wa_ref, ga_ref) + branch(yb_ref, wb_ref, gb_ref) + branch(yc_ref, wc_ref, gc_ref)).astype(o_ref.dtype)


def _merge(ya, yb, yc, wa, wb, wc, p_c, gate_col0, d_model, tm, tn):
    m = ya.shape[0]
    nj = d_model // tn
    j0 = gate_col0 // tn
    y_spec = lambda width: pl.BlockSpec((tm, width), lambda j, i: (i, 0))
    w_spec = lambda width: pl.BlockSpec((width, tn), lambda j, i: (0, j))
    g_spec = lambda br: pl.BlockSpec((tm, tn), lambda j, i: (i, j0 + br * nj + j))
    return pl.pallas_call(
        _merge_kernel,
        grid=(nj, m // tm),
        in_specs=[y_spec(A_WIDTH), y_spec(B_WIDTH), y_spec(C_WIDTH), w_spec(A_WIDTH), w_spec(B_WIDTH), w_spec(C_WIDTH),
                  g_spec(0), g_spec(1), g_spec(2)],
        out_specs=pl.BlockSpec((tm, tn), lambda j, i: (i, j)),
        out_shape=jax.ShapeDtypeStruct((m, d_model), BF16),
        compiler_params=_params(2),
        name="gated_merge",
    )(ya, yb, yc, wa, wb, wc, p_c, p_c, p_c)


def _out_kernel(m_ref, w_ref, x_ref, g_ref, o_ref, *, final_norm):
    y = x_ref[...] + jnp.dot(m_ref[...], w_ref[...], preferred_element_type=F32)
    if final_norm:
        y = y * lax.rsqrt(jnp.mean(y * y, axis=-1, keepdims=True) + RMS_EPS) * g_ref[...]
    o_ref[...] = y


def _out_proj(merged, w_out, x2d, g, final_norm, tm):
    m, d = x2d.shape
    return pl.pallas_call(
        functools.partial(_out_kernel, final_norm=final_norm),
        grid=(m // tm,),
        in_specs=[pl.BlockSpec((tm, d), lambda i: (i, 0)), pl.BlockSpec((d, d), lambda i: (0, 0)),
                  pl.BlockSpec((tm, d), lambda i: (i, 0)), pl.BlockSpec((1, d), lambda i: (0, 0))],
        out_specs=pl.BlockSpec((tm, d), lambda i: (i, 0)),
        out_shape=jax.ShapeDtypeStruct((m, d), F32),
        compiler_params=_params(1),
        name="out_proj",
    )(merged, w_out, x2d, g.reshape(1, d))


def _pick_tile(n, candidates):
    for t in candidates:
        if n % t == 0:
            return t
    raise ValueError(f"no tile for extent {n}")


def _pad_cols(w, width):
    return jnp.pad(w, ((0, 0), (0, width - w.shape[1])))


def _layer(x2d, mem2d, bsz, seq, mem_len, bias_tiles, rel_bias, norm_g, mem_norm_g, w_in, rw, w_mem_kv, w_proj_a, w_proj_b,
           w_proj_c, w_out, final_g):
    m, d = x2d.shape
    tm = _pick_tile(m, (1024, 512, 256))
    tm_in = _pick_tile(seq, (2048, 1024, 512, 256))

    c_z_a = 3 * A_WIDTH
    c_rkv_b = c_z_a + A_WIDTH
    c_z_b = c_rkv_b + 3 * B_WIDTH
    c_lw = c_z_b + B_WIDTH
    c_la = c_lw + LORA
    c_q_c = c_la + LORA

    h = _rmsnorm(x2d, norm_g, BF16, tm)
    ones = lambda n: jnp.ones((1, n), F32)
    wt = w_in.T
    scale_a = jnp.concatenate([jnp.full((1, A_WIDTH), MOBA_Q_SCALE, F32), ones(3 * A_WIDTH)], axis=1)
    p_a = _matmul_wt(h, wt, 0, 4 * A_WIDTH, scale_a, BF16, tm_in, 1024, "in_proj_a")
    (mu_r, mu_k, mu_v, mu_w, mu_a, w0, w_decay2, a0, w_aaa2, k_k, k_a, r_k, lnx_w, lnx_b) = rw
    vec = lambda v: v.reshape(1, -1)
    p_b = _matmul_wt(h, wt, c_rkv_b, 3 * B_WIDTH, jnp.concatenate([vec(mu_r), vec(mu_k), vec(mu_v)], axis=1), F32, tm_in, 768,
                     "in_proj_b", seq_rows=seq)
    p_zb = _matmul_wt(h, wt, c_z_b, B_WIDTH, ones(B_WIDTH), BF16, tm_in, 768, "in_proj_zb")
    pad_rows = lambda w: jnp.pad(w, ((0, LORA_PAD - LORA), (0, 0)))
    wt_lora = jnp.concatenate([pad_rows(wt[c_lw:c_la]), pad_rows(wt[c_la:c_q_c])], axis=0)
    mu_wa = jnp.concatenate([_pad_cols(vec(mu_w), LORA_PAD), _pad_cols(vec(mu_a), LORA_PAD)], axis=1)
    p_lora = _matmul_wt(h, wt_lora, 0, 2 * LORA_PAD, mu_wa, F32, tm_in, 2 * LORA_PAD, "in_proj_lora", seq_rows=seq)
    n_c = 2 * C_WIDTH + N_BRANCHES * d
    p_c = _matmul_wt(h, wt, c_q_c, n_c, ones(n_c), BF16, tm_in, 1024, "in_proj_c")

    mem_n = _rmsnorm(mem2d, mem_norm_g, BF16, _pick_tile(mem2d.shape[0], (1024, 512, 256)))
    kv = _matmul(mem_n, w_mem_kv.astype(BF16), BF16, _pick_tile(mem2d.shape[0], (1024, 512, 256)), 1024, "mem_kv")

    ya = _moba(p_a, rel_bias, bias_tiles, bsz, seq)
    yc = _cross(p_c, kv, bsz, seq, mem_len, _pick_tile(seq, (512, 256)))

    vecs = (vec(w0), vec(a0), vec(k_k), vec(k_a), vec(r_k), vec(lnx_w), vec(lnx_b))
    yb = _rwkv(p_b, p_lora, p_zb, vecs, pad_rows(w_decay2), pad_rows(w_aaa2), bsz, seq, _pick_tile(seq, (512, 256, 128, 64)))

    merged = _merge(ya, yb, yc, w_proj_a.astype(BF16), w_proj_b.astype(BF16), w_proj_c.astype(BF16), p_c, 2 * C_WIDTH, d,
                    _pick_tile(m, (512, 256)), 1024)
    g = final_g if final_g is not None else jnp.ones((d,), F32)
    return _out_proj(merged, w_out.astype(BF16), x2d, g, final_g is not None, _pick_tile(m, (512, 256)))


def kernel(x, mem, rel_bias, norm_g, mem_norm_g, w_in, rw_mu_r, rw_mu_k, rw_mu_v, rw_mu_w, rw_mu_a, rw_w0, rw_w_decay2, rw_a0, rw_w_aaa2, rw_k_k, rw_k_a, rw_r_k, rw_lnx_w, rw_lnx_b, w_mem_kv, w_proj_a, w_proj_b, w_proj_c, w_out, final_norm_g):
    bsz, seq, d = x.shape
    mem_len = mem.shape[1]
    depth = norm_g.shape[0]
    x2d = x.reshape(bsz * seq, d)
    mem2d = mem.reshape(bsz * mem_len, d)
    bias_tiles = _bias_tiles(rel_bias)
    for l in range(depth):
        rw = (rw_mu_r[l], rw_mu_k[l], rw_mu_v[l], rw_mu_w[l], rw_mu_a[l], rw_w0[l], rw_w_decay2[l], rw_a0[l], rw_w_aaa2[l],
              rw_k_k[l], rw_k_a[l], rw_r_k[l], rw_lnx_w[l], rw_lnx_b[l])
        x2d = _layer(x2d, mem2d, bsz, seq, mem_len, bias_tiles, rel_bias, norm_g[l], mem_norm_g[l], w_in[l], rw, w_mem_kv[l],
                     w_proj_a[l], w_proj_b[l], w_proj_c[l], w_out[l], final_norm_g if l == depth - 1 else None)
    return x2d.reshape(bsz, seq, d)
```

```python
import functools
import math

import jax
import jax.numpy as jnp
from jax import lax
from jax.experimental import pallas as pl
from jax.experimental.pallas import tpu as pltpu

F32 = jnp.float32
BF16 = jnp.bfloat16

RMS_EPS = 1e-6

A_HEADS = 12
A_HEAD_DIM = 128
A_WIDTH = A_HEADS * A_HEAD_DIM
MOBA_BLOCK = 256
MOBA_TOPK = 3
MOBA_HEADS_PER_STEP = 6
MOBA_FAR_GROUP = 4
MOBA_Q_SCALE = A_HEAD_DIM ** -0.5 * math.log2(math.e)
REL_BUCKETS = 32
REL_MAX_DIST = 128

B_HEAD_DIM = 64
B_WIDTH = 1536
LORA = 96
LORA_PAD = 128
LNX_EPS = 64e-5
WKV_CHUNK = 64
WKV_GROUP = 256
WKV_HEADS_PER_GROUP = WKV_GROUP // B_HEAD_DIM
WKV_GROUPS_PER_STEP = 6
WKV_CHUNKS_PER_STEP = 2

C_HEADS = 4
C_HEAD_DIM = 256
C_WIDTH = C_HEADS * C_HEAD_DIM

N_BRANCHES = 3

VMEM_LIMIT = 48 * 1024 * 1024
VMEM_LIMIT_PROJ = 58 * 1024 * 1024

NN = (((1,), (0,)), ((), ()))
NT = (((1,), (1,)), ((), ()))
TN = (((0,), (0,)), ((), ()))

MASKED = -1e30
LOG2E = math.log2(math.e)


def _params(n_axes, vmem_limit=VMEM_LIMIT):
    return pltpu.CompilerParams(dimension_semantics=("arbitrary",) * n_axes, vmem_limit_bytes=vmem_limit)


def _sigmoid(x):
    return 0.5 * jnp.tanh(0.5 * x) + 0.5


def _silu(z):
    return z * _sigmoid(z)


def _bf(x):
    return x.astype(BF16)


def _mm(a, b, dn):
    return lax.dot_general(a, b, dn, preferred_element_type=F32)


def _each(f, *lists):
    return [f(*args) for args in zip(*lists)]


def _rmsnorm_kernel(x_ref, g_ref, o_ref):
    x = x_ref[...].astype(F32)
    y = x * lax.rsqrt(jnp.mean(x * x, axis=-1, keepdims=True) + RMS_EPS)
    o_ref[...] = (y * g_ref[...]).astype(o_ref.dtype)


def _rmsnorm(x2d, g, out_dtype, tm):
    m, d = x2d.shape
    return pl.pallas_call(
        _rmsnorm_kernel,
        grid=(m // tm,),
        in_specs=[pl.BlockSpec((tm, d), lambda i: (i, 0)), pl.BlockSpec((1, d), lambda i: (0, 0))],
        out_specs=pl.BlockSpec((tm, d), lambda i: (i, 0)),
        out_shape=jax.ShapeDtypeStruct((m, d), out_dtype),
        compiler_params=_params(1),
        name="rmsnorm",
    )(x2d, g.reshape(1, d))


def _matmul_kernel(a_ref, w_ref, o_ref):
    o_ref[...] = jnp.dot(a_ref[...], w_ref[...], preferred_element_type=F32).astype(o_ref.dtype)


def _matmul(a, w, out_dtype, tm, tn, name):
    m, k = a.shape
    n = w.shape[1]
    return pl.pallas_call(
        _matmul_kernel,
        grid=(n // tn, m // tm),
        in_specs=[pl.BlockSpec((tm, k), lambda j, i: (i, 0)), pl.BlockSpec((k, tn), lambda j, i: (0, j))],
        out_specs=pl.BlockSpec((tm, tn), lambda j, i: (i, j)),
        out_shape=jax.ShapeDtypeStruct((m, n), out_dtype),
        compiler_params=_params(2),
        name=name,
    )(a, w)


def _matmul_wt_kernel(a_ref, wt_ref, s_ref, o_ref, wb_ref, *carry_ref, seq_rows):
    i = pl.program_id(1)

    @pl.when(i == 0)
    def _():
        wb_ref[...] = wt_ref[...].astype(BF16)
        for ref in carry_ref:
            ref[...] = jnp.zeros_like(ref)

    x = lax.dot_general(a_ref[...], wb_ref[...], NT, preferred_element_type=F32)
    if seq_rows is None:
        o_ref[...] = (x * s_ref[...]).astype(o_ref.dtype)
    else:
        tm = x.shape[0]
        last, = carry_ref
        first_row = lax.broadcasted_iota(jnp.int32, x.shape, 0) == 0
        carried = jnp.where((i * tm) % seq_rows == 0, 0.0, last[0:1, :])
        prev = jnp.where(first_row, carried, pltpu.roll(x, 1, 0))
        last[0:1, :] = x[tm - 1:tm, :]
        o_ref[...] = (x + (prev - x) * s_ref[...]).astype(o_ref.dtype)


def _matmul_wt(a, wt, row0, n, col_vec, out_dtype, tm, tn, name, seq_rows=None):
    m, k = a.shape
    scratch = [pltpu.VMEM((tn, k), BF16)]
    if seq_rows is not None:
        assert seq_rows % tm == 0
        scratch.append(pltpu.VMEM((8, tn), F32))
    return pl.pallas_call(
        functools.partial(_matmul_wt_kernel, seq_rows=seq_rows),
        grid=(n // tn, m // tm),
        in_specs=[pl.BlockSpec((tm, k), lambda j, i: (i, 0)),
                  pl.BlockSpec((pl.Element(tn), pl.Element(k)), lambda j, i: (pl.multiple_of(row0 + j * tn, 8), 0)),
                  pl.BlockSpec((1, tn), lambda j, i: (0, j))],
        out_specs=pl.BlockSpec((tm, tn), lambda j, i: (i, j)),
        out_shape=jax.ShapeDtypeStruct((m, n), out_dtype),
        scratch_shapes=scratch,
        compiler_params=_params(2, VMEM_LIMIT_PROJ),
        name=name,
    )(a, wt, col_vec)


def _t5_bucket(dist):
    n = jnp.maximum(dist, 0)
    max_exact = REL_BUCKETS // 2
    nf = jnp.maximum(n, max_exact).astype(F32)
    large = max_exact + (jnp.log(nf / max_exact) / math.log(REL_MAX_DIST / max_exact) * (REL_BUCKETS - max_exact)).astype(jnp.int32)
    large = jnp.minimum(large, REL_BUCKETS - 1)
    return jnp.where(n < max_exact, n, large)


def _bias_kernel(rel_ref, bucket_ref, o_ref):
    h = pl.program_id(0)
    bucket = bucket_ref[...]
    acc = jnp.zeros(bucket.shape, F32)
    for b in range(REL_BUCKETS):
        acc = jnp.where(bucket == b, rel_ref[b, h], acc)
    qpos = lax.broadcasted_iota(jnp.int32, bucket.shape, 0) + MOBA_BLOCK
    kpos = lax.broadcasted_iota(jnp.int32, bucket.shape, 1)
    o_ref[0] = jnp.where(kpos <= qpos, acc * LOG2E, MASKED)


def _bias_tiles(rel_bias):
    blk = MOBA_BLOCK
    qpos = lax.broadcasted_iota(jnp.int32, (blk, 2 * blk), 0) + blk
    kpos = lax.broadcasted_iota(jnp.int32, (blk, 2 * blk), 1)
    bucket = _t5_bucket(qpos - kpos)
    return pl.pallas_call(
        _bias_kernel,
        grid=(A_HEADS,),
        in_specs=[pl.BlockSpec(memory_space=pltpu.SMEM), pl.BlockSpec((blk, 2 * blk), lambda h: (0, 0))],
        out_specs=pl.BlockSpec((1, blk, 2 * blk), lambda h: (h, 0, 0)),
        out_shape=jax.ShapeDtypeStruct((A_HEADS, blk, 2 * blk), F32),
        compiler_params=_params(1),
        name="t5_bias",
    )(rel_bias, bucket)


def _moba_kernel(rel_ref, q_ref, k_ref, v_ref, z_ref, bias_ref, o_ref, kmean_ref, m_ref, acc_ref):
    hb = pl.program_id(1)
    qi = pl.program_id(2)
    blk, hd = MOBA_BLOCK, A_HEAD_DIM
    n_heads = q_ref.shape[1] // hd
    nb = k_ref.shape[0] // blk
    nbp = kmean_ref.shape[1]

    @pl.when(qi == 0)
    def _():
        kmean_ref[...] = jnp.zeros_like(kmean_ref)
        for j in range(nb):
            mean_j = jnp.sum(k_ref[j * blk:(j + 1) * blk, :].astype(F32), axis=0, keepdims=True) * (1.0 / blk)
            for hh in range(n_heads):
                kmean_ref[hh, j:j + 1, :] = mean_j[:, hh * hd:(hh + 1) * hd]

    n_far = min(MOBA_FAR_GROUP, nb)
    never = 127
    blk_id = lax.broadcasted_iota(jnp.int32, (nbp, blk), 0)
    blk_f = blk_id.astype(F32)
    eye = jnp.where(lax.broadcasted_iota(jnp.int32, (nbp, 128), 0) == lax.broadcasted_iota(jnp.int32, (nbp, 128), 1), 1.0, 0.0).astype(BF16)
    lane = lax.broadcasted_iota(jnp.int32, (blk, 128), 1)
    j_prev = jnp.maximum(qi - 1, 0)
    own = pl.multiple_of(qi * blk, blk)
    prev = pl.multiple_of(j_prev * blk, blk)

    heads = [slice(hh * hd, (hh + 1) * hd) for hh in range(n_heads)]
    q = [q_ref[:, hl] for hl in heads]

    def gate_of(qh, hh):
        km = kmean_ref[hh]
        km_hi = km.astype(BF16)
        km_lo = (km - km_hi.astype(F32)).astype(BF16)
        return jnp.where(blk_id < qi, _mm(km_hi, qh, NT) + _mm(km_lo, qh, NT), -jnp.inf)

    g = _each(gate_of, q, range(n_heads))
    sel_t = [jnp.zeros((nbp, blk), F32) for _ in heads]
    for _ in range(MOBA_TOPK):
        gmax = _each(lambda x: jnp.max(x, axis=0, keepdims=True), g)
        first = _each(lambda x, mx: jnp.min(jnp.where(x == mx, blk_f, float(nbp)), axis=0, keepdims=True), g, gmax)
        pick = _each(lambda f, mx: (blk_f == f) & (mx > -jnp.inf), first, gmax)
        sel_t = _each(lambda p, s: jnp.where(p, 1.0, s), pick, sel_t)
        g = _each(lambda p, x: jnp.where(p, -jnp.inf, x), pick, g)
    sel = _each(lambda s: _mm(s.astype(BF16), eye, TN), sel_t)
    qm = _each(lambda qh, s: jnp.concatenate([qh, jnp.where(s > 0.0, 0.0, MASKED).astype(BF16)], axis=1), q, sel)

    def masked_logits(keys, key_blk):
        col = jnp.where(lax.broadcasted_iota(jnp.int32, key_blk.shape, 1) == key_blk, 1.0, 0.0).astype(BF16)
        return _each(lambda x, kk: _mm(x, jnp.concatenate([kk, col], axis=1), NT), qm, keys)

    def pv(p, values):
        ones = jnp.ones(values[0].shape, BF16)
        return _each(lambda x, v: jnp.dot(x.astype(BF16), jnp.concatenate([v, ones], axis=1), preferred_element_type=F32), p, values)

    cat0 = lambda ref, hl: jnp.concatenate([ref[pl.ds(prev, blk), hl], ref[pl.ds(own, blk), hl]], axis=0)
    key_row = lax.broadcasted_iota(jnp.int32, (2 * blk, 128), 0)
    raw = masked_logits([cat0(k_ref, hl) for hl in heads], jnp.where(key_row < blk, j_prev, -1))
    t = _each(lambda x, hh: x + bias_ref[hh], raw, range(n_heads))
    m = _each(lambda x: jnp.max(x, axis=-1, keepdims=True), t)
    p = _each(lambda x, mx: jnp.exp2(x - mx), t, m)
    acc = pv(p, [cat0(v_ref, hl) for hl in heads])
    for hh in range(n_heads):
        m_ref[hh] = m[hh]
        acc_ref[hh] = acc[hh]

    bias_far = [rel_ref[REL_BUCKETS - 1, hb * n_heads + hh] * LOG2E for hh in range(n_heads)]
    far_row_blk = lax.broadcasted_iota(jnp.int32, (n_far * blk, 128), 0) // blk

    def body(gi, carry):
        rows = pl.ds(pl.multiple_of(gi * (n_far * blk), n_far * blk), n_far * blk)
        key_blk = gi * n_far + far_row_blk
        raw = masked_logits([k_ref[rows, hl] for hl in heads], jnp.where(key_blk < j_prev, key_blk, never))
        m = [m_ref[hh] for hh in range(n_heads)]
        m_new = _each(lambda mx, x, b: jnp.maximum(mx, jnp.max(x, axis=-1, keepdims=True) + b), m, raw, bias_far)
        p = _each(lambda x, mn, b: jnp.exp2(x + (b - mn)), raw, m_new, bias_far)
        alpha = _each(lambda mx, mn: jnp.exp2(mx - mn), m, m_new)
        upd = pv(p, [v_ref[rows, hl] for hl in heads])
        for hh in range(n_heads):
            m_ref[hh] = m_new[hh]
            acc_ref[hh] = alpha[hh] * acc_ref[hh] + upd[hh]
        return carry

    lax.fori_loop(0, (j_prev + n_far - 1) // n_far, body, 0)
    for hh, hl in enumerate(heads):
        y = acc_ref[hh, :, :hd] / acc_ref[hh, :, hd:]
        o_ref[:, hl] = (y * _silu(z_ref[:, hl].astype(F32))).astype(o_ref.dtype)


def _moba(p_a, rel_bias, bias_tiles, bsz, seq):
    blk = MOBA_BLOCK
    nq = seq // blk
    hw = MOBA_HEADS_PER_STEP * A_HEAD_DIM
    ns = A_WIDTH // hw
    nbp = max(16, -(-nq // 8) * 8)
    return pl.pallas_call(
        _moba_kernel,
        grid=(bsz, ns, nq),
        in_specs=[
            pl.BlockSpec(memory_space=pltpu.SMEM),
            pl.BlockSpec((blk, hw), lambda b, h, i: (b * nq + i, h)),
            pl.BlockSpec((seq, hw), lambda b, h, i: (b, ns + h)),
            pl.BlockSpec((seq, hw), lambda b, h, i: (b, 2 * ns + h)),
            pl.BlockSpec((blk, hw), lambda b, h, i: (b * nq + i, 3 * ns + h)),
            pl.BlockSpec((MOBA_HEADS_PER_STEP, blk, 2 * blk), lambda b, h, i: (h, 0, 0)),
        ],
        out_specs=pl.BlockSpec((blk, hw), lambda b, h, i: (b * nq + i, h)),
        out_shape=jax.ShapeDtypeStruct((bsz * seq, A_WIDTH), BF16),
        scratch_shapes=[pltpu.VMEM((MOBA_HEADS_PER_STEP, nbp, A_HEAD_DIM), F32), pltpu.VMEM((MOBA_HEADS_PER_STEP, blk, 1), F32),
                        pltpu.VMEM((MOBA_HEADS_PER_STEP, blk, 2 * A_HEAD_DIM), F32)],
        compiler_params=_params(3),
        name="moba_attention",
    )(rel_bias, p_a, p_a, p_a, p_a, bias_tiles)


def _cross_kernel(q_ref, k_ref, v_ref, z_ref, o_ref):
    hd = C_HEAD_DIM
    heads = [slice(h * hd, (h + 1) * hd) for h in range(C_HEADS)]
    s = _each(lambda hl: _mm(q_ref[:, hl], k_ref[:, hl], NT) * (hd ** -0.5), heads)
    m = _each(lambda x: jnp.max(x, axis=-1, keepdims=True), s)
    p = _each(lambda x, mx: jnp.exp(x - mx), s, m)
    l = _each(lambda x: jnp.sum(x, axis=-1, keepdims=True), p)
    y = _each(lambda x, hl: jnp.dot(x.astype(BF16), v_ref[:, hl], preferred_element_type=F32), p, heads)
    for hl, yh, lh in zip(heads, y, l):
        o_ref[:, hl] = (yh / lh * _silu(z_ref[:, hl].astype(F32))).astype(o_ref.dtype)


def _cross(p_c, kv, bsz, seq, mem_len, tq):
    nt = seq // tq
    return pl.pallas_call(
        _cross_kernel,
        grid=(bsz, nt),
        in_specs=[
            pl.BlockSpec((tq, C_WIDTH), lambda b, i: (b * nt + i, 0)),
            pl.BlockSpec((mem_len, C_WIDTH), lambda b, i: (b, 0)),
            pl.BlockSpec((mem_len, C_WIDTH), lambda b, i: (b, 1)),
            pl.BlockSpec((tq, C_WIDTH), lambda b, i: (b * nt + i, 1)),
        ],
        out_specs=pl.BlockSpec((tq, C_WIDTH), lambda b, i: (b * nt + i, 0)),
        out_shape=jax.ShapeDtypeStruct((bsz * seq, C_WIDTH), BF16),
        compiler_params=_params(2),
        name="memory_attention",
    )(p_c, kv, kv, p_c)


class _WkvConsts:
    def __init__(self):
        c, g, hd = WKV_CHUNK, WKV_GROUP, B_HEAD_DIM
        row = lax.broadcasted_iota(jnp.int32, (g, g), 0)
        col = lax.broadcasted_iota(jnp.int32, (g, g), 1)
        self.same_head = jnp.where((row // hd) == (col // hd), 1.0, 0.0)
        self.ones_bd = self.same_head.astype(BF16)
        t = lax.broadcasted_iota(jnp.int32, (c, g), 0)
        s = lax.broadcasted_iota(jnp.int32, (c, g), 1) % hd
        self.strict = s < t
        self.incl = s <= t
        self.strict_incl = jnp.concatenate([self.strict, self.incl], axis=0)
        self.eye = jnp.where(s == t, 1.0, 0.0).astype(F32)
        tr = lax.broadcasted_iota(jnp.int32, (c, c), 0)
        tc = lax.broadcasted_iota(jnp.int32, (c, c), 1)
        self.lower = jnp.where(tc <= tr, 1.0, 0.0).astype(BF16)

    def bd(self, p):
        return jnp.concatenate([p] * WKV_HEADS_PER_GROUP, axis=0) * self.ones_bd

    def segsum(self, x):
        return jnp.dot(_bf(x), self.ones_bd, preferred_element_type=F32)


def _wkv_prepare(rm, kmod, vm, avec, bvec, logdec, k):
    c = WKV_CHUNK
    cat = lambda u, w: jnp.concatenate([u, w], axis=0)
    ld_hi = _each(_bf, logdec)
    ld_lo = _each(lambda x, hi: _bf(x - hi.astype(F32)), logdec, ld_hi)
    cumsum = lambda hi, lo: jnp.dot(k.lower, hi, preferred_element_type=F32) + jnp.dot(k.lower, lo, preferred_element_type=F32)
    cl = _each(cumsum, ld_hi, ld_lo)
    cl_last = _each(lambda x: x[c - 1:c, :], cl)
    rt = _each(lambda r, x: _bf(r * jnp.exp(x)), rm, cl)
    at = _each(lambda a, x, ld: _bf(a * jnp.exp(x - ld)), avec, cl, logdec)
    e_neg = _each(lambda x: jnp.exp(-x), cl)
    bt = _each(lambda b, e: k.bd(_bf(b * e)), bvec, e_neg)
    kt = _each(lambda kk, e: k.bd(_bf(kk * e)), kmod, e_neg)
    e_rem = _each(lambda xl, x: jnp.exp(xl - x), cl_last, cl)
    bkh = _each(lambda b, kk, e: cat(_bf(b * e), _bf(kk * e)), bvec, kmod, e_rem)
    vb = _each(_bf, vm)

    lhs = _each(cat, at, rt)
    ab = _each(lambda x, y: _mm(x, y, NT), lhs, bt)
    ak = _each(lambda x, y: _mm(x, y, NT), lhs, kt)
    a_ab = _each(lambda x: jnp.where(k.strict, x[:c], 0.0), ab)
    a_rb = _each(lambda x: _bf(jnp.where(k.incl, x[c:], 0.0)), ab)
    a_k = _each(lambda x: _bf(jnp.where(k.strict_incl, x, 0.0)), ak)

    inv = _each(lambda x: k.eye + x, a_ab)
    pw = _each(_bf, a_ab)
    pw = _each(lambda p: _bf(_mm(p, k.bd(p), NN)), pw)
    for _ in range(int(math.log2(c)) - 2):
        tp = _each(lambda i, p: _mm(cat(_bf(i), p), k.bd(p), NN), inv, pw)
        inv = _each(lambda i, t: i + t[:c], inv, tp)
        pw = _each(lambda t: _bf(t[c:]), tp)
    inv = _each(lambda i, p: _bf(i + _mm(_bf(i), k.bd(p), NN)), inv, pw)

    av = _each(lambda x, v: _mm(x, k.bd(v), NN), a_k, vb)
    w = _each(lambda i, a: _mm(i, k.bd(a), NN), inv, at)
    u0 = _each(lambda i, x: _mm(i, k.bd(_bf(x[:c])), NN), inv, av)
    wr = _each(lambda ww, r: cat(_bf(ww), r), w, rt)
    y0 = _each(lambda x: x[c:], av)
    decay = _each(jnp.exp, cl_last)
    return list(zip(wr, u0, y0, a_rb, vb, bkh, decay))


def _wkv_apply(prep, state, k):
    c = WKV_CHUNK
    wr, u0, y0, a_rb, vb, bkh, decay = (list(x) for x in zip(*prep))
    uy = _each(lambda x, s: _mm(x, _bf(s), NT), wr, state)
    ub = _each(lambda x, y: _bf(x[:c] + y), uy, u0)
    y = _each(lambda x, a, uu, z: x[c:] + _mm(a, k.bd(uu), NN) + z, uy, a_rb, ub, y0)
    upd = _each(lambda uu, v, bk: _mm(jnp.concatenate([uu, v], axis=0), bk, TN), ub, vb, bkh)
    new_state = _each(lambda s, dc, d: s * dc + d * k.same_head, state, decay, upd)
    return y, new_state


def _rwkv_kernel(r_ref, k_ref, v_ref, lwla_ref, z_ref, w0_ref, a0_ref, kk_ref, ka_ref, rk_ref, lnw_ref, lnb_ref,
                 wd_ref, wa_ref, o_ref, state_ref):
    c, g = WKV_CHUNK, WKV_GROUP
    n_groups = r_ref.shape[1] // g
    n_chunks = WKV_CHUNKS_PER_STEP

    @pl.when(pl.program_id(2) == 0)
    def _():
        state_ref[...] = jnp.zeros_like(state_ref)

    k = _WkvConsts()
    wd = _bf(wd_ref[...])
    wa = _bf(wa_ref[...])
    inv_hd = 1.0 / B_HEAD_DIM
    groups = [slice(gi * g, (gi + 1) * g) for gi in range(n_groups)]
    split = lambda x: [x[ch * c:(ch + 1) * c, ln] for ch in range(n_chunks) for ln in groups]

    def segsums(xs):
        s = k.segsum(jnp.concatenate(xs, axis=0))
        return [s[i * c:(i + 1) * c] for i in range(len(xs))]

    def step(ci, carry):
        rows = pl.ds(pl.multiple_of(ci * (n_chunks * c), n_chunks * c), n_chunks * c)
        rm, km, vm = r_ref[rows, :], k_ref[rows, :], v_ref[rows, :]
        lwla = lwla_ref[rows, :]
        lw = _bf(jnp.tanh(lwla[:, :LORA_PAD]))
        la = _bf(lwla[:, LORA_PAD:])
        logdec = -math.exp(-0.5) * _sigmoid(w0_ref[...] + jnp.dot(lw, wd, preferred_element_type=F32))
        a_lr = _sigmoid(a0_ref[...] + jnp.dot(la, wa, preferred_element_type=F32))
        kmod = km * (1.0 + (a_lr - 1.0) * ka_ref[...])
        kk = split(km * kk_ref[...])
        n = len(kk)
        sums = segsums(_each(lambda x: x * x, kk) + split(rm * kmod * rk_ref[...]))
        kk = _each(lambda x, ss: x * jnp.minimum(lax.rsqrt(ss), 1e12), kk, sums[:n])
        bonus = _each(lambda s, v: s * v, sums[n:], split(vm))
        prep = _wkv_prepare(split(rm), split(kmod), split(vm), _each(lambda x: -x, kk),
                            _each(lambda x, a: x * a, kk, split(a_lr)), split(logdec), k)
        state = [state_ref[gi] for gi in range(n_groups)]
        y = []
        for ch in range(n_chunks):
            y_ch, state = _wkv_apply(prep[ch * n_groups:(ch + 1) * n_groups], state, k)
            y += y_ch
        for gi in range(n_groups):
            state_ref[gi] = state[gi]
        yc = _each(lambda x, s: x - s * inv_hd, y, segsums(y))
        var = _each(lambda s: s * inv_hd, segsums(_each(lambda x: x * x, yc)))
        gate = _silu(z_ref[rows, :].astype(F32))
        for ch in range(n_chunks):
            out_rows = pl.ds(pl.multiple_of(ci * (n_chunks * c) + ch * c, c), c)
            for gi, ln in enumerate(groups):
                i = ch * n_groups + gi
                yn = yc[i] * lax.rsqrt(var[i] + LNX_EPS) * lnw_ref[:, ln] + lnb_ref[:, ln]
                o_ref[out_rows, ln] = ((yn + bonus[i]) * gate[ch * c:(ch + 1) * c, ln]).astype(o_ref.dtype)
        return carry

    lax.fori_loop(0, r_ref.shape[0] // (n_chunks * c), step, 0)


def _rwkv(p_b, p_lora, p_zb, vecs, wd2, wa2, bsz, seq, tt):
    g = WKV_GROUP
    gw = WKV_GROUPS_PER_STEP * g
    ns = B_WIDTH // gw
    nt = seq // tt
    row = lambda b, gi, t: b * nt + t
    vec_spec = pl.BlockSpec((1, gw), lambda b, gi, t: (0, gi))
    lora_w_spec = pl.BlockSpec((LORA_PAD, gw), lambda b, gi, t: (0, gi))
    return pl.pallas_call(
        _rwkv_kernel,
        grid=(bsz, ns, nt),
        in_specs=[
            pl.BlockSpec((tt, gw), lambda b, gi, t: (row(b, gi, t), gi)),
            pl.BlockSpec((tt, gw), lambda b, gi, t: (row(b, gi, t), ns + gi)),
            pl.BlockSpec((tt, gw), lambda b, gi, t: (row(b, gi, t), 2 * ns + gi)),
            pl.BlockSpec((tt, 2 * LORA_PAD), lambda b, gi, t: (row(b, gi, t), 0)),
            pl.BlockSpec((tt, gw), lambda b, gi, t: (row(b, gi, t), gi)),
        ] + [vec_spec] * len(vecs) + [lora_w_spec, lora_w_spec],
        out_specs=pl.BlockSpec((tt, gw), lambda b, gi, t: (row(b, gi, t), gi)),
        out_shape=jax.ShapeDtypeStruct((bsz * seq, B_WIDTH), BF16),
        scratch_shapes=[pltpu.VMEM((WKV_GROUPS_PER_STEP, g, g), F32)],
        compiler_params=_params(3),
        name="rwkv7_time_mix",
    )(p_b, p_b, p_b, p_lora, p_zb, *vecs, wd2, wa2)


def _merge_kernel(ya_ref, yb_ref, yc_ref, wa_ref, wb_ref, wc_ref, ga_ref, gb_ref, gc_ref, o_ref):
    def branch(y_ref, w_ref, g_ref):
        return _sigmoid(g_ref[...].astype(F32)) * jnp.dot(y_ref[...], w_ref[...], preferred_element_type=F32)

    o_ref[...] = (branch(ya_ref, wa_ref, ga_ref) + branch(yb_ref, wb_ref, gb_ref) + branch(yc_ref, wc_ref, gc_ref)).astype(o_ref.dtype)


def _merge(ya, yb, yc, wa, wb, wc, p_c, gate_col0, d_model, tm, tn):
    m = ya.shape[0]
    nj = d_model // tn
    j0 = gate_col0 // tn
    y_spec = lambda width: pl.BlockSpec((tm, width), lambda j, i: (i, 0))
    w_spec = lambda width: pl.BlockSpec((width, tn), lambda j, i: (0, j))
    g_spec = lambda br: pl.BlockSpec((tm, tn), lambda j, i: (i, j0 + br * nj + j))
    return pl.pallas_call(
        _merge_kernel,
        grid=(nj, m // tm),
        in_specs=[y_spec(A_WIDTH), y_spec(B_WIDTH), y_spec(C_WIDTH), w_spec(A_WIDTH), w_spec(B_WIDTH), w_spec(C_WIDTH),
                  g_spec(0), g_spec(1), g_spec(2)],
        out_specs=pl.BlockSpec((tm, tn), lambda j, i: (i, j)),
        out_shape=jax.ShapeDtypeStruct((m, d_model), BF16),
        compiler_params=_params(2),
        name="gated_merge",
    )(ya, yb, yc, wa, wb, wc, p_c, p_c, p_c)


def _out_kernel(m_ref, w_ref, x_ref, g_ref, o_ref, *, final_norm):
    y = x_ref[...] + jnp.dot(m_ref[...], w_ref[...], preferred_element_type=F32)
    if final_norm:
        y = y * lax.rsqrt(jnp.mean(y * y, axis=-1, keepdims=True) + RMS_EPS) * g_ref[...]
    o_ref[...] = y


def _out_proj(merged, w_out, x2d, g, final_norm, tm):
    m, d = x2d.shape
    return pl.pallas_call(
        functools.partial(_out_kernel, final_norm=final_norm),
        grid=(m // tm,),
        in_specs=[pl.BlockSpec((tm, d), lambda i: (i, 0)), pl.BlockSpec((d, d), lambda i: (0, 0)),
                  pl.BlockSpec((tm, d), lambda i: (i, 0)), pl.BlockSpec((1, d), lambda i: (0, 0))],
        out_specs=pl.BlockSpec((tm, d), lambda i: (i, 0)),
        out_shape=jax.ShapeDtypeStruct((m, d), F32),
        compiler_params=_params(1),
        name="out_proj",
    )(merged, w_out, x2d, g.reshape(1, d))


def _pick_tile(n, candidates):
    for t in candidates:
        if n % t == 0:
            return t
    raise ValueError(f"no tile for extent {n}")


def _pad_cols(w, width):
    return jnp.pad(w, ((0, 0), (0, width - w.shape[1])))


def _layer(x2d, mem2d, bsz, seq, mem_len, bias_tiles, rel_bias, norm_g, mem_norm_g, w_in, rw, w_mem_kv, w_proj_a, w_proj_b,
           w_proj_c, w_out, final_g):
    m, d = x2d.shape
    tm = _pick_tile(m, (1024, 512, 256))
    tm_in = _pick_tile(seq, (2048, 1024, 512, 256))

    c_z_a = 3 * A_WIDTH
    c_rkv_b = c_z_a + A_WIDTH
    c_z_b = c_rkv_b + 3 * B_WIDTH
    c_lw = c_z_b + B_WIDTH
    c_la = c_lw + LORA
    c_q_c = c_la + LORA

    h = _rmsnorm(x2d, norm_g, BF16, tm)
    ones = lambda n: jnp.ones((1, n), F32)
    wt = w_in.T
    scale_a = jnp.concatenate([jnp.full((1, A_WIDTH), MOBA_Q_SCALE, F32), ones(3 * A_WIDTH)], axis=1)
    p_a = _matmul_wt(h, wt, 0, 4 * A_WIDTH, scale_a, BF16, tm_in, 1024, "in_proj_a")
    (mu_r, mu_k, mu_v, mu_w, mu_a, w0, w_decay2, a0, w_aaa2, k_k, k_a, r_k, lnx_w, lnx_b) = rw
    vec = lambda v: v.reshape(1, -1)
    p_b = _matmul_wt(h, wt, c_rkv_b, 3 * B_WIDTH, jnp.concatenate([vec(mu_r), vec(mu_k), vec(mu_v)], axis=1), F32, tm_in, 768,
                     "in_proj_b", seq_rows=seq)
    p_zb = _matmul_wt(h, wt, c_z_b, B_WIDTH, ones(B_WIDTH), BF16, tm_in, 768, "in_proj_zb")
    pad_rows = lambda w: jnp.pad(w, ((0, LORA_PAD - LORA), (0, 0)))
    wt_lora = jnp.concatenate([pad_rows(wt[c_lw:c_la]), pad_rows(wt[c_la:c_q_c])], axis=0)
    mu_wa = jnp.concatenate([_pad_cols(vec(mu_w), LORA_PAD), _pad_cols(vec(mu_a), LORA_PAD)], axis=1)
    p_lora = _matmul_wt(h, wt_lora, 0, 2 * LORA_PAD, mu_wa, F32, tm_in, 2 * LORA_PAD, "in_proj_lora", seq_rows=seq)
    n_c = 2 * C_WIDTH + N_BRANCHES * d
    p_c = _matmul_wt(h, wt, c_q_c, n_c, ones(n_c), BF16, tm_in, 1024, "in_proj_c")

    mem_n = _rmsnorm(mem2d, mem_norm_g, BF16, _pick_tile(mem2d.shape[0], (1024, 512, 256)))
    kv = _matmul(mem_n, w_mem_kv.astype(BF16), BF16, _pick_tile(mem2d.shape[0], (1024, 512, 256)), 1024, "mem_kv")

    ya = _moba(p_a, rel_bias, bias_tiles, bsz, seq)
    yc = _cross(p_c, kv, bsz, seq, mem_len, _pick_tile(seq, (512, 256)))

    vecs = (vec(w0), vec(a0), vec(k_k), vec(k_a), vec(r_k), vec(lnx_w), vec(lnx_b))
    yb = _rwkv(p_b, p_lora, p_zb, vecs, pad_rows(w_decay2), pad_rows(w_aaa2), bsz, seq, _pick_tile(seq, (512, 256, 128, 64)))

    merged = _merge(ya, yb, yc, w_proj_a.astype(BF16), w_proj_b.astype(BF16), w_proj_c.astype(BF16), p_c, 2 * C_WIDTH, d,
                    _pick_tile(m, (512, 256)), 1024)
    g = final_g if final_g is not None else jnp.ones((d,), F32)
    return _out_proj(merged, w_out.astype(BF16), x2d, g, final_g is not None, _pick_tile(m, (512, 256)))


def kernel(x, mem, rel_bias, norm_g, mem_norm_g, w_in, rw_mu_r, rw_mu_k, rw_mu_v, rw_mu_w, rw_mu_a, rw_w0, rw_w_decay2, rw_a0, rw_w_aaa2, rw_k_k, rw_k_a, rw_r_k, rw_lnx_w, rw_lnx_b, w_mem_kv, w_proj_a, w_proj_b, w_proj_c, w_out, final_norm_g):
    bsz, seq, d = x.shape
    mem_len = mem.shape[1]
    depth = norm_g.shape[0]
    x2d = x.reshape(bsz * seq, d)
    mem2d = mem.reshape(bsz * mem_len, d)
    bias_tiles = _bias_tiles(rel_bias)
    for l in range(depth):
        rw = (rw_mu_r[l], rw_mu_k[l], rw_mu_v[l], rw_mu_w[l], rw_mu_a[l], rw_w0[l], rw_w_decay2[l], rw_a0[l], rw_w_aaa2[l],
              rw_k_k[l], rw_k_a[l], rw_r_k[l], rw_lnx_w[l], rw_lnx_b[l])
        x2d = _layer(x2d, mem2d, bsz, seq, mem_len, bias_tiles, rel_bias, norm_g[l], mem_norm_g[l], w_in[l], rw, w_mem_kv[l],
                     w_proj_a[l], w_proj_b[l], w_proj_c[l], w_out[l], final_norm_g if l == depth - 1 else None)
    return x2d.reshape(bsz, seq, d)
```

```python
import functools
import math
from typing import NamedTuple

import jax
import jax.numpy as jnp
from jax import lax
from jax.experimental import pallas as pl
from jax.experimental.pallas import tpu as pltpu

F32 = jnp.float32
BF16 = jnp.bfloat16

RMS_EPS = 1e-6

A_HEADS = 12
A_HEAD_DIM = 128
A_WIDTH = A_HEADS * A_HEAD_DIM
MOBA_BLOCK = 256
MOBA_TOPK = 3
MOBA_HEADS_PER_STEP = 6
MOBA_FAR_GROUP = 4
MOBA_Q_SCALE = A_HEAD_DIM ** -0.5 * math.log2(math.e)
REL_BUCKETS = 32
REL_MAX_DIST = 128

B_HEAD_DIM = 64
B_WIDTH = 1536
LORA = 96
LORA_PAD = 128
LNX_EPS = 64e-5
WKV_CHUNK = 64
WKV_GROUP = 256
WKV_HEADS_PER_GROUP = WKV_GROUP // B_HEAD_DIM
WKV_GROUPS_PER_STEP = 6
WKV_CHUNKS_PER_STEP = 2

C_HEADS = 4
C_HEAD_DIM = 256
C_WIDTH = C_HEADS * C_HEAD_DIM

N_BRANCHES = 3

VMEM_LIMIT = 48 * 1024 * 1024
VMEM_LIMIT_PROJ = 58 * 1024 * 1024

NN = (((1,), (0,)), ((), ()))
NT = (((1,), (1,)), ((), ()))
TN = (((0,), (0,)), ((), ()))

MASKED = -1e30
LOG2E = math.log2(math.e)


def _params(n_axes, vmem_limit=VMEM_LIMIT):
    return pltpu.CompilerParams(dimension_semantics=("arbitrary",) * n_axes, vmem_limit_bytes=vmem_limit)


def _sigmoid(x):
    return 0.5 * jnp.tanh(0.5 * x) + 0.5


def _silu(z):
    return z * _sigmoid(z)


def _bf(x):
    return x.astype(BF16)


def _mm(a, b, dn):
    return lax.dot_general(a, b, dn, preferred_element_type=F32)


def _each(f, *lists):
    return [f(*args) for args in zip(*lists)]


def _rmsnorm_kernel(x_ref, g_ref, o_ref):
    x = x_ref[...].astype(F32)
    y = x * lax.rsqrt(jnp.mean(x * x, axis=-1, keepdims=True) + RMS_EPS)
    o_ref[...] = (y * g_ref[...]).astype(o_ref.dtype)


def _rmsnorm(x2d, g, out_dtype, tm):
    m, d = x2d.shape
    return pl.pallas_call(
        _rmsnorm_kernel,
        grid=(m // tm,),
        in_specs=[pl.BlockSpec((tm, d), lambda i: (i, 0)), pl.BlockSpec((1, d), lambda i: (0, 0))],
        out_specs=pl.BlockSpec((tm, d), lambda i: (i, 0)),
        out_shape=jax.ShapeDtypeStruct((m, d), out_dtype),
        compiler_params=_params(1),
        name="rmsnorm",
    )(x2d, g.reshape(1, d))


def _matmul_kernel(a_ref, w_ref, o_ref):
    o_ref[...] = jnp.dot(a_ref[...], w_ref[...], preferred_element_type=F32).astype(o_ref.dtype)


def _matmul(a, w, out_dtype, tm, tn, name):
    m, k = a.shape
    n = w.shape[1]
    return pl.pallas_call(
        _matmul_kernel,
        grid=(n // tn, m // tm),
        in_specs=[pl.BlockSpec((tm, k), lambda j, i: (i, 0)), pl.BlockSpec((k, tn), lambda j, i: (0, j))],
        out_specs=pl.BlockSpec((tm, tn), lambda j, i: (i, j)),
        out_shape=jax.ShapeDtypeStruct((m, n), out_dtype),
        compiler_params=_params(2),
        name=name,
    )(a, w)


def _matmul_wt_kernel(a_ref, wt_ref, s_ref, o_ref, wb_ref, *carry_ref, seq_rows):
    i = pl.program_id(1)

    @pl.when(i == 0)
    def _():
        wb_ref[...] = wt_ref[...].astype(BF16)
        for ref in carry_ref:
            ref[...] = jnp.zeros_like(ref)

    x = lax.dot_general(a_ref[...], wb_ref[...], NT, preferred_element_type=F32)
    if seq_rows is None:
        o_ref[...] = (x * s_ref[...]).astype(o_ref.dtype)
    else:
        tm = x.shape[0]
        last, = carry_ref
        first_row = lax.broadcasted_iota(jnp.int32, x.shape, 0) == 0
        carried = jnp.where((i * tm) % seq_rows == 0, 0.0, last[0:1, :])
        prev = jnp.where(first_row, carried, pltpu.roll(x, 1, 0))
        last[0:1, :] = x[tm - 1:tm, :]
        o_ref[...] = (x + (prev - x) * s_ref[...]).astype(o_ref.dtype)


def _matmul_wt(a, wt, row0, n, col_vec, out_dtype, tm, tn, name, seq_rows=None):
    m, k = a.shape
    scratch = [pltpu.VMEM((tn, k), BF16)]
    if seq_rows is not None:
        assert seq_rows % tm == 0
        scratch.append(pltpu.VMEM((8, tn), F32))
    return pl.pallas_call(
        functools.partial(_matmul_wt_kernel, seq_rows=seq_rows),
        grid=(n // tn, m // tm),
        in_specs=[pl.BlockSpec((tm, k), lambda j, i: (i, 0)),
                  pl.BlockSpec((pl.Element(tn), pl.Element(k)), lambda j, i: (pl.multiple_of(row0 + j * tn, 8), 0)),
                  pl.BlockSpec((1, tn), lambda j, i: (0, j))],
        out_specs=pl.BlockSpec((tm, tn), lambda j, i: (i, j)),
        out_shape=jax.ShapeDtypeStruct((m, n), out_dtype),
        scratch_shapes=scratch,
        compiler_params=_params(2, VMEM_LIMIT_PROJ),
        name=name,
    )(a, wt, col_vec)


def _t5_bucket(dist):
    n = jnp.maximum(dist, 0)
    max_exact = REL_BUCKETS // 2
    nf = jnp.maximum(n, max_exact).astype(F32)
    large = max_exact + (jnp.log(nf / max_exact) / math.log(REL_MAX_DIST / max_exact) * (REL_BUCKETS - max_exact)).astype(jnp.int32)
    large = jnp.minimum(large, REL_BUCKETS - 1)
    return jnp.where(n < max_exact, n, large)


def _bias_kernel(rel_ref, bucket_ref, o_ref):
    h = pl.program_id(0)
    bucket = bucket_ref[...]
    acc = jnp.zeros(bucket.shape, F32)
    for b in range(REL_BUCKETS):
        acc = jnp.where(bucket == b, rel_ref[b, h], acc)
    qpos = lax.broadcasted_iota(jnp.int32, bucket.shape, 0) + MOBA_BLOCK
    kpos = lax.broadcasted_iota(jnp.int32, bucket.shape, 1)
    o_ref[0] = jnp.where(kpos <= qpos, acc * LOG2E, MASKED)


def _bias_tiles(rel_bias):
    blk = MOBA_BLOCK
    qpos = lax.broadcasted_iota(jnp.int32, (blk, 2 * blk), 0) + blk
    kpos = lax.broadcasted_iota(jnp.int32, (blk, 2 * blk), 1)
    bucket = _t5_bucket(qpos - kpos)
    return pl.pallas_call(
        _bias_kernel,
        grid=(A_HEADS,),
        in_specs=[pl.BlockSpec(memory_space=pltpu.SMEM), pl.BlockSpec((blk, 2 * blk), lambda h: (0, 0))],
        out_specs=pl.BlockSpec((1, blk, 2 * blk), lambda h: (h, 0, 0)),
        out_shape=jax.ShapeDtypeStruct((A_HEADS, blk, 2 * blk), F32),
        compiler_params=_params(1),
        name="t5_bias",
    )(rel_bias, bucket)


def _moba_kernel(rel_ref, q_ref, k_ref, v_ref, z_ref, bias_ref, o_ref, kmean_ref, m_ref, acc_ref):
    hb = pl.program_id(1)
    qi = pl.program_id(2)
    blk, hd = MOBA_BLOCK, A_HEAD_DIM
    n_heads = q_ref.shape[1] // hd
    nb = k_ref.shape[0] // blk
    nbp = kmean_ref.shape[1]

    @pl.when(qi == 0)
    def _():
        kmean_ref[...] = jnp.zeros_like(kmean_ref)
        for j in range(nb):
            mean_j = jnp.sum(k_ref[j * blk:(j + 1) * blk, :].astype(F32), axis=0, keepdims=True) * (1.0 / blk)
            for hh in range(n_heads):
                kmean_ref[hh, j:j + 1, :] = mean_j[:, hh * hd:(hh + 1) * hd]

    n_far = min(MOBA_FAR_GROUP, nb)
    never = 127
    blk_id = lax.broadcasted_iota(jnp.int32, (nbp, blk), 0)
    blk_f = blk_id.astype(F32)
    eye = jnp.where(lax.broadcasted_iota(jnp.int32, (nbp, 128), 0) == lax.broadcasted_iota(jnp.int32, (nbp, 128), 1), 1.0, 0.0).astype(BF16)
    lane = lax.broadcasted_iota(jnp.int32, (blk, 128), 1)
    j_prev = jnp.maximum(qi - 1, 0)
    own = pl.multiple_of(qi * blk, blk)
    prev = pl.multiple_of(j_prev * blk, blk)

    heads = [slice(hh * hd, (hh + 1) * hd) for hh in range(n_heads)]
    q = [q_ref[:, hl] for hl in heads]

    def gate_of(qh, hh):
        km = kmean_ref[hh]
        km_hi = km.astype(BF16)
        km_lo = (km - km_hi.astype(F32)).astype(BF16)
        return jnp.where(blk_id < qi, _mm(km_hi, qh, NT) + _mm(km_lo, qh, NT), -jnp.inf)

    g = _each(gate_of, q, range(n_heads))
    sel_t = [jnp.zeros((nbp, blk), F32) for _ in heads]
    for _ in range(MOBA_TOPK):
        gmax = _each(lambda x: jnp.max(x, axis=0, keepdims=True), g)
        first = _each(lambda x, mx: jnp.min(jnp.where(x == mx, blk_f, float(nbp)), axis=0, keepdims=True), g, gmax)
        pick = _each(lambda f, mx: (blk_f == f) & (mx > -jnp.inf), first, gmax)
        sel_t = _each(lambda p, s: jnp.where(p, 1.0, s), pick, sel_t)
        g = _each(lambda p, x: jnp.where(p, -jnp.inf, x), pick, g)
    sel = _each(lambda s: _mm(s.astype(BF16), eye, TN), sel_t)
    qm = _each(lambda qh, s: jnp.concatenate([qh, jnp.where(s > 0.0, 0.0, MASKED).astype(BF16)], axis=1), q, sel)

    def masked_logits(keys, key_blk):
        col = jnp.where(lax.broadcasted_iota(jnp.int32, key_blk.shape, 1) == key_blk, 1.0, 0.0).astype(BF16)
        return _each(lambda x, kk: _mm(x, jnp.concatenate([kk, col], axis=1), NT), qm, keys)

    def pv(p, values):
        ones = jnp.ones(values[0].shape, BF16)
        return _each(lambda x, v: jnp.dot(x.astype(BF16), jnp.concatenate([v, ones], axis=1), preferred_element_type=F32), p, values)

    cat0 = lambda ref, hl: jnp.concatenate([ref[pl.ds(prev, blk), hl], ref[pl.ds(own, blk), hl]], axis=0)
    key_row = lax.broadcasted_iota(jnp.int32, (2 * blk, 128), 0)
    raw = masked_logits([cat0(k_ref, hl) for hl in heads], jnp.where(key_row < blk, j_prev, -1))
    t = _each(lambda x, hh: x + bias_ref[hh], raw, range(n_heads))
    m = _each(lambda x: jnp.max(x, axis=-1, keepdims=True), t)
    p = _each(lambda x, mx: jnp.exp2(x - mx), t, m)
    acc = pv(p, [cat0(v_ref, hl) for hl in heads])
    for hh in range(n_heads):
        m_ref[hh] = m[hh]
        acc_ref[hh] = acc[hh]

    bias_far = [rel_ref[REL_BUCKETS - 1, hb * n_heads + hh] * LOG2E for hh in range(n_heads)]
    far_row_blk = lax.broadcasted_iota(jnp.int32, (n_far * blk, 128), 0) // blk

    def body(gi, carry):
        rows = pl.ds(pl.multiple_of(gi * (n_far * blk), n_far * blk), n_far * blk)
        key_blk = gi * n_far + far_row_blk
        raw = masked_logits([k_ref[rows, hl] for hl in heads], jnp.where(key_blk < j_prev, key_blk, never))
        m = [m_ref[hh] for hh in range(n_heads)]
        m_new = _each(lambda mx, x, b: jnp.maximum(mx, jnp.max(x, axis=-1, keepdims=True) + b), m, raw, bias_far)
        p = _each(lambda x, mn, b: jnp.exp2(x + (b - mn)), raw, m_new, bias_far)
        alpha = _each(lambda mx, mn: jnp.exp2(mx - mn), m, m_new)
        upd = pv(p, [v_ref[rows, hl] for hl in heads])
        for hh in range(n_heads):
            m_ref[hh] = m_new[hh]
            acc_ref[hh] = alpha[hh] * acc_ref[hh] + upd[hh]
        return carry

    lax.fori_loop(0, (j_prev + n_far - 1) // n_far, body, 0)
    for hh, hl in enumerate(heads):
        y = acc_ref[hh, :, :hd] / acc_ref[hh, :, hd:]
        o_ref[:, hl] = (y * _silu(z_ref[:, hl].astype(F32))).astype(o_ref.dtype)


def _moba(p_a, rel_bias, bias_tiles, bsz, seq):
    blk = MOBA_BLOCK
    nq = seq // blk
    hw = MOBA_HEADS_PER_STEP * A_HEAD_DIM
    ns = A_WIDTH // hw
    nbp = max(16, -(-nq // 8) * 8)
    return pl.pallas_call(
        _moba_kernel,
        grid=(bsz, ns, nq),
        in_specs=[
            pl.BlockSpec(memory_space=pltpu.SMEM),
            pl.BlockSpec((blk, hw), lambda b, h, i: (b * nq + i, h)),
            pl.BlockSpec((seq, hw), lambda b, h, i: (b, ns + h)),
            pl.BlockSpec((seq, hw), lambda b, h, i: (b, 2 * ns + h)),
            pl.BlockSpec((blk, hw), lambda b, h, i: (b * nq + i, 3 * ns + h)),
            pl.BlockSpec((MOBA_HEADS_PER_STEP, blk, 2 * blk), lambda b, h, i: (h, 0, 0)),
        ],
        out_specs=pl.BlockSpec((blk, hw), lambda b, h, i: (b * nq + i, h)),
        out_shape=jax.ShapeDtypeStruct((bsz * seq, A_WIDTH), BF16),
        scratch_shapes=[pltpu.VMEM((MOBA_HEADS_PER_STEP, nbp, A_HEAD_DIM), F32), pltpu.VMEM((MOBA_HEADS_PER_STEP, blk, 1), F32),
                        pltpu.VMEM((MOBA_HEADS_PER_STEP, blk, 2 * A_HEAD_DIM), F32)],
        compiler_params=_params(3),
        name="moba_attention",
    )(rel_bias, p_a, p_a, p_a, p_a, bias_tiles)


def _cross_kernel(q_ref, k_ref, v_ref, z_ref, o_ref):
    hd = C_HEAD_DIM
    heads = [slice(h * hd, (h + 1) * hd) for h in range(C_HEADS)]
    s = _each(lambda hl: _mm(q_ref[:, hl], k_ref[:, hl], NT) * (hd ** -0.5), heads)
    m = _each(lambda x: jnp.max(x, axis=-1, keepdims=True), s)
    p = _each(lambda x, mx: jnp.exp(x - mx), s, m)
    l = _each(lambda x: jnp.sum(x, axis=-1, keepdims=True), p)
    y = _each(lambda x, hl: jnp.dot(x.astype(BF16), v_ref[:, hl], preferred_element_type=F32), p, heads)
    for hl, yh, lh in zip(heads, y, l):
        o_ref[:, hl] = (yh / lh * _silu(z_ref[:, hl].astype(F32))).astype(o_ref.dtype)


def _cross(p_c, kv, bsz, seq, mem_len, tq):
    nt = seq // tq
    return pl.pallas_call(
        _cross_kernel,
        grid=(bsz, nt),
        in_specs=[
            pl.BlockSpec((tq, C_WIDTH), lambda b, i: (b * nt + i, 0)),
            pl.BlockSpec((mem_len, C_WIDTH), lambda b, i: (b, 0)),
            pl.BlockSpec((mem_len, C_WIDTH), lambda b, i: (b, 1)),
            pl.BlockSpec((tq, C_WIDTH), lambda b, i: (b * nt + i, 1)),
        ],
        out_specs=pl.BlockSpec((tq, C_WIDTH), lambda b, i: (b * nt + i, 0)),
        out_shape=jax.ShapeDtypeStruct((bsz * seq, C_WIDTH), BF16),
        compiler_params=_params(2),
        name="memory_attention",
    )(p_c, kv, kv, p_c)


class _WkvConsts:
    def __init__(self):
        c, g, hd = WKV_CHUNK, WKV_GROUP, B_HEAD_DIM
        row = lax.broadcasted_iota(jnp.int32, (g, g), 0)
        col = lax.broadcasted_iota(jnp.int32, (g, g), 1)
        self.same_head = jnp.where((row // hd) == (col // hd), 1.0, 0.0)
        self.ones_bd = self.same_head.astype(BF16)
        t = lax.broadcasted_iota(jnp.int32, (c, g), 0)
        s = lax.broadcasted_iota(jnp.int32, (c, g), 1) % hd
        self.strict = s < t
        self.incl = s <= t
        self.strict_incl = jnp.concatenate([self.strict, self.incl], axis=0)
        self.eye = jnp.where(s == t, 1.0, 0.0).astype(F32)
        tr = lax.broadcasted_iota(jnp.int32, (c, c), 0)
        tc = lax.broadcasted_iota(jnp.int32, (c, c), 1)
        lower = jnp.where(tc <= tr, 1.0, 0.0).astype(BF16)
        self.lower2 = jnp.concatenate([lower, lower], axis=1)

    def bd(self, p):
        return jnp.concatenate([p] * WKV_HEADS_PER_GROUP, axis=0) * self.ones_bd

    def segsum(self, x):
        return jnp.dot(_bf(x), self.ones_bd, preferred_element_type=F32)


def _wkv_prepare(rm, kmod, vm, avec, bvec, logdec, k):
    c = WKV_CHUNK
    cat = lambda u, w: jnp.concatenate([u, w], axis=0)
    ld_hi = _each(_bf, logdec)
    ld_lo = _each(lambda x, hi: _bf(x - hi.astype(F32)), logdec, ld_hi)
    cl = _each(lambda hi, lo: jnp.dot(k.lower2, cat(hi, lo), preferred_element_type=F32), ld_hi, ld_lo)
    cl_last = _each(lambda x: x[c - 1:c, :], cl)
    rt = _each(lambda r, x: _bf(r * jnp.exp(x)), rm, cl)
    at = _each(lambda a, x, ld: _bf(a * jnp.exp(x - ld)), avec, cl, logdec)
    e_neg = _each(lambda x: jnp.exp(-x), cl)
    bt = _each(lambda b, e: k.bd(_bf(b * e)), bvec, e_neg)
    kt = _each(lambda kk, e: k.bd(_bf(kk * e)), kmod, e_neg)
    e_rem = _each(lambda xl, x: jnp.exp(xl - x), cl_last, cl)
    bkh = _each(lambda b, kk, e: cat(_bf(b * e), _bf(kk * e)), bvec, kmod, e_rem)
    vb = _each(_bf, vm)

    lhs = _each(cat, at, rt)
    ab = _each(lambda x, y: _mm(x, y, NT), lhs, bt)
    ak = _each(lambda x, y: _mm(x, y, NT), lhs, kt)
    a_ab = _each(lambda x: jnp.where(k.strict, x[:c], 0.0), ab)
    a_rb = _each(lambda x: _bf(jnp.where(k.incl, x[c:], 0.0)), ab)
    a_k = _each(lambda x: _bf(jnp.where(k.strict_incl, x, 0.0)), ak)

    inv = _each(lambda x: k.eye + x, a_ab)
    pw = _each(_bf, a_ab)
    pw = _each(lambda p: _bf(_mm(p, k.bd(p), NN)), pw)
    for _ in range(int(math.log2(c)) - 2):
        tp = _each(lambda i, p: _mm(cat(_bf(i), p), k.bd(p), NN), inv, pw)
        inv = _each(lambda i, t: i + t[:c], inv, tp)
        pw = _each(lambda t: _bf(t[c:]), tp)
    inv = _each(lambda i, p: _bf(i + _mm(_bf(i), k.bd(p), NN)), inv, pw)

    av = _each(lambda x, v: _mm(x, k.bd(v), NN), a_k, vb)
    w = _each(lambda i, a: _mm(i, k.bd(a), NN), inv, at)
    u0 = _each(lambda i, x: _mm(i, k.bd(_bf(x[:c])), NN), inv, av)
    wr = _each(lambda ww, r: cat(_bf(ww), r), w, rt)
    y0 = _each(lambda x: x[c:], av)
    decay = _each(jnp.exp, cl_last)
    return list(zip(wr, u0, y0, a_rb, vb, bkh, decay))


def _wkv_apply(prep, state, k):
    c = WKV_CHUNK
    wr, u0, y0, a_rb, vb, bkh, decay = (list(x) for x in zip(*prep))
    uy = _each(lambda x, s: _mm(x, _bf(s), NT), wr, state)
    ub = _each(lambda x, y: _bf(x[:c] + y), uy, u0)
    y = _each(lambda x, a, uu, z: x[c:] + _mm(a, k.bd(uu), NN) + z, uy, a_rb, ub, y0)
    upd = _each(lambda uu, v, bk: _mm(jnp.concatenate([uu, v], axis=0), bk, TN), ub, vb, bkh)
    new_state = _each(lambda s, dc, d: s * dc + d * k.same_head, state, decay, upd)
    return y, new_state


def _rwkv_kernel(r_ref, k_ref, v_ref, lwla_ref, z_ref, w0_ref, a0_ref, kk_ref, ka_ref, rk_ref, lnw_ref, lnb_ref,
                 wd_ref, wa_ref, o_ref, state_ref):
    c, g = WKV_CHUNK, WKV_GROUP
    n_groups = r_ref.shape[1] // g
    n_chunks = WKV_CHUNKS_PER_STEP

    @pl.when(pl.program_id(2) == 0)
    def _():
        state_ref[...] = jnp.zeros_like(state_ref)

    k = _WkvConsts()
    wd = _bf(wd_ref[...])
    wa = _bf(wa_ref[...])
    inv_hd = 1.0 / B_HEAD_DIM
    groups = [slice(gi * g, (gi + 1) * g) for gi in range(n_groups)]
    split = lambda x: [x[ch * c:(ch + 1) * c, ln] for ch in range(n_chunks) for ln in groups]

    def segsums(xs):
        s = k.segsum(jnp.concatenate(xs, axis=0))
        return [s[i * c:(i + 1) * c] for i in range(len(xs))]

    def step(ci, carry):
        rows = pl.ds(pl.multiple_of(ci * (n_chunks * c), n_chunks * c), n_chunks * c)
        rm, km, vm = r_ref[rows, :], k_ref[rows, :], v_ref[rows, :]
        lwla = lwla_ref[rows, :]
        lw = _bf(jnp.tanh(lwla[:, :LORA_PAD]))
        la = _bf(lwla[:, LORA_PAD:])
        logdec = -math.exp(-0.5) * _sigmoid(w0_ref[...] + jnp.dot(lw, wd, preferred_element_type=F32))
        a_lr = _sigmoid(a0_ref[...] + jnp.dot(la, wa, preferred_element_type=F32))
        kmod = km * (1.0 + (a_lr - 1.0) * ka_ref[...])
        kk = split(km * kk_ref[...])
        n = len(kk)
        sums = segsums(_each(lambda x: x * x, kk) + split(rm * kmod * rk_ref[...]))
        kk = _each(lambda x, ss: x * jnp.minimum(lax.rsqrt(ss), 1e12), kk, sums[:n])
        bonus = _each(lambda s, v: s * v, sums[n:], split(vm))
        prep = _wkv_prepare(split(rm), split(kmod), split(vm), _each(lambda x: -x, kk),
                            _each(lambda x, a: x * a, kk, split(a_lr)), split(logdec), k)
        state = [state_ref[gi] for gi in range(n_groups)]
        y = []
        for ch in range(n_chunks):
            y_ch, state = _wkv_apply(prep[ch * n_groups:(ch + 1) * n_groups], state, k)
            y += y_ch
        for gi in range(n_groups):
            state_ref[gi] = state[gi]
        yc = _each(lambda x, s: x - s * inv_hd, y, segsums(y))
        var = _each(lambda s: s * inv_hd, segsums(_each(lambda x: x * x, yc)))
        gate = _silu(z_ref[rows, :].astype(F32))
        for ch in range(n_chunks):
            out_rows = pl.ds(pl.multiple_of(ci * (n_chunks * c) + ch * c, c), c)
            for gi, ln in enumerate(groups):
                i = ch * n_groups + gi
                yn = yc[i] * lax.rsqrt(var[i] + LNX_EPS) * lnw_ref[:, ln] + lnb_ref[:, ln]
                o_ref[out_rows, ln] = ((yn + bonus[i]) * gate[ch * c:(ch + 1) * c, ln]).astype(o_ref.dtype)
        return carry

    lax.fori_loop(0, r_ref.shape[0] // (n_chunks * c), step, 0)


def _rwkv(p_b, p_lora, p_zb, vecs, wd2, wa2, bsz, seq, tt):
    g = WKV_GROUP
    gw = WKV_GROUPS_PER_STEP * g
    ns = B_WIDTH // gw
    nt = seq // tt
    row = lambda b, gi, t: b * nt + t
    vec_spec = pl.BlockSpec((1, gw), lambda b, gi, t: (0, gi))
    lora_w_spec = pl.BlockSpec((LORA_PAD, gw), lambda b, gi, t: (0, gi))
    return pl.pallas_call(
        _rwkv_kernel,
        grid=(bsz, ns, nt),
        in_specs=[
            pl.BlockSpec((tt, gw), lambda b, gi, t: (row(b, gi, t), gi)),
            pl.BlockSpec((tt, gw), lambda b, gi, t: (row(b, gi, t), ns + gi)),
            pl.BlockSpec((tt, gw), lambda b, gi, t: (row(b, gi, t), 2 * ns + gi)),
            pl.BlockSpec((tt, 2 * LORA_PAD), lambda b, gi, t: (row(b, gi, t), 0)),
            pl.BlockSpec((tt, gw), lambda b, gi, t: (row(b, gi, t), gi)),
        ] + [vec_spec] * len(vecs) + [lora_w_spec, lora_w_spec],
        out_specs=pl.BlockSpec((tt, gw), lambda b, gi, t: (row(b, gi, t), gi)),
        out_shape=jax.ShapeDtypeStruct((bsz * seq, B_WIDTH), BF16),
        scratch_shapes=[pltpu.VMEM((WKV_GROUPS_PER_STEP, g, g), F32)],
        compiler_params=_params(3),
        name="rwkv7_time_mix",
    )(p_b, p_b, p_b, p_lora, p_zb, *vecs, wd2, wa2)


def _merge_kernel(ya_ref, yb_ref, yc_ref, wa_ref, wb_ref, wc_ref, ga_ref, gb_ref, gc_ref, o_ref):
    def branch(y_ref, w_ref, g_ref):
        return _sigmoid(g_ref[...].astype(F32)) * jnp.dot(y_ref[...], w_ref[...], preferred_element_type=F32)

    o_ref[...] = (branch(ya_ref, wa_ref, ga_ref) + branch(yb_ref, wb_ref, gb_ref) + branch(yc_ref, wc_ref, gc_ref)).astype(o_ref.dtype)


def _merge(ya, yb, yc, wa, wb, wc, p_c, gate_col0, d_model, tm, tn):
    m = ya.shape[0]
    nj = d_model // tn
    j0 = gate_col0 // tn
    y_spec = lambda width: pl.BlockSpec((tm, width), lambda j, i: (i, 0))
    w_spec = lambda width: pl.BlockSpec((width, tn), lambda j, i: (0, j))
    g_spec = lambda br: pl.BlockSpec((tm, tn), lambda j, i: (i, j0 + br * nj + j))
    return pl.pallas_call(
        _merge_kernel,
        grid=(nj, m // tm),
        in_specs=[y_spec(A_WIDTH), y_spec(B_WIDTH), y_spec(C_WIDTH), w_spec(A_WIDTH), w_spec(B_WIDTH), w_spec(C_WIDTH),
                  g_spec(0), g_spec(1), g_spec(2)],
        out_specs=pl.BlockSpec((tm, tn), lambda j, i: (i, j)),
        out_shape=jax.ShapeDtypeStruct((m, d_model), BF16),
        compiler_params=_params(2, VMEM_LIMIT_PROJ),
        name="gated_merge",
    )(ya, yb, yc, wa, wb, wc, p_c, p_c, p_c)


def _out_kernel(m_ref, w_ref, x_ref, g_ref, o_ref, *, final_norm):
    y = x_ref[...] + jnp.dot(m_ref[...], w_ref[...], preferred_element_type=F32)
    if final_norm:
        y = y * lax.rsqrt(jnp.mean(y * y, axis=-1, keepdims=True) + RMS_EPS) * g_ref[...]
    o_ref[...] = y


def _out_proj(merged, w_out, x2d, g, final_norm, tm):
    m, d = x2d.shape
    return pl.pallas_call(
        functools.partial(_out_kernel, final_norm=final_norm),
        grid=(m // tm,),
        in_specs=[pl.BlockSpec((tm, d), lambda i: (i, 0)), pl.BlockSpec((d, d), lambda i: (0, 0)),
                  pl.BlockSpec((tm, d), lambda i: (i, 0)), pl.BlockSpec((1, d), lambda i: (0, 0))],
        out_specs=pl.BlockSpec((tm, d), lambda i: (i, 0)),
        out_shape=jax.ShapeDtypeStruct((m, d), F32),
        compiler_params=_params(1),
        name="out_proj",
    )(merged, w_out, x2d, g.reshape(1, d))


def _pick_tile(n, candidates):
    for t in candidates:
        if n % t == 0:
            return t
    raise ValueError(f"no tile for extent {n}")


class _Tiles(NamedTuple):
    norm_rows: int
    proj_rows: int
    proj_cols: int
    proj_cols_b: int
    mem_rows: int
    cross_rows: int
    wkv_rows: int
    merge_rows: int
    merge_cols: int
    out_rows: int


def _tiles(m, seq, mem_rows):
    return _Tiles(
        norm_rows=_pick_tile(m, (1024, 512, 256)),
        proj_rows=_pick_tile(seq, (2048, 1024, 512, 256)),
        proj_cols=1024,
        proj_cols_b=768,
        mem_rows=_pick_tile(mem_rows, (1024, 512, 256)),
        cross_rows=_pick_tile(seq, (512, 256)),
        wkv_rows=_pick_tile(seq, (512, 256, 128, 64)),
        merge_rows=_pick_tile(m, (1024, 512, 256)),
        merge_cols=1024,
        out_rows=_pick_tile(m, (512, 256)),
    )


def _pad_cols(w, width):
    return jnp.pad(w, ((0, 0), (0, width - w.shape[1])))


def _layer(x2d, mem2d, bsz, seq, mem_len, bias_tiles, rel_bias, norm_g, mem_norm_g, w_in, rw, w_mem_kv, w_proj_a, w_proj_b,
           w_proj_c, w_out, final_g):
    m, d = x2d.shape
    t = _tiles(m, seq, mem2d.shape[0])

    c_z_a = 3 * A_WIDTH
    c_rkv_b = c_z_a + A_WIDTH
    c_z_b = c_rkv_b + 3 * B_WIDTH
    c_lw = c_z_b + B_WIDTH
    c_la = c_lw + LORA
    c_q_c = c_la + LORA

    h = _rmsnorm(x2d, norm_g, BF16, t.norm_rows)
    ones = lambda n: jnp.ones((1, n), F32)
    wt = w_in.T
    scale_a = jnp.concatenate([jnp.full((1, A_WIDTH), MOBA_Q_SCALE, F32), ones(3 * A_WIDTH)], axis=1)
    p_a = _matmul_wt(h, wt, 0, 4 * A_WIDTH, scale_a, BF16, t.proj_rows, t.proj_cols, "in_proj_a")
    (mu_r, mu_k, mu_v, mu_w, mu_a, w0, w_decay2, a0, w_aaa2, k_k, k_a, r_k, lnx_w, lnx_b) = rw
    vec = lambda v: v.reshape(1, -1)
    p_b = _matmul_wt(h, wt, c_rkv_b, 3 * B_WIDTH, jnp.concatenate([vec(mu_r), vec(mu_k), vec(mu_v)], axis=1), F32,
                     t.proj_rows, t.proj_cols_b, "in_proj_b", seq_rows=seq)
    p_zb = _matmul_wt(h, wt, c_z_b, B_WIDTH, ones(B_WIDTH), BF16, t.proj_rows, t.proj_cols_b, "in_proj_zb")
    pad_rows = lambda w: jnp.pad(w, ((0, LORA_PAD - LORA), (0, 0)))
    wt_lora = jnp.concatenate([pad_rows(wt[c_lw:c_la]), pad_rows(wt[c_la:c_q_c])], axis=0)
    mu_wa = jnp.concatenate([_pad_cols(vec(mu_w), LORA_PAD), _pad_cols(vec(mu_a), LORA_PAD)], axis=1)
    p_lora = _matmul_wt(h, wt_lora, 0, 2 * LORA_PAD, mu_wa, F32, t.proj_rows, 2 * LORA_PAD, "in_proj_lora", seq_rows=seq)
    n_c = 2 * C_WIDTH + N_BRANCHES * d
    p_c = _matmul_wt(h, wt, c_q_c, n_c, ones(n_c), BF16, t.proj_rows, t.proj_cols, "in_proj_c")

    mem_n = _rmsnorm(mem2d, mem_norm_g, BF16, t.mem_rows)
    kv = _matmul(mem_n, w_mem_kv.astype(BF16), BF16, t.mem_rows, t.proj_cols, "mem_kv")

    ya = _moba(p_a, rel_bias, bias_tiles, bsz, seq)
    yc = _cross(p_c, kv, bsz, seq, mem_len, t.cross_rows)

    vecs = (vec(w0), vec(a0), vec(k_k), vec(k_a), vec(r_k), vec(lnx_w), vec(lnx_b))
    yb = _rwkv(p_b, p_lora, p_zb, vecs, pad_rows(w_decay2), pad_rows(w_aaa2), bsz, seq, t.wkv_rows)

    merged = _merge(ya, yb, yc, w_proj_a.astype(BF16), w_proj_b.astype(BF16), w_proj_c.astype(BF16), p_c, 2 * C_WIDTH, d,
                    t.merge_rows, t.merge_cols)
    g = final_g if final_g is not None else jnp.ones((d,), F32)
    return _out_proj(merged, w_out.astype(BF16), x2d, g, final_g is not None, t.out_rows)


def kernel(x, mem, rel_bias, norm_g, mem_norm_g, w_in, rw_mu_r, rw_mu_k, rw_mu_v, rw_mu_w, rw_mu_a, rw_w0, rw_w_decay2, rw_a0, rw_w_aaa2, rw_k_k, rw_k_a, rw_r_k, rw_lnx_w, rw_lnx_b, w_mem_kv, w_proj_a, w_proj_b, w_proj_c, w_out, final_norm_g):
    bsz, seq, d = x.shape
    mem_len = mem.shape[1]
    depth = norm_g.shape[0]
    x2d = x.reshape(bsz * seq, d)
    mem2d = mem.reshape(bsz * mem_len, d)
    bias_tiles = _bias_tiles(rel_bias)
    for l in range(depth):
        rw = (rw_mu_r[l], rw_mu_k[l], rw_mu_v[l], rw_mu_w[l], rw_mu_a[l], rw_w0[l], rw_w_decay2[l], rw_a0[l], rw_w_aaa2[l],
              rw_k_k[l], rw_k_a[l], rw_r_k[l], rw_lnx_w[l], rw_lnx_b[l])
        x2d = _layer(x2d, mem2d, bsz, seq, mem_len, bias_tiles, rel_bias, norm_g[l], mem_norm_g[l], w_in[l], rw, w_mem_kv[l],
                     w_proj_a[l], w_proj_b[l], w_proj_c[l], w_out[l], final_norm_g if l == depth - 1 else None)
    return x2d.reshape(bsz, seq, d)
```

```python
import functools
import math
from typing import NamedTuple

import jax
import jax.numpy as jnp
from jax import lax
from jax.experimental import pallas as pl
from jax.experimental.pallas import tpu as pltpu

F32 = jnp.float32
BF16 = jnp.bfloat16
LANES = 128
BF16_SUBLANES = 16

RMS_EPS = 1e-6

A_HEADS = 12
A_HEAD_DIM = 128
A_WIDTH = A_HEADS * A_HEAD_DIM
MOBA_BLOCK = 256
MOBA_TOPK = 3
MOBA_HEADS_PER_STEP = 6
MOBA_FAR_GROUP = 4
MOBA_Q_SCALE = A_HEAD_DIM ** -0.5 * math.log2(math.e)
REL_BUCKETS = 32
REL_MAX_DIST = 128

B_HEAD_DIM = 64
B_WIDTH = 1536
LORA = 96
LORA_PAD = LANES
LNX_EPS = 64e-5
WKV_CHUNK = 64
WKV_GROUP = 256
WKV_HEADS_PER_GROUP = WKV_GROUP // B_HEAD_DIM
WKV_GROUPS_PER_STEP = 6
WKV_CHUNKS_PER_STEP = 4

C_HEADS = 4
C_HEAD_DIM = 256
C_WIDTH = C_HEADS * C_HEAD_DIM

N_BRANCHES = 3

VMEM_LIMIT = 48 * 1024 * 1024
VMEM_LIMIT_PROJ = 58 * 1024 * 1024

NN = (((1,), (0,)), ((), ()))
NT = (((1,), (1,)), ((), ()))
TN = (((0,), (0,)), ((), ()))

MASKED = -1e30
LOG2E = math.log2(math.e)


def _params(n_axes, vmem_limit=VMEM_LIMIT):
    return pltpu.CompilerParams(dimension_semantics=("arbitrary",) * n_axes, vmem_limit_bytes=vmem_limit)


def _sigmoid(x):
    return 0.5 * jnp.tanh(0.5 * x) + 0.5


def _silu(z):
    return z * _sigmoid(z)


def _bf(x):
    return x.astype(BF16)


def _mm(a, b, dn):
    return lax.dot_general(a, b, dn, preferred_element_type=F32)


def _each(f, *lists):
    return [f(*args) for args in zip(*lists)]


def _rmsnorm_kernel(x_ref, g_ref, o_ref):
    x = x_ref[...].astype(F32)
    y = x * lax.rsqrt(jnp.mean(x * x, axis=-1, keepdims=True) + RMS_EPS)
    o_ref[...] = (y * g_ref[...]).astype(o_ref.dtype)


def _rmsnorm(x2d, g, out_dtype, tm):
    m, d = x2d.shape
    return pl.pallas_call(
        _rmsnorm_kernel,
        grid=(m // tm,),
        in_specs=[pl.BlockSpec((tm, d), lambda i: (i, 0)), pl.BlockSpec((1, d), lambda i: (0, 0))],
        out_specs=pl.BlockSpec((tm, d), lambda i: (i, 0)),
        out_shape=jax.ShapeDtypeStruct((m, d), out_dtype),
        compiler_params=_params(1),
        name="rmsnorm",
    )(x2d, g.reshape(1, d))


def _matmul_kernel(a_ref, w_ref, o_ref):
    o_ref[...] = jnp.dot(a_ref[...], w_ref[...], preferred_element_type=F32).astype(o_ref.dtype)


def _matmul(a, w, out_dtype, tm, tn, name):
    m, k = a.shape
    n = w.shape[1]
    return pl.pallas_call(
        _matmul_kernel,
        grid=(n // tn, m // tm),
        in_specs=[pl.BlockSpec((tm, k), lambda j, i: (i, 0)), pl.BlockSpec((k, tn), lambda j, i: (0, j))],
        out_specs=pl.BlockSpec((tm, tn), lambda j, i: (i, j)),
        out_shape=jax.ShapeDtypeStruct((m, n), out_dtype),
        compiler_params=_params(2),
        name=name,
    )(a, w)


def _matmul_wt_kernel(a_ref, wt_ref, s_ref, o_ref, wb_ref, *carry_ref, seq_rows):
    i = pl.program_id(1)

    @pl.when(i == 0)
    def _():
        wb_ref[...] = wt_ref[...].astype(BF16)
        for ref in carry_ref:
            ref[...] = jnp.zeros_like(ref)

    x = lax.dot_general(a_ref[...], wb_ref[...], NT, preferred_element_type=F32)
    if seq_rows is None:
        o_ref[...] = (x * s_ref[...]).astype(o_ref.dtype)
    else:
        tm = x.shape[0]
        last, = carry_ref
        first_row = lax.broadcasted_iota(jnp.int32, x.shape, 0) == 0
        carried = jnp.where((i * tm) % seq_rows == 0, 0.0, last[0:1, :])
        prev = jnp.where(first_row, carried, pltpu.roll(x, 1, 0))
        last[0:1, :] = x[tm - 1:tm, :]
        o_ref[...] = (x + (prev - x) * s_ref[...]).astype(o_ref.dtype)


def _matmul_wt(a, wt, row0, n, col_vec, out_dtype, tm, tn, name, seq_rows=None):
    m, k = a.shape
    scratch = [pltpu.VMEM((tn, k), BF16)]
    if seq_rows is not None:
        assert seq_rows % tm == 0
        scratch.append(pltpu.VMEM((8, tn), F32))
    return pl.pallas_call(
        functools.partial(_matmul_wt_kernel, seq_rows=seq_rows),
        grid=(n // tn, m // tm),
        in_specs=[pl.BlockSpec((tm, k), lambda j, i: (i, 0)),
                  pl.BlockSpec((pl.Element(tn), pl.Element(k)), lambda j, i: (pl.multiple_of(row0 + j * tn, 8), 0)),
                  pl.BlockSpec((1, tn), lambda j, i: (0, j))],
        out_specs=pl.BlockSpec((tm, tn), lambda j, i: (i, j)),
        out_shape=jax.ShapeDtypeStruct((m, n), out_dtype),
        scratch_shapes=scratch,
        compiler_params=_params(2, VMEM_LIMIT_PROJ),
        name=name,
    )(a, wt, col_vec)


def _t5_bucket(dist):
    n = jnp.maximum(dist, 0)
    max_exact = REL_BUCKETS // 2
    nf = jnp.maximum(n, max_exact).astype(F32)
    large = max_exact + (jnp.log(nf / max_exact) / math.log(REL_MAX_DIST / max_exact) * (REL_BUCKETS - max_exact)).astype(jnp.int32)
    large = jnp.minimum(large, REL_BUCKETS - 1)
    return jnp.where(n < max_exact, n, large)


def _bias_kernel(rel_ref, bucket_ref, o_ref):
    h = pl.program_id(0)
    bucket = bucket_ref[...]
    acc = jnp.zeros(bucket.shape, F32)
    for b in range(REL_BUCKETS):
        acc = jnp.where(bucket == b, rel_ref[b, h], acc)
    qpos = lax.broadcasted_iota(jnp.int32, bucket.shape, 0) + MOBA_BLOCK
    kpos = lax.broadcasted_iota(jnp.int32, bucket.shape, 1)
    o_ref[0] = jnp.where(kpos <= qpos, acc * LOG2E, MASKED)


def _bias_tiles(rel_bias):
    blk = MOBA_BLOCK
    qpos = lax.broadcasted_iota(jnp.int32, (blk, 2 * blk), 0) + blk
    kpos = lax.broadcasted_iota(jnp.int32, (blk, 2 * blk), 1)
    bucket = _t5_bucket(qpos - kpos)
    return pl.pallas_call(
        _bias_kernel,
        grid=(A_HEADS,),
        in_specs=[pl.BlockSpec(memory_space=pltpu.SMEM), pl.BlockSpec((blk, 2 * blk), lambda h: (0, 0))],
        out_specs=pl.BlockSpec((1, blk, 2 * blk), lambda h: (h, 0, 0)),
        out_shape=jax.ShapeDtypeStruct((A_HEADS, blk, 2 * blk), F32),
        compiler_params=_params(1),
        name="t5_bias",
    )(rel_bias, bucket)


def _moba_kernel(rel_ref, q_ref, k_ref, v_ref, z_ref, bias_ref, o_ref, kmean_ref, m_ref, acc_ref):
    hb = pl.program_id(1)
    qi = pl.program_id(2)
    blk, hd = MOBA_BLOCK, A_HEAD_DIM
    n_heads = q_ref.shape[1] // hd
    nb = k_ref.shape[0] // blk
    nbp = kmean_ref.shape[1]

    @pl.when(qi == 0)
    def _():
        kmean_ref[...] = jnp.zeros_like(kmean_ref)
        for j in range(nb):
            mean_j = jnp.sum(k_ref[j * blk:(j + 1) * blk, :].astype(F32), axis=0, keepdims=True) * (1.0 / blk)
            for hh in range(n_heads):
                kmean_ref[hh, j:j + 1, :] = mean_j[:, hh * hd:(hh + 1) * hd]

    n_far = min(MOBA_FAR_GROUP, nb)
    never = LANES - 1
    blk_id = lax.broadcasted_iota(jnp.int32, (nbp, blk), 0)
    blk_f = blk_id.astype(F32)
    eye = jnp.where(lax.broadcasted_iota(jnp.int32, (nbp, LANES), 0) == lax.broadcasted_iota(jnp.int32, (nbp, LANES), 1), 1.0, 0.0).astype(BF16)
    j_prev = jnp.maximum(qi - 1, 0)
    own = pl.multiple_of(qi * blk, blk)
    prev = pl.multiple_of(j_prev * blk, blk)

    heads = [slice(hh * hd, (hh + 1) * hd) for hh in range(n_heads)]
    q = [q_ref[:, hl] for hl in heads]

    def gate_of(qh, hh):
        km = kmean_ref[hh]
        km_hi = km.astype(BF16)
        km_lo = (km - km_hi.astype(F32)).astype(BF16)
        return jnp.where(blk_id < qi, _mm(km_hi, qh, NT) + _mm(km_lo, qh, NT), -jnp.inf)

    g = _each(gate_of, q, range(n_heads))
    sel_t = [jnp.zeros((nbp, blk), F32) for _ in heads]
    for _ in range(MOBA_TOPK):
        gmax = _each(lambda x: jnp.max(x, axis=0, keepdims=True), g)
        first = _each(lambda x, mx: jnp.min(jnp.where(x == mx, blk_f, float(nbp)), axis=0, keepdims=True), g, gmax)
        pick = _each(lambda f, mx: (blk_f == f) & (mx > -jnp.inf), first, gmax)
        sel_t = _each(lambda p, s: jnp.where(p, 1.0, s), pick, sel_t)
        g = _each(lambda p, x: jnp.where(p, -jnp.inf, x), pick, g)
    sel = _each(lambda s: _mm(s.astype(BF16), eye, TN), sel_t)
    qm = _each(lambda qh, s: jnp.concatenate([qh, jnp.where(s > 0.0, 0.0, MASKED).astype(BF16)], axis=1), q, sel)

    def masked_logits(keys, key_blk):
        col = jnp.where(lax.broadcasted_iota(jnp.int32, key_blk.shape, 1) == key_blk, 1.0, 0.0).astype(BF16)
        return _each(lambda x, kk: _mm(x, jnp.concatenate([kk, col], axis=1), NT), qm, keys)

    def pv(p, values):
        ones = jnp.ones(values[0].shape, BF16)
        return _each(lambda x, v: jnp.dot(x.astype(BF16), jnp.concatenate([v, ones], axis=1), preferred_element_type=F32), p, values)

    cat0 = lambda ref, hl: jnp.concatenate([ref[pl.ds(prev, blk), hl], ref[pl.ds(own, blk), hl]], axis=0)
    key_row = lax.broadcasted_iota(jnp.int32, (2 * blk, LANES), 0)
    raw = masked_logits([cat0(k_ref, hl) for hl in heads], jnp.where(key_row < blk, j_prev, -1))
    t = _each(lambda x, hh: x + bias_ref[hh], raw, range(n_heads))
    m = _each(lambda x: jnp.max(x, axis=-1, keepdims=True), t)
    p = _each(lambda x, mx: jnp.exp2(x - mx), t, m)
    acc = pv(p, [cat0(v_ref, hl) for hl in heads])
    for hh in range(n_heads):
        m_ref[hh] = m[hh]
        acc_ref[hh] = acc[hh]

    bias_far = [rel_ref[REL_BUCKETS - 1, hb * n_heads + hh] * LOG2E for hh in range(n_heads)]
    far_row_blk = lax.broadcasted_iota(jnp.int32, (n_far * blk, LANES), 0) // blk

    def body(gi, carry):
        rows = pl.ds(pl.multiple_of(gi * (n_far * blk), n_far * blk), n_far * blk)
        key_blk = gi * n_far + far_row_blk
        raw = masked_logits([k_ref[rows, hl] for hl in heads], jnp.where(key_blk < j_prev, key_blk, never))
        m = [m_ref[hh] for hh in range(n_heads)]
        m_new = _each(lambda mx, x, b: jnp.maximum(mx, jnp.max(x, axis=-1, keepdims=True) + b), m, raw, bias_far)
        p = _each(lambda x, mn, b: jnp.exp2(x + (b - mn)), raw, m_new, bias_far)
        alpha = _each(lambda mx, mn: jnp.exp2(mx - mn), m, m_new)
        upd = pv(p, [v_ref[rows, hl] for hl in heads])
        for hh in range(n_heads):
            m_ref[hh] = m_new[hh]
            acc_ref[hh] = alpha[hh] * acc_ref[hh] + upd[hh]
        return carry

    lax.fori_loop(0, (j_prev + n_far - 1) // n_far, body, 0)
    for hh, hl in enumerate(heads):
        y = acc_ref[hh, :, :hd] / acc_ref[hh, :, hd:]
        o_ref[:, hl] = (y * _silu(z_ref[:, hl].astype(F32))).astype(o_ref.dtype)


def _moba(p_a, rel_bias, bias_tiles, bsz, seq):
    blk = MOBA_BLOCK
    nq = seq // blk
    hw = MOBA_HEADS_PER_STEP * A_HEAD_DIM
    ns = A_WIDTH // hw
    nbp = -(-nq // BF16_SUBLANES) * BF16_SUBLANES
    assert nbp < LANES
    return pl.pallas_call(
        _moba_kernel,
        grid=(bsz, ns, nq),
        in_specs=[
            pl.BlockSpec(memory_space=pltpu.SMEM),
            pl.BlockSpec((blk, hw), lambda b, h, i: (b * nq + i, h)),
            pl.BlockSpec((seq, hw), lambda b, h, i: (b, ns + h)),
            pl.BlockSpec((seq, hw), lambda b, h, i: (b, 2 * ns + h)),
            pl.BlockSpec((blk, hw), lambda b, h, i: (b * nq + i, 3 * ns + h)),
            pl.BlockSpec((MOBA_HEADS_PER_STEP, blk, 2 * blk), lambda b, h, i: (h, 0, 0)),
        ],
        out_specs=pl.BlockSpec((blk, hw), lambda b, h, i: (b * nq + i, h)),
        out_shape=jax.ShapeDtypeStruct((bsz * seq, A_WIDTH), BF16),
        scratch_shapes=[pltpu.VMEM((MOBA_HEADS_PER_STEP, nbp, A_HEAD_DIM), F32), pltpu.VMEM((MOBA_HEADS_PER_STEP, blk, 1), F32),
                        pltpu.VMEM((MOBA_HEADS_PER_STEP, blk, 2 * A_HEAD_DIM), F32)],
        compiler_params=_params(3),
        name="moba_attention",
    )(rel_bias, p_a, p_a, p_a, p_a, bias_tiles)


def _cross_kernel(q_ref, k_ref, v_ref, z_ref, o_ref):
    hd = C_HEAD_DIM
    heads = [slice(h * hd, (h + 1) * hd) for h in range(C_HEADS)]
    s = _each(lambda hl: _mm(q_ref[:, hl], k_ref[:, hl], NT) * (hd ** -0.5), heads)
    m = _each(lambda x: jnp.max(x, axis=-1, keepdims=True), s)
    p = _each(lambda x, mx: jnp.exp(x - mx), s, m)
    l = _each(lambda x: jnp.sum(x, axis=-1, keepdims=True), p)
    y = _each(lambda x, hl: jnp.dot(x.astype(BF16), v_ref[:, hl], preferred_element_type=F32), p, heads)
    for hl, yh, lh in zip(heads, y, l):
        o_ref[:, hl] = (yh / lh * _silu(z_ref[:, hl].astype(F32))).astype(o_ref.dtype)


def _cross(p_c, kv, bsz, seq, mem_len, tq):
    nt = seq // tq
    return pl.pallas_call(
        _cross_kernel,
        grid=(bsz, nt),
        in_specs=[
            pl.BlockSpec((tq, C_WIDTH), lambda b, i: (b * nt + i, 0)),
            pl.BlockSpec((mem_len, C_WIDTH), lambda b, i: (b, 0)),
            pl.BlockSpec((mem_len, C_WIDTH), lambda b, i: (b, 1)),
            pl.BlockSpec((tq, C_WIDTH), lambda b, i: (b * nt + i, 1)),
        ],
        out_specs=pl.BlockSpec((tq, C_WIDTH), lambda b, i: (b * nt + i, 0)),
        out_shape=jax.ShapeDtypeStruct((bsz * seq, C_WIDTH), BF16),
        compiler_params=_params(2),
        name="memory_attention",
    )(p_c, kv, kv, p_c)


class _WkvConsts:
    def __init__(self):
        c, g, hd = WKV_CHUNK, WKV_GROUP, B_HEAD_DIM
        row = lax.broadcasted_iota(jnp.int32, (g, g), 0)
        col = lax.broadcasted_iota(jnp.int32, (g, g), 1)
        self.same_head = jnp.where((row // hd) == (col // hd), 1.0, 0.0)
        self.ones_bd = self.same_head.astype(BF16)
        t = lax.broadcasted_iota(jnp.int32, (c, g), 0)
        s = lax.broadcasted_iota(jnp.int32, (c, g), 1) % hd
        self.strict = s < t
        self.incl = s <= t
        self.strict_incl = jnp.concatenate([self.strict, self.incl], axis=0)
        self.eye = jnp.where(s == t, 1.0, 0.0).astype(F32)
        tr = lax.broadcasted_iota(jnp.int32, (c, c), 0)
        tc = lax.broadcasted_iota(jnp.int32, (c, c), 1)
        lower = jnp.where(tc <= tr, 1.0, 0.0).astype(BF16)
        self.lower2 = jnp.concatenate([lower, lower], axis=1)

    def bd(self, p):
        return jnp.concatenate([p] * WKV_HEADS_PER_GROUP, axis=0) * self.ones_bd

    def segsum(self, x):
        return jnp.dot(_bf(x), self.ones_bd, preferred_element_type=F32)


def _wkv_prepare(rm, kmod, vm, avec, bvec, logdec, k):
    c = WKV_CHUNK
    cat = lambda u, w: jnp.concatenate([u, w], axis=0)
    ld_hi = _each(_bf, logdec)
    ld_lo = _each(lambda x, hi: _bf(x - hi.astype(F32)), logdec, ld_hi)
    cl = _each(lambda hi, lo: jnp.dot(k.lower2, cat(hi, lo), preferred_element_type=F32), ld_hi, ld_lo)
    cl_last = _each(lambda x: x[c - 1:c, :], cl)
    rt = _each(lambda r, x: _bf(r * jnp.exp(x)), rm, cl)
    at = _each(lambda a, x, ld: _bf(a * jnp.exp(x - ld)), avec, cl, logdec)
    e_neg = _each(lambda x: jnp.exp(-x), cl)
    bt = _each(lambda b, e: k.bd(_bf(b * e)), bvec, e_neg)
    kt = _each(lambda kk, e: k.bd(_bf(kk * e)), kmod, e_neg)
    e_rem = _each(lambda xl, x: jnp.exp(xl - x), cl_last, cl)
    bkh = _each(lambda b, kk, e: cat(_bf(b * e), _bf(kk * e)), bvec, kmod, e_rem)
    vb = _each(_bf, vm)

    lhs = _each(cat, at, rt)
    ab = _each(lambda x, y: _mm(x, y, NT), lhs, bt)
    ak = _each(lambda x, y: _mm(x, y, NT), lhs, kt)
    a_ab = _each(lambda x: jnp.where(k.strict, x[:c], 0.0), ab)
    a_rb = _each(lambda x: _bf(jnp.where(k.incl, x[c:], 0.0)), ab)
    a_k = _each(lambda x: _bf(jnp.where(k.strict_incl, x, 0.0)), ak)

    inv = _each(lambda x: k.eye + x, a_ab)
    pw = _each(_bf, a_ab)
    pw = _each(lambda p: _bf(_mm(p, k.bd(p), NN)), pw)
    for _ in range(int(math.log2(c)) - 2):
        tp = _each(lambda i, p: _mm(cat(_bf(i), p), k.bd(p), NN), inv, pw)
        inv = _each(lambda i, t: i + t[:c], inv, tp)
        pw = _each(lambda t: _bf(t[c:]), tp)
    inv = _each(lambda i, p: _bf(i + _mm(_bf(i), k.bd(p), NN)), inv, pw)

    av = _each(lambda x, v: _mm(x, k.bd(v), NN), a_k, vb)
    w = _each(lambda i, a: _mm(i, k.bd(a), NN), inv, at)
    u0 = _each(lambda i, x: _mm(i, k.bd(_bf(x[:c])), NN), inv, av)
    wr = _each(lambda ww, r: cat(_bf(ww), r), w, rt)
    y0 = _each(lambda x: x[c:], av)
    decay = _each(jnp.exp, cl_last)
    return list(zip(wr, u0, y0, a_rb, vb, bkh, decay))


def _wkv_apply(prep, state, k):
    c = WKV_CHUNK
    wr, u0, y0, a_rb, vb, bkh, decay = (list(x) for x in zip(*prep))
    uy = _each(lambda x, s: _mm(x, _bf(s), NT), wr, state)
    ub = _each(lambda x, y: _bf(x[:c] + y), uy, u0)
    y = _each(lambda x, a, uu, z: x[c:] + _mm(a, k.bd(uu), NN) + z, uy, a_rb, ub, y0)
    upd = _each(lambda uu, v, bk: _mm(jnp.concatenate([uu, v], axis=0), bk, TN), ub, vb, bkh)
    new_state = _each(lambda s, dc, d: s * dc + d * k.same_head, state, decay, upd)
    return y, new_state


def _rwkv_kernel(r_ref, k_ref, v_ref, lwla_ref, z_ref, w0_ref, a0_ref, kk_ref, ka_ref, rk_ref, lnw_ref, lnb_ref,
                 wd_ref, wa_ref, o_ref, state_ref):
    c, g = WKV_CHUNK, WKV_GROUP
    n_groups = r_ref.shape[1] // g
    n_chunks = WKV_CHUNKS_PER_STEP

    @pl.when(pl.program_id(2) == 0)
    def _():
        state_ref[...] = jnp.zeros_like(state_ref)

    k = _WkvConsts()
    wd = _bf(wd_ref[...])
    wa = _bf(wa_ref[...])
    inv_hd = 1.0 / B_HEAD_DIM
    groups = [slice(gi * g, (gi + 1) * g) for gi in range(n_groups)]
    split = lambda x: [x[ch * c:(ch + 1) * c, ln] for ch in range(n_chunks) for ln in groups]

    def segsums(xs):
        s = k.segsum(jnp.concatenate(xs, axis=0))
        return [s[i * c:(i + 1) * c] for i in range(len(xs))]

    def step(ci, carry):
        rows = pl.ds(pl.multiple_of(ci * (n_chunks * c), n_chunks * c), n_chunks * c)
        rm, km, vm = r_ref[rows, :], k_ref[rows, :], v_ref[rows, :]
        lwla = lwla_ref[rows, :]
        lw = _bf(jnp.tanh(lwla[:, :LORA_PAD]))
        la = _bf(lwla[:, LORA_PAD:])
        logdec = -math.exp(-0.5) * _sigmoid(w0_ref[...] + jnp.dot(lw, wd, preferred_element_type=F32))
        a_lr = _sigmoid(a0_ref[...] + jnp.dot(la, wa, preferred_element_type=F32))
        kmod = km * (1.0 + (a_lr - 1.0) * ka_ref[...])
        kk = split(km * kk_ref[...])
        n = len(kk)
        sums = segsums(_each(lambda x: x * x, kk) + split(rm * kmod * rk_ref[...]))
        kk = _each(lambda x, ss: x * jnp.minimum(lax.rsqrt(ss), 1e12), kk, sums[:n])
        bonus = _each(lambda s, v: s * v, sums[n:], split(vm))
        prep = _wkv_prepare(split(rm), split(kmod), split(vm), _each(lambda x: -x, kk),
                            _each(lambda x, a: x * a, kk, split(a_lr)), split(logdec), k)
        state = [state_ref[gi] for gi in range(n_groups)]
        y = []
        for ch in range(n_chunks):
            y_ch, state = _wkv_apply(prep[ch * n_groups:(ch + 1) * n_groups], state, k)
            y += y_ch
        for gi in range(n_groups):
            state_ref[gi] = state[gi]
        yc = _each(lambda x, s: x - s * inv_hd, y, segsums(y))
        var = _each(lambda s: s * inv_hd, segsums(_each(lambda x: x * x, yc)))
        gate = _silu(z_ref[rows, :].astype(F32))
        for ch in range(n_chunks):
            out_rows = pl.ds(pl.multiple_of(ci * (n_chunks * c) + ch * c, c), c)
            for gi, ln in enumerate(groups):
                i = ch * n_groups + gi
                yn = yc[i] * lax.rsqrt(var[i] + LNX_EPS) * lnw_ref[:, ln] + lnb_ref[:, ln]
                o_ref[out_rows, ln] = ((yn + bonus[i]) * gate[ch * c:(ch + 1) * c, ln]).astype(o_ref.dtype)
        return carry

    lax.fori_loop(0, r_ref.shape[0] // (n_chunks * c), step, 0)


def _rwkv(p_b, p_lora, p_zb, vecs, wd2, wa2, bsz, seq, tt):
    g = WKV_GROUP
    gw = WKV_GROUPS_PER_STEP * g
    ns = B_WIDTH // gw
    nt = seq // tt
    row = lambda b, gi, t: b * nt + t
    vec_spec = pl.BlockSpec((1, gw), lambda b, gi, t: (0, gi))
    lora_w_spec = pl.BlockSpec((LORA_PAD, gw), lambda b, gi, t: (0, gi))
    return pl.pallas_call(
        _rwkv_kernel,
        grid=(bsz, ns, nt),
        in_specs=[
            pl.BlockSpec((tt, gw), lambda b, gi, t: (row(b, gi, t), gi)),
            pl.BlockSpec((tt, gw), lambda b, gi, t: (row(b, gi, t), ns + gi)),
            pl.BlockSpec((tt, gw), lambda b, gi, t: (row(b, gi, t), 2 * ns + gi)),
            pl.BlockSpec((tt, 2 * LORA_PAD), lambda b, gi, t: (row(b, gi, t), 0)),
            pl.BlockSpec((tt, gw), lambda b, gi, t: (row(b, gi, t), gi)),
        ] + [vec_spec] * len(vecs) + [lora_w_spec, lora_w_spec],
        out_specs=pl.BlockSpec((tt, gw), lambda b, gi, t: (row(b, gi, t), gi)),
        out_shape=jax.ShapeDtypeStruct((bsz * seq, B_WIDTH), BF16),
        scratch_shapes=[pltpu.VMEM((WKV_GROUPS_PER_STEP, g, g), F32)],
        compiler_params=_params(3),
        name="rwkv7_time_mix",
    )(p_b, p_b, p_b, p_lora, p_zb, *vecs, wd2, wa2)


def _merge_kernel(ya_ref, yb_ref, yc_ref, wa_ref, wb_ref, wc_ref, ga_ref, gb_ref, gc_ref, o_ref):
    def branch(y_ref, w_ref, g_ref):
        return _sigmoid(g_ref[...].astype(F32)) * jnp.dot(y_ref[...], w_ref[...], preferred_element_type=F32)

    o_ref[...] = (branch(ya_ref, wa_ref, ga_ref) + branch(yb_ref, wb_ref, gb_ref) + branch(yc_ref, wc_ref, gc_ref)).astype(o_ref.dtype)


def _merge(ya, yb, yc, wa, wb, wc, p_c, gate_col0, d_model, tm, tn):
    m = ya.shape[0]
    nj = d_model // tn
    j0 = gate_col0 // tn
    y_spec = lambda width: pl.BlockSpec((tm, width), lambda j, i: (i, 0))
    w_spec = lambda width: pl.BlockSpec((width, tn), lambda j, i: (0, j))
    g_spec = lambda br: pl.BlockSpec((tm, tn), lambda j, i: (i, j0 + br * nj + j))
    return pl.pallas_call(
        _merge_kernel,
        grid=(nj, m // tm),
        in_specs=[y_spec(A_WIDTH), y_spec(B_WIDTH), y_spec(C_WIDTH), w_spec(A_WIDTH), w_spec(B_WIDTH), w_spec(C_WIDTH),
                  g_spec(0), g_spec(1), g_spec(2)],
        out_specs=pl.BlockSpec((tm, tn), lambda j, i: (i, j)),
        out_shape=jax.ShapeDtypeStruct((m, d_model), BF16),
        compiler_params=_params(2, VMEM_LIMIT_PROJ),
        name="gated_merge",
    )(ya, yb, yc, wa, wb, wc, p_c, p_c, p_c)


def _out_kernel(m_ref, w_ref, x_ref, g_ref, o_ref, *, final_norm):
    y = x_ref[...] + jnp.dot(m_ref[...], w_ref[...], preferred_element_type=F32)
    if final_norm:
        y = y * lax.rsqrt(jnp.mean(y * y, axis=-1, keepdims=True) + RMS_EPS) * g_ref[...]
    o_ref[...] = y


def _out_proj(merged, w_out, x2d, g, final_norm, tm):
    m, d = x2d.shape
    return pl.pallas_call(
        functools.partial(_out_kernel, final_norm=final_norm),
        grid=(m // tm,),
        in_specs=[pl.BlockSpec((tm, d), lambda i: (i, 0)), pl.BlockSpec((d, d), lambda i: (0, 0)),
                  pl.BlockSpec((tm, d), lambda i: (i, 0)), pl.BlockSpec((1, d), lambda i: (0, 0))],
        out_specs=pl.BlockSpec((tm, d), lambda i: (i, 0)),
        out_shape=jax.ShapeDtypeStruct((m, d), F32),
        compiler_params=_params(1),
        name="out_proj",
    )(merged, w_out, x2d, g.reshape(1, d))


def _pick_tile(n, candidates):
    for t in candidates:
        if n % t == 0:
            return t
    raise ValueError(f"no tile for extent {n}")


class _Tiles(NamedTuple):
    norm_rows: int
    proj_rows: int
    proj_cols: int
    proj_cols_b: int
    mem_rows: int
    cross_rows: int
    wkv_rows: int
    merge_rows: int
    merge_cols: int
    out_rows: int


def _tiles(m, seq, mem_rows):
    return _Tiles(
        norm_rows=_pick_tile(m, (1024, 512, 256)),
        proj_rows=_pick_tile(seq, (2048, 1024, 512, 256)),
        proj_cols=1024,
        proj_cols_b=768,
        mem_rows=_pick_tile(mem_rows, (1024, 512, 256)),
        cross_rows=_pick_tile(seq, (512, 256)),
        wkv_rows=_pick_tile(seq, (512, 256, 128, 64)),
        merge_rows=_pick_tile(m, (1024, 512, 256)),
        merge_cols=1024,
        out_rows=_pick_tile(m, (512, 256)),
    )


def _pad_cols(w, width):
    return jnp.pad(w, ((0, 0), (0, width - w.shape[1])))


def _layer(x2d, mem2d, bsz, seq, mem_len, bias_tiles, rel_bias, norm_g, mem_norm_g, w_in, rw, w_mem_kv, w_proj_a, w_proj_b,
           w_proj_c, w_out, final_g):
    m, d = x2d.shape
    t = _tiles(m, seq, mem2d.shape[0])

    c_z_a = 3 * A_WIDTH
    c_rkv_b = c_z_a + A_WIDTH
    c_z_b = c_rkv_b + 3 * B_WIDTH
    c_lw = c_z_b + B_WIDTH
    c_la = c_lw + LORA
    c_q_c = c_la + LORA

    h = _rmsnorm(x2d, norm_g, BF16, t.norm_rows)
    ones = lambda n: jnp.ones((1, n), F32)
    wt = w_in.T
    scale_a = jnp.concatenate([jnp.full((1, A_WIDTH), MOBA_Q_SCALE, F32), ones(3 * A_WIDTH)], axis=1)
    p_a = _matmul_wt(h, wt, 0, 4 * A_WIDTH, scale_a, BF16, t.proj_rows, t.proj_cols, "in_proj_a")
    (mu_r, mu_k, mu_v, mu_w, mu_a, w0, w_decay2, a0, w_aaa2, k_k, k_a, r_k, lnx_w, lnx_b) = rw
    vec = lambda v: v.reshape(1, -1)
    p_b = _matmul_wt(h, wt, c_rkv_b, 3 * B_WIDTH, jnp.concatenate([vec(mu_r), vec(mu_k), vec(mu_v)], axis=1), F32,
                     t.proj_rows, t.proj_cols_b, "in_proj_b", seq_rows=seq)
    p_zb = _matmul_wt(h, wt, c_z_b, B_WIDTH, ones(B_WIDTH), BF16, t.proj_rows, t.proj_cols_b, "in_proj_zb")
    pad_rows = lambda w: jnp.pad(w, ((0, LORA_PAD - LORA), (0, 0)))
    wt_lora = jnp.concatenate([pad_rows(wt[c_lw:c_la]), pad_rows(wt[c_la:c_q_c])], axis=0)
    mu_wa = jnp.concatenate([_pad_cols(vec(mu_w), LORA_PAD), _pad_cols(vec(mu_a), LORA_PAD)], axis=1)
    p_lora = _matmul_wt(h, wt_lora, 0, 2 * LORA_PAD, mu_wa, F32, t.proj_rows, 2 * LORA_PAD, "in_proj_lora", seq_rows=seq)
    n_c = 2 * C_WIDTH + N_BRANCHES * d
    p_c = _matmul_wt(h, wt, c_q_c, n_c, ones(n_c), BF16, t.proj_rows, t.proj_cols, "in_proj_c")

    mem_n = _rmsnorm(mem2d, mem_norm_g, BF16, t.mem_rows)
    kv = _matmul(mem_n, w_mem_kv.astype(BF16), BF16, t.mem_rows, t.proj_cols, "mem_kv")

    ya = _moba(p_a, rel_bias, bias_tiles, bsz, seq)
    yc = _cross(p_c, kv, bsz, seq, mem_len, t.cross_rows)

    vecs = (vec(w0), vec(a0), vec(k_k), vec(k_a), vec(r_k), vec(lnx_w), vec(lnx_b))
    yb = _rwkv(p_b, p_lora, p_zb, vecs, pad_rows(w_decay2), pad_rows(w_aaa2), bsz, seq, t.wkv_rows)

    merged = _merge(ya, yb, yc, w_proj_a.astype(BF16), w_proj_b.astype(BF16), w_proj_c.astype(BF16), p_c, 2 * C_WIDTH, d,
                    t.merge_rows, t.merge_cols)
    g = final_g if final_g is not None else jnp.ones((d,), F32)
    return _out_proj(merged, w_out.astype(BF16), x2d, g, final_g is not None, t.out_rows)


def kernel(x, mem, rel_bias, norm_g, mem_norm_g, w_in, rw_mu_r, rw_mu_k, rw_mu_v, rw_mu_w, rw_mu_a, rw_w0, rw_w_decay2, rw_a0, rw_w_aaa2, rw_k_k, rw_k_a, rw_r_k, rw_lnx_w, rw_lnx_b, w_mem_kv, w_proj_a, w_proj_b, w_proj_c, w_out, final_norm_g):
    bsz, seq, d = x.shape
    mem_len = mem.shape[1]
    depth = norm_g.shape[0]
    x2d = x.reshape(bsz * seq, d)
    mem2d = mem.reshape(bsz * mem_len, d)
    bias_tiles = _bias_tiles(rel_bias)
    for l in range(depth):
        rw = (rw_mu_r[l], rw_mu_k[l], rw_mu_v[l], rw_mu_w[l], rw_mu_a[l], rw_w0[l], rw_w_decay2[l], rw_a0[l], rw_w_aaa2[l],
              rw_k_k[l], rw_k_a[l], rw_r_k[l], rw_lnx_w[l], rw_lnx_b[l])
        x2d = _layer(x2d, mem2d, bsz, seq, mem_len, bias_tiles, rel_bias, norm_g[l], mem_norm_g[l], w_in[l], rw, w_mem_kv[l],
                     w_proj_a[l], w_proj_b[l], w_proj_c[l], w_out[l], final_norm_g if l == depth - 1 else None)
    return x2d.reshape(bsz, seq, d)
```

```python
import functools
import math
from typing import NamedTuple

import jax
import jax.numpy as jnp
from jax import lax
from jax.experimental import pallas as pl
from jax.experimental.pallas import tpu as pltpu

F32 = jnp.float32
BF16 = jnp.bfloat16
LANES = 128
BF16_SUBLANES = 16

RMS_EPS = 1e-6

A_HEADS = 12
A_HEAD_DIM = 128
A_WIDTH = A_HEADS * A_HEAD_DIM
MOBA_BLOCK = 256
MOBA_TOPK = 3
MOBA_HEADS_PER_STEP = 6
MOBA_FAR_GROUP = 4
MOBA_Q_SCALE = A_HEAD_DIM ** -0.5 * math.log2(math.e)
REL_BUCKETS = 32
REL_MAX_DIST = 128

B_HEAD_DIM = 64
B_WIDTH = 1536
LORA = 96
LORA_PAD = LANES
LNX_EPS = 64e-5
WKV_CHUNK = 64
WKV_GROUP = 256
WKV_HEADS_PER_GROUP = WKV_GROUP // B_HEAD_DIM
WKV_GROUPS_PER_STEP = 6
WKV_CHUNKS_PER_STEP = 4

C_HEADS = 4
C_HEAD_DIM = 256
C_WIDTH = C_HEADS * C_HEAD_DIM

N_BRANCHES = 3

VMEM_LIMIT = 48 * 1024 * 1024
VMEM_LIMIT_PROJ = 58 * 1024 * 1024

NN = (((1,), (0,)), ((), ()))
NT = (((1,), (1,)), ((), ()))
TN = (((0,), (0,)), ((), ()))

MASKED = -1e30
LOG2E = math.log2(math.e)


def _params(n_axes, vmem_limit=VMEM_LIMIT):
    return pltpu.CompilerParams(dimension_semantics=("arbitrary",) * n_axes, vmem_limit_bytes=vmem_limit)


def _sigmoid(x):
    return 0.5 * jnp.tanh(0.5 * x) + 0.5


def _silu(z):
    return z * _sigmoid(z)


def _bf(x):
    return x.astype(BF16)


def _mm(a, b, dn):
    return lax.dot_general(a, b, dn, preferred_element_type=F32)


def _each(f, *lists):
    return [f(*args) for args in zip(*lists)]


def _rmsnorm_kernel(x_ref, g_ref, o_ref):
    x = x_ref[...].astype(F32)
    y = x * lax.rsqrt(jnp.mean(x * x, axis=-1, keepdims=True) + RMS_EPS)
    o_ref[...] = (y * g_ref[...]).astype(o_ref.dtype)


def _rmsnorm(x2d, g, out_dtype, tm):
    m, d = x2d.shape
    return pl.pallas_call(
        _rmsnorm_kernel,
        grid=(m // tm,),
        in_specs=[pl.BlockSpec((tm, d), lambda i: (i, 0)), pl.BlockSpec((1, d), lambda i: (0, 0))],
        out_specs=pl.BlockSpec((tm, d), lambda i: (i, 0)),
        out_shape=jax.ShapeDtypeStruct((m, d), out_dtype),
        compiler_params=_params(1),
        name="rmsnorm",
    )(x2d, g.reshape(1, d))


def _matmul_kernel(a_ref, w_ref, o_ref):
    o_ref[...] = jnp.dot(a_ref[...], w_ref[...], preferred_element_type=F32).astype(o_ref.dtype)


def _matmul(a, w, out_dtype, tm, tn, name):
    m, k = a.shape
    n = w.shape[1]
    return pl.pallas_call(
        _matmul_kernel,
        grid=(n // tn, m // tm),
        in_specs=[pl.BlockSpec((tm, k), lambda j, i: (i, 0)), pl.BlockSpec((k, tn), lambda j, i: (0, j))],
        out_specs=pl.BlockSpec((tm, tn), lambda j, i: (i, j)),
        out_shape=jax.ShapeDtypeStruct((m, n), out_dtype),
        compiler_params=_params(2),
        name=name,
    )(a, w)


def _matmul_wt_kernel(a_ref, wt_ref, s_ref, o_ref, wb_ref, *carry_ref, seq_rows):
    i = pl.program_id(1)

    @pl.when(i == 0)
    def _():
        wb_ref[...] = wt_ref[...].astype(BF16)
        for ref in carry_ref:
            ref[...] = jnp.zeros_like(ref)

    x = lax.dot_general(a_ref[...], wb_ref[...], NT, preferred_element_type=F32)
    if seq_rows is None:
        o_ref[...] = (x * s_ref[...]).astype(o_ref.dtype)
    else:
        tm = x.shape[0]
        last, = carry_ref
        first_row = lax.broadcasted_iota(jnp.int32, x.shape, 0) == 0
        carried = jnp.where((i * tm) % seq_rows == 0, 0.0, last[0:1, :])
        prev = jnp.where(first_row, carried, pltpu.roll(x, 1, 0))
        last[0:1, :] = x[tm - 1:tm, :]
        o_ref[...] = (x + (prev - x) * s_ref[...]).astype(o_ref.dtype)


def _matmul_wt(a, wt, row0, n, col_vec, out_dtype, tm, tn, name, seq_rows=None):
    m, k = a.shape
    scratch = [pltpu.VMEM((tn, k), BF16)]
    if seq_rows is not None:
        assert seq_rows % tm == 0
        scratch.append(pltpu.VMEM((8, tn), F32))
    return pl.pallas_call(
        functools.partial(_matmul_wt_kernel, seq_rows=seq_rows),
        grid=(n // tn, m // tm),
        in_specs=[pl.BlockSpec((tm, k), lambda j, i: (i, 0)),
                  pl.BlockSpec((pl.Element(tn), pl.Element(k)), lambda j, i: (pl.multiple_of(row0 + j * tn, 8), 0)),
                  pl.BlockSpec((1, tn), lambda j, i: (0, j))],
        out_specs=pl.BlockSpec((tm, tn), lambda j, i: (i, j)),
        out_shape=jax.ShapeDtypeStruct((m, n), out_dtype),
        scratch_shapes=scratch,
        compiler_params=_params(2, VMEM_LIMIT_PROJ),
        name=name,
    )(a, wt, col_vec)


def _t5_bucket(dist):
    n = jnp.maximum(dist, 0)
    max_exact = REL_BUCKETS // 2
    nf = jnp.maximum(n, max_exact).astype(F32)
    large = max_exact + (jnp.log(nf / max_exact) / math.log(REL_MAX_DIST / max_exact) * (REL_BUCKETS - max_exact)).astype(jnp.int32)
    large = jnp.minimum(large, REL_BUCKETS - 1)
    return jnp.where(n < max_exact, n, large)


def _bias_kernel(rel_ref, bucket_ref, o_ref):
    h = pl.program_id(0)
    bucket = bucket_ref[...]
    acc = jnp.zeros(bucket.shape, F32)
    for b in range(REL_BUCKETS):
        acc = jnp.where(bucket == b, rel_ref[b, h], acc)
    qpos = lax.broadcasted_iota(jnp.int32, bucket.shape, 0) + MOBA_BLOCK
    kpos = lax.broadcasted_iota(jnp.int32, bucket.shape, 1)
    o_ref[0] = jnp.where(kpos <= qpos, acc * LOG2E, MASKED)


def _bias_tiles(rel_bias):
    blk = MOBA_BLOCK
    qpos = lax.broadcasted_iota(jnp.int32, (blk, 2 * blk), 0) + blk
    kpos = lax.broadcasted_iota(jnp.int32, (blk, 2 * blk), 1)
    bucket = _t5_bucket(qpos - kpos)
    return pl.pallas_call(
        _bias_kernel,
        grid=(A_HEADS,),
        in_specs=[pl.BlockSpec(memory_space=pltpu.SMEM), pl.BlockSpec((blk, 2 * blk), lambda h: (0, 0))],
        out_specs=pl.BlockSpec((1, blk, 2 * blk), lambda h: (h, 0, 0)),
        out_shape=jax.ShapeDtypeStruct((A_HEADS, blk, 2 * blk), F32),
        compiler_params=_params(1),
        name="t5_bias",
    )(rel_bias, bucket)


def _moba_kernel(rel_ref, q_ref, k_ref, v_ref, z_ref, bias_ref, o_ref, kmean_ref, m_ref, acc_ref):
    hb = pl.program_id(1)
    qi = pl.program_id(2)
    blk, hd = MOBA_BLOCK, A_HEAD_DIM
    n_heads = q_ref.shape[1] // hd
    nb = k_ref.shape[0] // blk
    nbp = kmean_ref.shape[1]

    @pl.when(qi == 0)
    def _():
        kmean_ref[...] = jnp.zeros_like(kmean_ref)
        for j in range(nb):
            mean_j = jnp.sum(k_ref[j * blk:(j + 1) * blk, :].astype(F32), axis=0, keepdims=True) * (1.0 / blk)
            for hh in range(n_heads):
                kmean_ref[hh, j:j + 1, :] = mean_j[:, hh * hd:(hh + 1) * hd]

    n_far = min(MOBA_FAR_GROUP, nb)
    never = LANES - 1
    blk_id = lax.broadcasted_iota(jnp.int32, (nbp, blk), 0)
    blk_f = blk_id.astype(F32)
    eye = jnp.where(lax.broadcasted_iota(jnp.int32, (nbp, LANES), 0) == lax.broadcasted_iota(jnp.int32, (nbp, LANES), 1), 1.0, 0.0).astype(BF16)
    j_prev = jnp.maximum(qi - 1, 0)
    own = pl.multiple_of(qi * blk, blk)
    prev = pl.multiple_of(j_prev * blk, blk)

    heads = [slice(hh * hd, (hh + 1) * hd) for hh in range(n_heads)]
    q = [q_ref[:, hl] for hl in heads]

    def gate_of(qh, hh):
        km = kmean_ref[hh]
        km_hi = km.astype(BF16)
        km_lo = (km - km_hi.astype(F32)).astype(BF16)
        return jnp.where(blk_id < qi, _mm(km_hi, qh, NT) + _mm(km_lo, qh, NT), -jnp.inf)

    g = _each(gate_of, q, range(n_heads))
    sel_t = [jnp.zeros((nbp, blk), F32) for _ in heads]
    for _ in range(MOBA_TOPK):
        gmax = _each(lambda x: jnp.max(x, axis=0, keepdims=True), g)
        first = _each(lambda x, mx: jnp.min(jnp.where(x == mx, blk_f, float(nbp)), axis=0, keepdims=True), g, gmax)
        pick = _each(lambda f, mx: (blk_f == f) & (mx > -jnp.inf), first, gmax)
        sel_t = _each(lambda p, s: jnp.where(p, 1.0, s), pick, sel_t)
        g = _each(lambda p, x: jnp.where(p, -jnp.inf, x), pick, g)
    sel = _each(lambda s: _mm(s.astype(BF16), eye, TN), sel_t)
    qm = _each(lambda qh, s: jnp.concatenate([qh, jnp.where(s > 0.0, 0.0, MASKED).astype(BF16)], axis=1), q, sel)

    def masked_logits(keys, key_blk):
        col = jnp.where(lax.broadcasted_iota(jnp.int32, key_blk.shape, 1) == key_blk, 1.0, 0.0).astype(BF16)
        return _each(lambda x, kk: _mm(x, jnp.concatenate([kk, col], axis=1), NT), qm, keys)

    def pv(p, values):
        ones = jnp.ones(values[0].shape, BF16)
        return _each(lambda x, v: jnp.dot(x.astype(BF16), jnp.concatenate([v, ones], axis=1), preferred_element_type=F32), p, values)

    cat0 = lambda ref, hl: jnp.concatenate([ref[pl.ds(prev, blk), hl], ref[pl.ds(own, blk), hl]], axis=0)
    key_row = lax.broadcasted_iota(jnp.int32, (2 * blk, LANES), 0)
    raw = masked_logits([cat0(k_ref, hl) for hl in heads], jnp.where(key_row < blk, j_prev, -1))
    t = _each(lambda x, hh: x + bias_ref[hh], raw, range(n_heads))
    m = _each(lambda x: jnp.max(x, axis=-1, keepdims=True), t)
    p = _each(lambda x, mx: jnp.exp2(x - mx), t, m)
    acc = pv(p, [cat0(v_ref, hl) for hl in heads])
    for hh in range(n_heads):
        m_ref[hh] = m[hh]
        acc_ref[hh] = acc[hh]

    bias_far = [rel_ref[REL_BUCKETS - 1, hb * n_heads + hh] * LOG2E for hh in range(n_heads)]
    far_row_blk = lax.broadcasted_iota(jnp.int32, (n_far * blk, LANES), 0) // blk

    def body(gi, carry):
        rows = pl.ds(pl.multiple_of(gi * (n_far * blk), n_far * blk), n_far * blk)
        key_blk = gi * n_far + far_row_blk
        raw = masked_logits([k_ref[rows, hl] for hl in heads], jnp.where(key_blk < j_prev, key_blk, never))
        m = [m_ref[hh] for hh in range(n_heads)]
        m_new = _each(lambda mx, x, b: jnp.maximum(mx, jnp.max(x, axis=-1, keepdims=True) + b), m, raw, bias_far)
        p = _each(lambda x, mn, b: jnp.exp2(x + (b - mn)), raw, m_new, bias_far)
        alpha = _each(lambda mx, mn: jnp.exp2(mx - mn), m, m_new)
        upd = pv(p, [v_ref[rows, hl] for hl in heads])
        for hh in range(n_heads):
            m_ref[hh] = m_new[hh]
            acc_ref[hh] = alpha[hh] * acc_ref[hh] + upd[hh]
        return carry

    lax.fori_loop(0, (j_prev + n_far - 1) // n_far, body, 0)
    for hh, hl in enumerate(heads):
        y = acc_ref[hh, :, :hd] / acc_ref[hh, :, hd:]
        o_ref[:, hl] = (y * _silu(z_ref[:, hl].astype(F32))).astype(o_ref.dtype)


def _moba(p_a, rel_bias, bias_tiles, bsz, seq):
    blk = MOBA_BLOCK
    nq = seq // blk
    hw = MOBA_HEADS_PER_STEP * A_HEAD_DIM
    ns = A_WIDTH // hw
    nbp = -(-nq // BF16_SUBLANES) * BF16_SUBLANES
    assert nbp < LANES
    return pl.pallas_call(
        _moba_kernel,
        grid=(bsz, ns, nq),
        in_specs=[
            pl.BlockSpec(memory_space=pltpu.SMEM),
            pl.BlockSpec((blk, hw), lambda b, h, i: (b * nq + i, h)),
            pl.BlockSpec((seq, hw), lambda b, h, i: (b, ns + h)),
            pl.BlockSpec((seq, hw), lambda b, h, i: (b, 2 * ns + h)),
            pl.BlockSpec((blk, hw), lambda b, h, i: (b * nq + i, 3 * ns + h)),
            pl.BlockSpec((MOBA_HEADS_PER_STEP, blk, 2 * blk), lambda b, h, i: (h, 0, 0)),
        ],
        out_specs=pl.BlockSpec((blk, hw), lambda b, h, i: (b * nq + i, h)),
        out_shape=jax.ShapeDtypeStruct((bsz * seq, A_WIDTH), BF16),
        scratch_shapes=[pltpu.VMEM((MOBA_HEADS_PER_STEP, nbp, A_HEAD_DIM), F32), pltpu.VMEM((MOBA_HEADS_PER_STEP, blk, 1), F32),
                        pltpu.VMEM((MOBA_HEADS_PER_STEP, blk, 2 * A_HEAD_DIM), F32)],
        compiler_params=_params(3),
        name="moba_attention",
    )(rel_bias, p_a, p_a, p_a, p_a, bias_tiles)


def _cross_kernel(q_ref, k_ref, v_ref, z_ref, o_ref):
    hd = C_HEAD_DIM
    heads = [slice(h * hd, (h + 1) * hd) for h in range(C_HEADS)]
    s = _each(lambda hl: _mm(q_ref[:, hl], k_ref[:, hl], NT) * (hd ** -0.5), heads)
    m = _each(lambda x: jnp.max(x, axis=-1, keepdims=True), s)
    p = _each(lambda x, mx: jnp.exp(x - mx), s, m)
    l = _each(lambda x: jnp.sum(x, axis=-1, keepdims=True), p)
    y = _each(lambda x, hl: jnp.dot(x.astype(BF16), v_ref[:, hl], preferred_element_type=F32), p, heads)
    for hl, yh, lh in zip(heads, y, l):
        o_ref[:, hl] = (yh / lh * _silu(z_ref[:, hl].astype(F32))).astype(o_ref.dtype)


def _cross(p_c, kv, bsz, seq, mem_len, tq):
    nt = seq // tq
    return pl.pallas_call(
        _cross_kernel,
        grid=(bsz, nt),
        in_specs=[
            pl.BlockSpec((tq, C_WIDTH), lambda b, i: (b * nt + i, 0)),
            pl.BlockSpec((mem_len, C_WIDTH), lambda b, i: (b, 0)),
            pl.BlockSpec((mem_len, C_WIDTH), lambda b, i: (b, 1)),
            pl.BlockSpec((tq, C_WIDTH), lambda b, i: (b * nt + i, 1)),
        ],
        out_specs=pl.BlockSpec((tq, C_WIDTH), lambda b, i: (b * nt + i, 0)),
        out_shape=jax.ShapeDtypeStruct((bsz * seq, C_WIDTH), BF16),
        compiler_params=_params(2),
        name="memory_attention",
    )(p_c, kv, kv, p_c)


class _WkvConsts:
    def __init__(self):
        c, g, hd = WKV_CHUNK, WKV_GROUP, B_HEAD_DIM
        row = lax.broadcasted_iota(jnp.int32, (g, g), 0)
        col = lax.broadcasted_iota(jnp.int32, (g, g), 1)
        self.same_head = jnp.where((row // hd) == (col // hd), 1.0, 0.0)
        self.ones_bd = self.same_head.astype(BF16)
        t = lax.broadcasted_iota(jnp.int32, (c, g), 0)
        s = lax.broadcasted_iota(jnp.int32, (c, g), 1) % hd
        self.strict = s < t
        self.incl = s <= t
        self.strict_incl = jnp.concatenate([self.strict, self.incl], axis=0)
        self.eye = jnp.where(s == t, 1.0, 0.0).astype(F32)
        tr = lax.broadcasted_iota(jnp.int32, (c, c), 0)
        tc = lax.broadcasted_iota(jnp.int32, (c, c), 1)
        lower = jnp.where(tc <= tr, 1.0, 0.0).astype(BF16)
        self.lower2 = jnp.concatenate([lower, lower], axis=1)

    def bd(self, p):
        return jnp.concatenate([p] * WKV_HEADS_PER_GROUP, axis=0) * self.ones_bd

    def segsum(self, x):
        return jnp.dot(_bf(x), self.ones_bd, preferred_element_type=F32)


def _wkv_prepare(rm, kmod, vm, avec, bvec, logdec, k):
    c = WKV_CHUNK
    cat = lambda u, w: jnp.concatenate([u, w], axis=0)
    ld_hi = _each(_bf, logdec)
    ld_lo = _each(lambda x, hi: _bf(x - hi.astype(F32)), logdec, ld_hi)
    cl = _each(lambda hi, lo: jnp.dot(k.lower2, cat(hi, lo), preferred_element_type=F32), ld_hi, ld_lo)
    cl_last = _each(lambda x: x[c - 1:c, :], cl)
    rt = _each(lambda r, x: _bf(r * jnp.exp(x)), rm, cl)
    at = _each(lambda a, x, ld: _bf(a * jnp.exp(x - ld)), avec, cl, logdec)
    e_neg = _each(lambda x: jnp.exp(-x), cl)
    bt = _each(lambda b, e: k.bd(_bf(b * e)), bvec, e_neg)
    kt = _each(lambda kk, e: k.bd(_bf(kk * e)), kmod, e_neg)
    decay = _each(jnp.exp, cl_last)
    e_rem = _each(lambda d, e: d * e, decay, e_neg)
    bkh = _each(lambda b, kk, e: cat(_bf(b * e), _bf(kk * e)), bvec, kmod, e_rem)
    vb = _each(_bf, vm)

    lhs = _each(cat, at, rt)
    ab = _each(lambda x, y: _mm(x, y, NT), lhs, bt)
    ak = _each(lambda x, y: _mm(x, y, NT), lhs, kt)
    a_ab = _each(lambda x: jnp.where(k.strict, x[:c], 0.0), ab)
    a_rb = _each(lambda x: _bf(jnp.where(k.incl, x[c:], 0.0)), ab)
    a_k = _each(lambda x: _bf(jnp.where(k.strict_incl, x, 0.0)), ak)

    inv = _each(lambda x: k.eye + x, a_ab)
    pw = _each(_bf, a_ab)
    pw = _each(lambda p: _bf(_mm(p, k.bd(p), NN)), pw)
    for _ in range(int(math.log2(c)) - 2):
        tp = _each(lambda i, p: _mm(cat(_bf(i), p), k.bd(p), NN), inv, pw)
        inv = _each(lambda i, t: i + t[:c], inv, tp)
        pw = _each(lambda t: _bf(t[c:]), tp)
    inv = _each(lambda i, p: _bf(i + _mm(_bf(i), k.bd(p), NN)), inv, pw)

    av = _each(lambda x, v: _mm(x, k.bd(v), NN), a_k, vb)
    w = _each(lambda i, a: _mm(i, k.bd(a), NN), inv, at)
    u0 = _each(lambda i, x: _mm(i, k.bd(_bf(x[:c])), NN), inv, av)
    wr = _each(lambda ww, r: cat(_bf(ww), r), w, rt)
    y0 = _each(lambda x: x[c:], av)
    return list(zip(wr, u0, y0, a_rb, vb, bkh, decay))


def _wkv_apply(prep, state, k):
    c = WKV_CHUNK
    wr, u0, y0, a_rb, vb, bkh, decay = (list(x) for x in zip(*prep))
    uy = _each(lambda x, s: _mm(x, _bf(s), NT), wr, state)
    ub = _each(lambda x, y: _bf(x[:c] + y), uy, u0)
    y = _each(lambda x, a, uu, z: x[c:] + _mm(a, k.bd(uu), NN) + z, uy, a_rb, ub, y0)
    upd = _each(lambda uu, v, bk: _mm(jnp.concatenate([uu, v], axis=0), bk, TN), ub, vb, bkh)
    new_state = _each(lambda s, dc, d: s * dc + d * k.same_head, state, decay, upd)
    return y, new_state


def _rwkv_kernel(r_ref, k_ref, v_ref, lwla_ref, z_ref, w0_ref, a0_ref, kk_ref, ka_ref, rk_ref, lnw_ref, lnb_ref,
                 wd_ref, wa_ref, o_ref, state_ref):
    c, g = WKV_CHUNK, WKV_GROUP
    n_groups = r_ref.shape[1] // g
    n_chunks = WKV_CHUNKS_PER_STEP

    @pl.when(pl.program_id(2) == 0)
    def _():
        state_ref[...] = jnp.zeros_like(state_ref)

    k = _WkvConsts()
    wd = _bf(wd_ref[...])
    wa = _bf(wa_ref[...])
    inv_hd = 1.0 / B_HEAD_DIM
    groups = [slice(gi * g, (gi + 1) * g) for gi in range(n_groups)]
    split = lambda x: [x[ch * c:(ch + 1) * c, ln] for ch in range(n_chunks) for ln in groups]

    def segsums(xs):
        s = k.segsum(jnp.concatenate(xs, axis=0))
        return [s[i * c:(i + 1) * c] for i in range(len(xs))]

    def step(ci, carry):
        rows = pl.ds(pl.multiple_of(ci * (n_chunks * c), n_chunks * c), n_chunks * c)
        rm, km, vm = r_ref[rows, :], k_ref[rows, :], v_ref[rows, :]
        lwla = lwla_ref[rows, :]
        lw = _bf(jnp.tanh(lwla[:, :LORA_PAD]))
        la = _bf(lwla[:, LORA_PAD:])
        logdec = -math.exp(-0.5) * _sigmoid(w0_ref[...] + jnp.dot(lw, wd, preferred_element_type=F32))
        a_lr = _sigmoid(a0_ref[...] + jnp.dot(la, wa, preferred_element_type=F32))
        kmod = km * (1.0 + (a_lr - 1.0) * ka_ref[...])
        kk = split(km * kk_ref[...])
        n = len(kk)
        sums = segsums(_each(lambda x: x * x, kk) + split(rm * kmod * rk_ref[...]))
        kk = _each(lambda x, ss: x * jnp.minimum(lax.rsqrt(ss), 1e12), kk, sums[:n])
        bonus = _each(lambda s, v: s * v, sums[n:], split(vm))
        prep = _wkv_prepare(split(rm), split(kmod), split(vm), _each(lambda x: -x, kk),
                            _each(lambda x, a: x * a, kk, split(a_lr)), split(logdec), k)
        state = [state_ref[gi] for gi in range(n_groups)]
        y = []
        for ch in range(n_chunks):
            y_ch, state = _wkv_apply(prep[ch * n_groups:(ch + 1) * n_groups], state, k)
            y += y_ch
        for gi in range(n_groups):
            state_ref[gi] = state[gi]
        yc = _each(lambda x, s: x - s * inv_hd, y, segsums(y))
        var = _each(lambda s: s * inv_hd, segsums(_each(lambda x: x * x, yc)))
        gate = _silu(z_ref[rows, :].astype(F32))
        for ch in range(n_chunks):
            out_rows = pl.ds(pl.multiple_of(ci * (n_chunks * c) + ch * c, c), c)
            for gi, ln in enumerate(groups):
                i = ch * n_groups + gi
                yn = yc[i] * lax.rsqrt(var[i] + LNX_EPS) * lnw_ref[:, ln] + lnb_ref[:, ln]
                o_ref[out_rows, ln] = ((yn + bonus[i]) * gate[ch * c:(ch + 1) * c, ln]).astype(o_ref.dtype)
        return carry

    lax.fori_loop(0, r_ref.shape[0] // (n_chunks * c), step, 0)


def _rwkv(p_b, p_lora, p_zb, vecs, wd2, wa2, bsz, seq, tt):
    g = WKV_GROUP
    gw = WKV_GROUPS_PER_STEP * g
    ns = B_WIDTH // gw
    nt = seq // tt
    row = lambda b, gi, t: b * nt + t
    vec_spec = pl.BlockSpec((1, gw), lambda b, gi, t: (0, gi))
    lora_w_spec = pl.BlockSpec((LORA_PAD, gw), lambda b, gi, t: (0, gi))
    return pl.pallas_call(
        _rwkv_kernel,
        grid=(bsz, ns, nt),
        in_specs=[
            pl.BlockSpec((tt, gw), lambda b, gi, t: (row(b, gi, t), gi)),
            pl.BlockSpec((tt, gw), lambda b, gi, t: (row(b, gi, t), ns + gi)),
            pl.BlockSpec((tt, gw), lambda b, gi, t: (row(b, gi, t), 2 * ns + gi)),
            pl.BlockSpec((tt, 2 * LORA_PAD), lambda b, gi, t: (row(b, gi, t), 0)),
            pl.BlockSpec((tt, gw), lambda b, gi, t: (row(b, gi, t), gi)),
        ] + [vec_spec] * len(vecs) + [lora_w_spec, lora_w_spec],
        out_specs=pl.BlockSpec((tt, gw), lambda b, gi, t: (row(b, gi, t), gi)),
        out_shape=jax.ShapeDtypeStruct((bsz * seq, B_WIDTH), BF16),
        scratch_shapes=[pltpu.VMEM((WKV_GROUPS_PER_STEP, g, g), F32)],
        compiler_params=_params(3),
        name="rwkv7_time_mix",
    )(p_b, p_b, p_b, p_lora, p_zb, *vecs, wd2, wa2)


def _merge_kernel(ya_ref, yb_ref, yc_ref, wa_ref, wb_ref, wc_ref, ga_ref, gb_ref, gc_ref, o_ref):
    def branch(y_ref, w_ref, g_ref):
        return _sigmoid(g_ref[...].astype(F32)) * jnp.dot(y_ref[...], w_ref[...], preferred_element_type=F32)

    o_ref[...] = (branch(ya_ref, wa_ref, ga_ref) + branch(yb_ref, wb_ref, gb_ref) + branch(yc_ref, wc_ref, gc_ref)).astype(o_ref.dtype)


def _merge(ya, yb, yc, wa, wb, wc, p_c, gate_col0, d_model, tm, tn):
    m = ya.shape[0]
    nj = d_model // tn
    j0 = gate_col0 // tn
    y_spec = lambda width: pl.BlockSpec((tm, width), lambda j, i: (i, 0))
    w_spec = lambda width: pl.BlockSpec((width, tn), lambda j, i: (0, j))
    g_spec = lambda br: pl.BlockSpec((tm, tn), lambda j, i: (i, j0 + br * nj + j))
    return pl.pallas_call(
        _merge_kernel,
        grid=(nj, m // tm),
        in_specs=[y_spec(A_WIDTH), y_spec(B_WIDTH), y_spec(C_WIDTH), w_spec(A_WIDTH), w_spec(B_WIDTH), w_spec(C_WIDTH),
                  g_spec(0), g_spec(1), g_spec(2)],
        out_specs=pl.BlockSpec((tm, tn), lambda j, i: (i, j)),
        out_shape=jax.ShapeDtypeStruct((m, d_model), BF16),
        compiler_params=_params(2, VMEM_LIMIT_PROJ),
        name="gated_merge",
    )(ya, yb, yc, wa, wb, wc, p_c, p_c, p_c)


def _out_kernel(m_ref, w_ref, x_ref, g_ref, o_ref, *, final_norm):
    y = x_ref[...] + jnp.dot(m_ref[...], w_ref[...], preferred_element_type=F32)
    if final_norm:
        y = y * lax.rsqrt(jnp.mean(y * y, axis=-1, keepdims=True) + RMS_EPS) * g_ref[...]
    o_ref[...] = y


def _out_proj(merged, w_out, x2d, g, final_norm, tm):
    m, d = x2d.shape
    return pl.pallas_call(
        functools.partial(_out_kernel, final_norm=final_norm),
        grid=(m // tm,),
        in_specs=[pl.BlockSpec((tm, d), lambda i: (i, 0)), pl.BlockSpec((d, d), lambda i: (0, 0)),
                  pl.BlockSpec((tm, d), lambda i: (i, 0)), pl.BlockSpec((1, d), lambda i: (0, 0))],
        out_specs=pl.BlockSpec((tm, d), lambda i: (i, 0)),
        out_shape=jax.ShapeDtypeStruct((m, d), F32),
        compiler_params=_params(1),
        name="out_proj",
    )(merged, w_out, x2d, g.reshape(1, d))


def _pick_tile(n, candidates):
    for t in candidates:
        if n % t == 0:
            return t
    raise ValueError(f"no tile for extent {n}")


class _Tiles(NamedTuple):
    norm_rows: int
    proj_rows: int
    proj_cols: int
    proj_cols_b: int
    mem_rows: int
    cross_rows: int
    wkv_rows: int
    merge_rows: int
    merge_cols: int
    out_rows: int


def _tiles(m, seq, mem_rows):
    return _Tiles(
        norm_rows=_pick_tile(m, (1024, 512, 256)),
        proj_rows=_pick_tile(seq, (2048, 1024, 512, 256)),
        proj_cols=1024,
        proj_cols_b=768,
        mem_rows=_pick_tile(mem_rows, (1024, 512, 256)),
        cross_rows=_pick_tile(seq, (1024, 512, 256)),
        wkv_rows=_pick_tile(seq, (512, 256, 128, 64)),
        merge_rows=_pick_tile(m, (1024, 512, 256)),
        merge_cols=1024,
        out_rows=_pick_tile(m, (512, 256)),
    )


def _pad_cols(w, width):
    return jnp.pad(w, ((0, 0), (0, width - w.shape[1])))


def _layer(x2d, mem2d, bsz, seq, mem_len, bias_tiles, rel_bias, norm_g, mem_norm_g, w_in, rw, w_mem_kv, w_proj_a, w_proj_b,
           w_proj_c, w_out, final_g):
    m, d = x2d.shape
    t = _tiles(m, seq, mem2d.shape[0])

    c_z_a = 3 * A_WIDTH
    c_rkv_b = c_z_a + A_WIDTH
    c_z_b = c_rkv_b + 3 * B_WIDTH
    c_lw = c_z_b + B_WIDTH
    c_la = c_lw + LORA
    c_q_c = c_la + LORA

    h = _rmsnorm(x2d, norm_g, BF16, t.norm_rows)
    ones = lambda n: jnp.ones((1, n), F32)
    wt = w_in.T
    scale_a = jnp.concatenate([jnp.full((1, A_WIDTH), MOBA_Q_SCALE, F32), ones(3 * A_WIDTH)], axis=1)
    p_a = _matmul_wt(h, wt, 0, 4 * A_WIDTH, scale_a, BF16, t.proj_rows, t.proj_cols, "in_proj_a")
    (mu_r, mu_k, mu_v, mu_w, mu_a, w0, w_decay2, a0, w_aaa2, k_k, k_a, r_k, lnx_w, lnx_b) = rw
    vec = lambda v: v.reshape(1, -1)
    p_b = _matmul_wt(h, wt, c_rkv_b, 3 * B_WIDTH, jnp.concatenate([vec(mu_r), vec(mu_k), vec(mu_v)], axis=1), F32,
                     t.proj_rows, t.proj_cols_b, "in_proj_b", seq_rows=seq)
    p_zb = _matmul_wt(h, wt, c_z_b, B_WIDTH, ones(B_WIDTH), BF16, t.proj_rows, t.proj_cols_b, "in_proj_zb")
    pad_rows = lambda w: jnp.pad(w, ((0, LORA_PAD - LORA), (0, 0)))
    wt_lora = jnp.concatenate([pad_rows(wt[c_lw:c_la]), pad_rows(wt[c_la:c_q_c])], axis=0)
    mu_wa = jnp.concatenate([_pad_cols(vec(mu_w), LORA_PAD), _pad_cols(vec(mu_a), LORA_PAD)], axis=1)
    p_lora = _matmul_wt(h, wt_lora, 0, 2 * LORA_PAD, mu_wa, F32, t.proj_rows, 2 * LORA_PAD, "in_proj_lora", seq_rows=seq)
    n_c = 2 * C_WIDTH + N_BRANCHES * d
    p_c = _matmul_wt(h, wt, c_q_c, n_c, ones(n_c), BF16, t.proj_rows, t.proj_cols, "in_proj_c")

    mem_n = _rmsnorm(mem2d, mem_norm_g, BF16, t.mem_rows)
    kv = _matmul(mem_n, w_mem_kv.astype(BF16), BF16, t.mem_rows, t.proj_cols, "mem_kv")

    ya = _moba(p_a, rel_bias, bias_tiles, bsz, seq)
    yc = _cross(p_c, kv, bsz, seq, mem_len, t.cross_rows)

    vecs = (vec(w0), vec(a0), vec(k_k), vec(k_a), vec(r_k), vec(lnx_w), vec(lnx_b))
    yb = _rwkv(p_b, p_lora, p_zb, vecs, pad_rows(w_decay2), pad_rows(w_aaa2), bsz, seq, t.wkv_rows)

    merged = _merge(ya, yb, yc, w_proj_a.astype(BF16), w_proj_b.astype(BF16), w_proj_c.astype(BF16), p_c, 2 * C_WIDTH, d,
                    t.merge_rows, t.merge_cols)
    g = final_g if final_g is not None else jnp.ones((d,), F32)
    return _out_proj(merged, w_out.astype(BF16), x2d, g, final_g is not None, t.out_rows)


def kernel(x, mem, rel_bias, norm_g, mem_norm_g, w_in, rw_mu_r, rw_mu_k, rw_mu_v, rw_mu_w, rw_mu_a, rw_w0, rw_w_decay2, rw_a0, rw_w_aaa2, rw_k_k, rw_k_a, rw_r_k, rw_lnx_w, rw_lnx_b, w_mem_kv, w_proj_a, w_proj_b, w_proj_c, w_out, final_norm_g):
    bsz, seq, d = x.shape
    mem_len = mem.shape[1]
    depth = norm_g.shape[0]
    x2d = x.reshape(bsz * seq, d)
    mem2d = mem.reshape(bsz * mem_len, d)
    bias_tiles = _bias_tiles(rel_bias)
    for l in range(depth):
        rw = (rw_mu_r[l], rw_mu_k[l], rw_mu_v[l], rw_mu_w[l], rw_mu_a[l], rw_w0[l], rw_w_decay2[l], rw_a0[l], rw_w_aaa2[l],
              rw_k_k[l], rw_k_a[l], rw_r_k[l], rw_lnx_w[l], rw_lnx_b[l])
        x2d = _layer(x2d, mem2d, bsz, seq, mem_len, bias_tiles, rel_bias, norm_g[l], mem_norm_g[l], w_in[l], rw, w_mem_kv[l],
                     w_proj_a[l], w_proj_b[l], w_proj_c[l], w_out[l], final_norm_g if l == depth - 1 else None)
    return x2d.reshape(bsz, seq, d)
```

```python
import functools
import math
from typing import NamedTuple

import jax
import jax.numpy as jnp
from jax import lax
from jax.experimental import pallas as pl
from jax.experimental.pallas import tpu as pltpu

F32 = jnp.float32
BF16 = jnp.bfloat16
LANES = 128
BF16_SUBLANES = 16

RMS_EPS = 1e-6

A_HEADS = 12
A_HEAD_DIM = 128
A_WIDTH = A_HEADS * A_HEAD_DIM
MOBA_BLOCK = 256
MOBA_TOPK = 3
MOBA_HEADS_PER_STEP = 6
MOBA_FAR_GROUP = 4
MOBA_Q_SCALE = A_HEAD_DIM ** -0.5 * math.log2(math.e)
REL_BUCKETS = 32
REL_MAX_DIST = 128

B_HEAD_DIM = 64
B_WIDTH = 1536
LORA = 96
LORA_PAD = LANES
LNX_EPS = 64e-5
WKV_CHUNK = 64
WKV_GROUP = 256
WKV_HEADS_PER_GROUP = WKV_GROUP // B_HEAD_DIM
WKV_GROUPS_PER_STEP = 6
WKV_CHUNKS_PER_STEP = 4

C_HEADS = 4
C_HEAD_DIM = 256
C_WIDTH = C_HEADS * C_HEAD_DIM

N_BRANCHES = 3

VMEM_LIMIT = 48 * 1024 * 1024
VMEM_LIMIT_PROJ = 58 * 1024 * 1024

NN = (((1,), (0,)), ((), ()))
NT = (((1,), (1,)), ((), ()))
TN = (((0,), (0,)), ((), ()))

MASKED = -1e30
LOG2E = math.log2(math.e)


def _params(n_axes, vmem_limit=VMEM_LIMIT):
    return pltpu.CompilerParams(dimension_semantics=("arbitrary",) * n_axes, vmem_limit_bytes=vmem_limit)


def _sigmoid(x):
    return 0.5 * jnp.tanh(0.5 * x) + 0.5


def _silu(z):
    return z * _sigmoid(z)


def _bf(x):
    return x.astype(BF16)


def _mm(a, b, dn):
    return lax.dot_general(a, b, dn, preferred_element_type=F32)


def _each(f, *lists):
    return [f(*args) for args in zip(*lists)]


def _rmsnorm_kernel(x_ref, g_ref, o_ref):
    x = x_ref[...].astype(F32)
    y = x * lax.rsqrt(jnp.mean(x * x, axis=-1, keepdims=True) + RMS_EPS)
    o_ref[...] = (y * g_ref[...]).astype(o_ref.dtype)


def _rmsnorm(x2d, g, out_dtype, tm):
    m, d = x2d.shape
    return pl.pallas_call(
        _rmsnorm_kernel,
        grid=(m // tm,),
        in_specs=[pl.BlockSpec((tm, d), lambda i: (i, 0)), pl.BlockSpec((1, d), lambda i: (0, 0))],
        out_specs=pl.BlockSpec((tm, d), lambda i: (i, 0)),
        out_shape=jax.ShapeDtypeStruct((m, d), out_dtype),
        compiler_params=_params(1),
        name="rmsnorm",
    )(x2d, g.reshape(1, d))


def _matmul_kernel(a_ref, w_ref, o_ref):
    o_ref[...] = jnp.dot(a_ref[...], w_ref[...], preferred_element_type=F32).astype(o_ref.dtype)


def _matmul(a, w, out_dtype, tm, tn, name):
    m, k = a.shape
    n = w.shape[1]
    return pl.pallas_call(
        _matmul_kernel,
        grid=(n // tn, m // tm),
        in_specs=[pl.BlockSpec((tm, k), lambda j, i: (i, 0)), pl.BlockSpec((k, tn), lambda j, i: (0, j))],
        out_specs=pl.BlockSpec((tm, tn), lambda j, i: (i, j)),
        out_shape=jax.ShapeDtypeStruct((m, n), out_dtype),
        compiler_params=_params(2),
        name=name,
    )(a, w)


def _matmul_wt_kernel(a_ref, wt_ref, s_ref, o_ref, wb_ref, *carry_ref, seq_rows):
    i = pl.program_id(1)

    @pl.when(i == 0)
    def _():
        wb_ref[...] = wt_ref[...].astype(BF16)
        for ref in carry_ref:
            ref[...] = jnp.zeros_like(ref)

    x = lax.dot_general(a_ref[...], wb_ref[...], NT, preferred_element_type=F32)
    if seq_rows is None:
        o_ref[...] = (x * s_ref[...]).astype(o_ref.dtype)
    else:
        tm = x.shape[0]
        last, = carry_ref
        first_row = lax.broadcasted_iota(jnp.int32, x.shape, 0) == 0
        carried = jnp.where((i * tm) % seq_rows == 0, 0.0, last[0:1, :])
        prev = jnp.where(first_row, carried, pltpu.roll(x, 1, 0))
        last[0:1, :] = x[tm - 1:tm, :]
        o_ref[...] = (x + (prev - x) * s_ref[...]).astype(o_ref.dtype)


def _matmul_wt(a, wt, row0, n, col_vec, out_dtype, tm, tn, name, seq_rows=None):
    m, k = a.shape
    scratch = [pltpu.VMEM((tn, k), BF16)]
    if seq_rows is not None:
        assert seq_rows % tm == 0
        scratch.append(pltpu.VMEM((8, tn), F32))
    return pl.pallas_call(
        functools.partial(_matmul_wt_kernel, seq_rows=seq_rows),
        grid=(n // tn, m // tm),
        in_specs=[pl.BlockSpec((tm, k), lambda j, i: (i, 0)),
                  pl.BlockSpec((pl.Element(tn), pl.Element(k)), lambda j, i: (pl.multiple_of(row0 + j * tn, 8), 0)),
                  pl.BlockSpec((1, tn), lambda j, i: (0, j))],
        out_specs=pl.BlockSpec((tm, tn), lambda j, i: (i, j)),
        out_shape=jax.ShapeDtypeStruct((m, n), out_dtype),
        scratch_shapes=scratch,
        compiler_params=_params(2, VMEM_LIMIT_PROJ),
        name=name,
    )(a, wt, col_vec)


def _t5_bucket(dist):
    n = jnp.maximum(dist, 0)
    max_exact = REL_BUCKETS // 2
    nf = jnp.maximum(n, max_exact).astype(F32)
    large = max_exact + (jnp.log(nf / max_exact) / math.log(REL_MAX_DIST / max_exact) * (REL_BUCKETS - max_exact)).astype(jnp.int32)
    large = jnp.minimum(large, REL_BUCKETS - 1)
    return jnp.where(n < max_exact, n, large)


def _bias_kernel(rel_ref, bucket_ref, o_ref):
    h = pl.program_id(0)
    bucket = bucket_ref[...]
    acc = jnp.zeros(bucket.shape, F32)
    for b in range(REL_BUCKETS):
        acc = jnp.where(bucket == b, rel_ref[b, h], acc)
    qpos = lax.broadcasted_iota(jnp.int32, bucket.shape, 0) + MOBA_BLOCK
    kpos = lax.broadcasted_iota(jnp.int32, bucket.shape, 1)
    o_ref[0] = jnp.where(kpos <= qpos, acc * LOG2E, MASKED)


def _bias_tiles(rel_bias):
    blk = MOBA_BLOCK
    qpos = lax.broadcasted_iota(jnp.int32, (blk, 2 * blk), 0) + blk
    kpos = lax.broadcasted_iota(jnp.int32, (blk, 2 * blk), 1)
    bucket = _t5_bucket(qpos - kpos)
    return pl.pallas_call(
        _bias_kernel,
        grid=(A_HEADS,),
        in_specs=[pl.BlockSpec(memory_space=pltpu.SMEM), pl.BlockSpec((blk, 2 * blk), lambda h: (0, 0))],
        out_specs=pl.BlockSpec((1, blk, 2 * blk), lambda h: (h, 0, 0)),
        out_shape=jax.ShapeDtypeStruct((A_HEADS, blk, 2 * blk), F32),
        compiler_params=_params(1),
        name="t5_bias",
    )(rel_bias, bucket)


def _moba_kernel(rel_ref, q_ref, k_ref, v_ref, z_ref, bias_ref, o_ref, kmean_ref, m_ref, acc_ref):
    hb = pl.program_id(1)
    qi = pl.program_id(2)
    blk, hd = MOBA_BLOCK, A_HEAD_DIM
    n_heads = q_ref.shape[1] // hd
    nb = k_ref.shape[0] // blk
    nbp = kmean_ref.shape[1]

    @pl.when(qi == 0)
    def _():
        kmean_ref[...] = jnp.zeros_like(kmean_ref)
        for j in range(nb):
            mean_j = jnp.sum(k_ref[j * blk:(j + 1) * blk, :].astype(F32), axis=0, keepdims=True) * (1.0 / blk)
            for hh in range(n_heads):
                kmean_ref[hh, j:j + 1, :] = mean_j[:, hh * hd:(hh + 1) * hd]

    n_far = min(MOBA_FAR_GROUP, nb)
    never = LANES - 1
    blk_id = lax.broadcasted_iota(jnp.int32, (nbp, blk), 0)
    blk_f = blk_id.astype(F32)
    eye = jnp.where(lax.broadcasted_iota(jnp.int32, (nbp, LANES), 0) == lax.broadcasted_iota(jnp.int32, (nbp, LANES), 1), 1.0, 0.0).astype(BF16)
    j_prev = jnp.maximum(qi - 1, 0)
    own = pl.multiple_of(qi * blk, blk)
    prev = pl.multiple_of(j_prev * blk, blk)

    heads = [slice(hh * hd, (hh + 1) * hd) for hh in range(n_heads)]
    q = [q_ref[:, hl] for hl in heads]

    def gate_of(qh, hh):
        km = kmean_ref[hh]
        km_hi = km.astype(BF16)
        km_lo = (km - km_hi.astype(F32)).astype(BF16)
        return jnp.where(blk_id < qi, _mm(km_hi, qh, NT) + _mm(km_lo, qh, NT), -jnp.inf)

    g = _each(gate_of, q, range(n_heads))
    sel_t = [jnp.zeros((nbp, blk), F32) for _ in heads]
    for _ in range(MOBA_TOPK):
        gmax = _each(lambda x: jnp.max(x, axis=0, keepdims=True), g)
        first = _each(lambda x, mx: jnp.min(jnp.where(x == mx, blk_f, float(nbp)), axis=0, keepdims=True), g, gmax)
        pick = _each(lambda f, mx: (blk_f == f) & (mx > -jnp.inf), first, gmax)
        sel_t = _each(lambda p, s: jnp.where(p, 1.0, s), pick, sel_t)
        g = _each(lambda p, x: jnp.where(p, -jnp.inf, x), pick, g)
    sel = _each(lambda s: _mm(s.astype(BF16), eye, TN), sel_t)
    qm = _each(lambda qh, s: jnp.concatenate([qh, jnp.where(s > 0.0, 0.0, MASKED).astype(BF16)], axis=1), q, sel)

    def masked_logits(keys, key_blk):
        col = jnp.where(lax.broadcasted_iota(jnp.int32, key_blk.shape, 1) == key_blk, 1.0, 0.0).astype(BF16)
        return _each(lambda x, kk: _mm(x, jnp.concatenate([kk, col], axis=1), NT), qm, keys)

    def pv(p, values):
        ones = jnp.ones(values[0].shape, BF16)
        return _each(lambda x, v: jnp.dot(x.astype(BF16), jnp.concatenate([v, ones], axis=1), preferred_element_type=F32), p, values)

    cat0 = lambda ref, hl: jnp.concatenate([ref[pl.ds(prev, blk), hl], ref[pl.ds(own, blk), hl]], axis=0)
    key_row = lax.broadcasted_iota(jnp.int32, (2 * blk, LANES), 0)
    raw = masked_logits([cat0(k_ref, hl) for hl in heads], jnp.where(key_row < blk, j_prev, -1))
    t = _each(lambda x, hh: x + bias_ref[hh], raw, range(n_heads))
    m = _each(lambda x: jnp.max(x, axis=-1, keepdims=True), t)
    p = _each(lambda x, mx: jnp.exp2(x - mx), t, m)
    acc = pv(p, [cat0(v_ref, hl) for hl in heads])
    for hh in range(n_heads):
        m_ref[hh] = m[hh]
        acc_ref[hh] = acc[hh]

    bias_far = [rel_ref[REL_BUCKETS - 1, hb * n_heads + hh] * LOG2E for hh in range(n_heads)]
    far_row_blk = lax.broadcasted_iota(jnp.int32, (n_far * blk, LANES), 0) // blk

    def body(gi, carry):
        rows = pl.ds(pl.multiple_of(gi * (n_far * blk), n_far * blk), n_far * blk)
        key_blk = gi * n_far + far_row_blk
        raw = masked_logits([k_ref[rows, hl] for hl in heads], jnp.where(key_blk < j_prev, key_blk, never))
        m = [m_ref[hh] for hh in range(n_heads)]
        m_new = _each(lambda mx, x, b: jnp.maximum(mx, jnp.max(x, axis=-1, keepdims=True) + b), m, raw, bias_far)
        p = _each(lambda x, mn, b: jnp.exp2(x + (b - mn)), raw, m_new, bias_far)
        alpha = _each(lambda mx, mn: jnp.exp2(mx - mn), m, m_new)
        upd = pv(p, [v_ref[rows, hl] for hl in heads])
        for hh in range(n_heads):
            m_ref[hh] = m_new[hh]
            acc_ref[hh] = alpha[hh] * acc_ref[hh] + upd[hh]
        return carry

    lax.fori_loop(0, (j_prev + n_far - 1) // n_far, body, 0)
    for hh, hl in enumerate(heads):
        y = acc_ref[hh, :, :hd] / acc_ref[hh, :, hd:]
        o_ref[:, hl] = (y * _silu(z_ref[:, hl].astype(F32))).astype(o_ref.dtype)


def _moba(p_a, rel_bias, bias_tiles, bsz, seq):
    blk = MOBA_BLOCK
    nq = seq // blk
    hw = MOBA_HEADS_PER_STEP * A_HEAD_DIM
    ns = A_WIDTH // hw
    nbp = -(-nq // BF16_SUBLANES) * BF16_SUBLANES
    assert nbp < LANES
    return pl.pallas_call(
        _moba_kernel,
        grid=(bsz, ns, nq),
        in_specs=[
            pl.BlockSpec(memory_space=pltpu.SMEM),
            pl.BlockSpec((blk, hw), lambda b, h, i: (b * nq + i, h)),
            pl.BlockSpec((seq, hw), lambda b, h, i: (b, ns + h)),
            pl.BlockSpec((seq, hw), lambda b, h, i: (b, 2 * ns + h)),
            pl.BlockSpec((blk, hw), lambda b, h, i: (b * nq + i, 3 * ns + h)),
            pl.BlockSpec((MOBA_HEADS_PER_STEP, blk, 2 * blk), lambda b, h, i: (h, 0, 0)),
        ],
        out_specs=pl.BlockSpec((blk, hw), lambda b, h, i: (b * nq + i, h)),
        out_shape=jax.ShapeDtypeStruct((bsz * seq, A_WIDTH), BF16),
        scratch_shapes=[pltpu.VMEM((MOBA_HEADS_PER_STEP, nbp, A_HEAD_DIM), F32), pltpu.VMEM((MOBA_HEADS_PER_STEP, blk, 1), F32),
                        pltpu.VMEM((MOBA_HEADS_PER_STEP, blk, 2 * A_HEAD_DIM), F32)],
        compiler_params=_params(3),
        name="moba_attention",
    )(rel_bias, p_a, p_a, p_a, p_a, bias_tiles)


def _cross_kernel(q_ref, k_ref, v_ref, z_ref, o_ref):
    hd = C_HEAD_DIM
    heads = [slice(h * hd, (h + 1) * hd) for h in range(C_HEADS)]
    s = _each(lambda hl: _mm(q_ref[:, hl], k_ref[:, hl], NT) * (hd ** -0.5), heads)
    m = _each(lambda x: jnp.max(x, axis=-1, keepdims=True), s)
    p = _each(lambda x, mx: jnp.exp(x - mx), s, m)
    l = _each(lambda x: jnp.sum(x, axis=-1, keepdims=True), p)
    y = _each(lambda x, hl: jnp.dot(x.astype(BF16), v_ref[:, hl], preferred_element_type=F32), p, heads)
    for hl, yh, lh in zip(heads, y, l):
        o_ref[:, hl] = (yh / lh * _silu(z_ref[:, hl].astype(F32))).astype(o_ref.dtype)


def _cross(p_c, kv, bsz, seq, mem_len, tq):
    nt = seq // tq
    return pl.pallas_call(
        _cross_kernel,
        grid=(bsz, nt),
        in_specs=[
            pl.BlockSpec((tq, C_WIDTH), lambda b, i: (b * nt + i, 0)),
            pl.BlockSpec((mem_len, C_WIDTH), lambda b, i: (b, 0)),
            pl.BlockSpec((mem_len, C_WIDTH), lambda b, i: (b, 1)),
            pl.BlockSpec((tq, C_WIDTH), lambda b, i: (b * nt + i, 1)),
        ],
        out_specs=pl.BlockSpec((tq, C_WIDTH), lambda b, i: (b * nt + i, 0)),
        out_shape=jax.ShapeDtypeStruct((bsz * seq, C_WIDTH), BF16),
        compiler_params=_params(2),
        name="memory_attention",
    )(p_c, kv, kv, p_c)


class _WkvConsts:
    def __init__(self):
        c, g, hd = WKV_CHUNK, WKV_GROUP, B_HEAD_DIM
        row = lax.broadcasted_iota(jnp.int32, (g, g), 0)
        col = lax.broadcasted_iota(jnp.int32, (g, g), 1)
        self.same_head = jnp.where((row // hd) == (col // hd), 1.0, 0.0)
        self.ones_bd = self.same_head.astype(BF16)
        t = lax.broadcasted_iota(jnp.int32, (c, g), 0)
        s = lax.broadcasted_iota(jnp.int32, (c, g), 1) % hd
        self.strict = s < t
        self.incl = s <= t
        self.strict_incl = jnp.concatenate([self.strict, self.incl], axis=0)
        self.eye = jnp.where(s == t, 1.0, 0.0).astype(F32)
        tr = lax.broadcasted_iota(jnp.int32, (c, c), 0)
        tc = lax.broadcasted_iota(jnp.int32, (c, c), 1)
        self.lower = jnp.where(tc <= tr, 1.0, 0.0).astype(BF16)

    def bd(self, p):
        return jnp.concatenate([p] * WKV_HEADS_PER_GROUP, axis=0) * self.ones_bd


def _wkv_prepare(rm, kmod, vm, avec, bvec, logdec, k):
    c = WKV_CHUNK
    cat = lambda u, w: jnp.concatenate([u, w], axis=0)
    cl = _each(lambda x: jnp.dot(k.lower, _bf(x), preferred_element_type=F32), logdec)
    cl_last = _each(lambda x: x[c - 1:c, :], cl)
    rt = _each(lambda r, x: _bf(r * jnp.exp(x)), rm, cl)
    at = _each(lambda a, x, ld: _bf(a * jnp.exp(x - ld)), avec, cl, logdec)
    e_neg = _each(lambda x: jnp.exp(-x), cl)
    bt = _each(lambda b, e: k.bd(_bf(b * e)), bvec, e_neg)
    kt = _each(lambda kk, e: k.bd(_bf(kk * e)), kmod, e_neg)
    decay = _each(jnp.exp, cl_last)
    e_rem = _each(lambda d, e: d * e, decay, e_neg)
    bkh = _each(lambda b, kk, e: cat(_bf(b * e), _bf(kk * e)), bvec, kmod, e_rem)
    vb = _each(_bf, vm)

    lhs = _each(cat, at, rt)
    ab = _each(lambda x, y: _mm(x, y, NT), lhs, bt)
    ak = _each(lambda x, y: _mm(x, y, NT), lhs, kt)
    a_ab = _each(lambda x: jnp.where(k.strict, x[:c], 0.0), ab)
    a_rb = _each(lambda x: _bf(jnp.where(k.incl, x[c:], 0.0)), ab)
    a_k = _each(lambda x: _bf(jnp.where(k.strict_incl, x, 0.0)), ak)

    inv = _each(lambda x: k.eye + x, a_ab)
    pw = _each(_bf, a_ab)
    pw = _each(lambda p: _bf(_mm(p, k.bd(p), NN)), pw)
    for _ in range(int(math.log2(c)) - 2):
        tp = _each(lambda i, p: _mm(cat(_bf(i), p), k.bd(p), NN), inv, pw)
        inv = _each(lambda i, t: i + t[:c], inv, tp)
        pw = _each(lambda t: _bf(t[c:]), tp)
    inv = _each(lambda i, p: _bf(i + _mm(_bf(i), k.bd(p), NN)), inv, pw)

    av = _each(lambda x, v: _mm(x, k.bd(v), NN), a_k, vb)
    w = _each(lambda i, a: _mm(i, k.bd(a), NN), inv, at)
    u0 = _each(lambda i, x: _mm(i, k.bd(_bf(x[:c])), NN), inv, av)
    wr = _each(lambda ww, r: cat(_bf(ww), r), w, rt)
    y0 = _each(lambda x: x[c:], av)
    return list(zip(wr, u0, y0, a_rb, vb, bkh, decay))


def _wkv_apply(prep, state, k):
    c = WKV_CHUNK
    wr, u0, y0, a_rb, vb, bkh, decay = (list(x) for x in zip(*prep))
    uy = _each(lambda x, s: _mm(x, _bf(s), NT), wr, state)
    ub = _each(lambda x, y: _bf(x[:c] + y), uy, u0)
    y = _each(lambda x, a, uu, z: x[c:] + _mm(a, k.bd(uu), NN) + z, uy, a_rb, ub, y0)
    upd = _each(lambda uu, v, bk: _mm(jnp.concatenate([uu, v], axis=0), bk, TN), ub, vb, bkh)
    new_state = _each(lambda s, dc, d: s * dc + d * k.same_head, state, decay, upd)
    return y, new_state


def _rwkv_kernel(r_ref, k_ref, v_ref, lwla_ref, z_ref, w0_ref, a0_ref, kk_ref, ka_ref, rk_ref, lnw_ref, lnb_ref,
                 wd_ref, wa_ref, o_ref, state_ref):
    c, g = WKV_CHUNK, WKV_GROUP
    n_groups = r_ref.shape[1] // g
    n_chunks = WKV_CHUNKS_PER_STEP

    @pl.when(pl.program_id(2) == 0)
    def _():
        state_ref[...] = jnp.zeros_like(state_ref)

    k = _WkvConsts()
    wd = _bf(wd_ref[...])
    wa = _bf(wa_ref[...])
    inv_hd = 1.0 / B_HEAD_DIM
    groups = [slice(gi * g, (gi + 1) * g) for gi in range(n_groups)]
    split = lambda x: [x[ch * c:(ch + 1) * c, ln] for ch in range(n_chunks) for ln in groups]

    def segsums(xs):
        s = jnp.dot(jnp.concatenate(_each(_bf, xs), axis=0), k.ones_bd, preferred_element_type=F32)
        return [s[i * c:(i + 1) * c] for i in range(len(xs))]

    def step(ci, carry):
        rows = pl.ds(pl.multiple_of(ci * (n_chunks * c), n_chunks * c), n_chunks * c)
        rm, km, vm = r_ref[rows, :], k_ref[rows, :], v_ref[rows, :]
        lwla = lwla_ref[rows, :]
        lw = _bf(jnp.tanh(lwla[:, :LORA_PAD]))
        la = _bf(lwla[:, LORA_PAD:])
        logdec = -math.exp(-0.5) * _sigmoid(w0_ref[...] + jnp.dot(lw, wd, preferred_element_type=F32))
        a_lr = _sigmoid(a0_ref[...] + jnp.dot(la, wa, preferred_element_type=F32))
        kmod = km * (1.0 + (a_lr - 1.0) * ka_ref[...])
        kk = split(km * kk_ref[...])
        n = len(kk)
        sums = segsums(_each(lambda x: x * x, kk) + split(rm * kmod * rk_ref[...]))
        kk = _each(lambda x, ss: x * jnp.minimum(lax.rsqrt(ss), 1e12), kk, sums[:n])
        bonus = _each(lambda s, v: s * v, sums[n:], split(vm))
        prep = _wkv_prepare(split(rm), split(kmod), split(vm), _each(lambda x: -x, kk),
                            _each(lambda x, a: x * a, kk, split(a_lr)), split(logdec), k)
        state = [state_ref[gi] for gi in range(n_groups)]
        y = []
        for ch in range(n_chunks):
            y_ch, state = _wkv_apply(prep[ch * n_groups:(ch + 1) * n_groups], state, k)
            y += y_ch
        for gi in range(n_groups):
            state_ref[gi] = state[gi]
        yc = _each(lambda x, s: x - s * inv_hd, y, segsums(y))
        var = _each(lambda s: s * inv_hd, segsums(_each(lambda x: x * x, yc)))
        gate = _silu(z_ref[rows, :].astype(F32))
        for ch in range(n_chunks):
            out_rows = pl.ds(pl.multiple_of(ci * (n_chunks * c) + ch * c, c), c)
            for gi, ln in enumerate(groups):
                i = ch * n_groups + gi
                yn = yc[i] * lax.rsqrt(var[i] + LNX_EPS) * lnw_ref[:, ln] + lnb_ref[:, ln]
                o_ref[out_rows, ln] = ((yn + bonus[i]) * gate[ch * c:(ch + 1) * c, ln]).astype(o_ref.dtype)
        return carry

    lax.fori_loop(0, r_ref.shape[0] // (n_chunks * c), step, 0)


def _rwkv(p_b, p_lora, p_zb, vecs, wd2, wa2, bsz, seq, tt):
    g = WKV_GROUP
    gw = WKV_GROUPS_PER_STEP * g
    ns = B_WIDTH // gw
    nt = seq // tt
    row = lambda b, gi, t: b * nt + t
    vec_spec = pl.BlockSpec((1, gw), lambda b, gi, t: (0, gi))
    lora_w_spec = pl.BlockSpec((LORA_PAD, gw), lambda b, gi, t: (0, gi))
    return pl.pallas_call(
        _rwkv_kernel,
        grid=(bsz, ns, nt),
        in_specs=[
            pl.BlockSpec((tt, gw), lambda b, gi, t: (row(b, gi, t), gi)),
            pl.BlockSpec((tt, gw), lambda b, gi, t: (row(b, gi, t), ns + gi)),
            pl.BlockSpec((tt, gw), lambda b, gi, t: (row(b, gi, t), 2 * ns + gi)),
            pl.BlockSpec((tt, 2 * LORA_PAD), lambda b, gi, t: (row(b, gi, t), 0)),
            pl.BlockSpec((tt, gw), lambda b, gi, t: (row(b, gi, t), gi)),
        ] + [vec_spec] * len(vecs) + [lora_w_spec, lora_w_spec],
        out_specs=pl.BlockSpec((tt, gw), lambda b, gi, t: (row(b, gi, t), gi)),
        out_shape=jax.ShapeDtypeStruct((bsz * seq, B_WIDTH), BF16),
        scratch_shapes=[pltpu.VMEM((WKV_GROUPS_PER_STEP, g, g), F32)],
        compiler_params=_params(3),
        name="rwkv7_time_mix",
    )(p_b, p_b, p_b, p_lora, p_zb, *vecs, wd2, wa2)


def _merge_kernel(ya_ref, yb_ref, yc_ref, wa_ref, wb_ref, wc_ref, ga_ref, gb_ref, gc_ref, o_ref):
    def branch(y_ref, w_ref, g_ref):
        return _sigmoid(g_ref[...].astype(F32)) * jnp.dot(y_ref[...], w_ref[...], preferred_element_type=F32)

    o_ref[...] = (branch(ya_ref, wa_ref, ga_ref) + branch(yb_ref, wb_ref, gb_ref) + branch(yc_ref, wc_ref, gc_ref)).astype(o_ref.dtype)


def _merge(ya, yb, yc, wa, wb, wc, p_c, gate_col0, d_model, tm, tn):
    m = ya.shape[0]
    nj = d_model // tn
    j0 = gate_col0 // tn
    y_spec = lambda width: pl.BlockSpec((tm, width), lambda j, i: (i, 0))
    w_spec = lambda width: pl.BlockSpec((width, tn), lambda j, i: (0, j))
    g_spec = lambda br: pl.BlockSpec((tm, tn), lambda j, i: (i, j0 + br * nj + j))
    return pl.pallas_call(
        _merge_kernel,
        grid=(nj, m // tm),
        in_specs=[y_spec(A_WIDTH), y_spec(B_WIDTH), y_spec(C_WIDTH), w_spec(A_WIDTH), w_spec(B_WIDTH), w_spec(C_WIDTH),
                  g_spec(0), g_spec(1), g_spec(2)],
        out_specs=pl.BlockSpec((tm, tn), lambda j, i: (i, j)),
        out_shape=jax.ShapeDtypeStruct((m, d_model), BF16),
        compiler_params=_params(2, VMEM_LIMIT_PROJ),
        name="gated_merge",
    )(ya, yb, yc, wa, wb, wc, p_c, p_c, p_c)


def _out_kernel(m_ref, w_ref, x_ref, g_ref, o_ref, *, final_norm):
    y = x_ref[...] + jnp.dot(m_ref[...], w_ref[...], preferred_element_type=F32)
    if final_norm:
        y = y * lax.rsqrt(jnp.mean(y * y, axis=-1, keepdims=True) + RMS_EPS) * g_ref[...]
    o_ref[...] = y


def _out_proj(merged, w_out, x2d, g, final_norm, tm):
    m, d = x2d.shape
    return pl.pallas_call(
        functools.partial(_out_kernel, final_norm=final_norm),
        grid=(m // tm,),
        in_specs=[pl.BlockSpec((tm, d), lambda i: (i, 0)), pl.BlockSpec((d, d), lambda i: (0, 0)),
                  pl.BlockSpec((tm, d), lambda i: (i, 0)), pl.BlockSpec((1, d), lambda i: (0, 0))],
        out_specs=pl.BlockSpec((tm, d), lambda i: (i, 0)),
        out_shape=jax.ShapeDtypeStruct((m, d), F32),
        compiler_params=_params(1),
        name="out_proj",
    )(merged, w_out, x2d, g.reshape(1, d))


def _pick_tile(n, candidates):
    for t in candidates:
        if n % t == 0:
            return t
    raise ValueError(f"no tile for extent {n}")


class _Tiles(NamedTuple):
    norm_rows: int
    proj_rows: int
    proj_cols: int
    proj_cols_b: int
    mem_rows: int
    cross_rows: int
    wkv_rows: int
    merge_rows: int
    merge_cols: int
    out_rows: int


def _tiles(m, seq, mem_rows):
    return _Tiles(
        norm_rows=_pick_tile(m, (1024, 512, 256)),
        proj_rows=_pick_tile(seq, (2048, 1024, 512, 256)),
        proj_cols=1024,
        proj_cols_b=768,
        mem_rows=_pick_tile(mem_rows, (1024, 512, 256)),
        cross_rows=_pick_tile(seq, (1024, 512, 256)),
        wkv_rows=_pick_tile(seq, (512, 256, 128, 64)),
        merge_rows=_pick_tile(m, (1024, 512, 256)),
        merge_cols=1024,
        out_rows=_pick_tile(m, (512, 256)),
    )


def _pad_cols(w, width):
    return jnp.pad(w, ((0, 0), (0, width - w.shape[1])))


def _layer(x2d, mem2d, bsz, seq, mem_len, bias_tiles, rel_bias, norm_g, mem_norm_g, w_in, rw, w_mem_kv, w_proj_a, w_proj_b,
           w_proj_c, w_out, final_g):
    m, d = x2d.shape
    t = _tiles(m, seq, mem2d.shape[0])

    c_z_a = 3 * A_WIDTH
    c_rkv_b = c_z_a + A_WIDTH
    c_z_b = c_rkv_b + 3 * B_WIDTH
    c_lw = c_z_b + B_WIDTH
    c_la = c_lw + LORA
    c_q_c = c_la + LORA

    h = _rmsnorm(x2d, norm_g, BF16, t.norm_rows)
    ones = lambda n: jnp.ones((1, n), F32)
    wt = w_in.T
    scale_a = jnp.concatenate([jnp.full((1, A_WIDTH), MOBA_Q_SCALE, F32), ones(3 * A_WIDTH)], axis=1)
    p_a = _matmul_wt(h, wt, 0, 4 * A_WIDTH, scale_a, BF16, t.proj_rows, t.proj_cols, "in_proj_a")
    (mu_r, mu_k, mu_v, mu_w, mu_a, w0, w_decay2, a0, w_aaa2, k_k, k_a, r_k, lnx_w, lnx_b) = rw
    vec = lambda v: v.reshape(1, -1)
    p_b = _matmul_wt(h, wt, c_rkv_b, 3 * B_WIDTH, jnp.concatenate([vec(mu_r), vec(mu_k), vec(mu_v)], axis=1), F32,
                     t.proj_rows, t.proj_cols_b, "in_proj_b", seq_rows=seq)
    p_zb = _matmul_wt(h, wt, c_z_b, B_WIDTH, ones(B_WIDTH), BF16, t.proj_rows, t.proj_cols_b, "in_proj_zb")
    pad_rows = lambda w: jnp.pad(w, ((0, LORA_PAD - LORA), (0, 0)))
    wt_lora = jnp.concatenate([pad_rows(wt[c_lw:c_la]), pad_rows(wt[c_la:c_q_c])], axis=0)
    mu_wa = jnp.concatenate([_pad_cols(vec(mu_w), LORA_PAD), _pad_cols(vec(mu_a), LORA_PAD)], axis=1)
    p_lora = _matmul_wt(h, wt_lora, 0, 2 * LORA_PAD, mu_wa, F32, t.proj_rows, 2 * LORA_PAD, "in_proj_lora", seq_rows=seq)
    n_c = 2 * C_WIDTH + N_BRANCHES * d
    p_c = _matmul_wt(h, wt, c_q_c, n_c, ones(n_c), BF16, t.proj_rows, t.proj_cols, "in_proj_c")

    mem_n = _rmsnorm(mem2d, mem_norm_g, BF16, t.mem_rows)
    kv = _matmul(mem_n, w_mem_kv.astype(BF16), BF16, t.mem_rows, t.proj_cols, "mem_kv")

    ya = _moba(p_a, rel_bias, bias_tiles, bsz, seq)
    yc = _cross(p_c, kv, bsz, seq, mem_len, t.cross_rows)

    vecs = (vec(w0), vec(a0), vec(k_k), vec(k_a), vec(r_k), vec(lnx_w), vec(lnx_b))
    yb = _rwkv(p_b, p_lora, p_zb, vecs, pad_rows(w_decay2), pad_rows(w_aaa2), bsz, seq, t.wkv_rows)

    merged = _merge(ya, yb, yc, w_proj_a.astype(BF16), w_proj_b.astype(BF16), w_proj_c.astype(BF16), p_c, 2 * C_WIDTH, d,
                    t.merge_rows, t.merge_cols)
    g = final_g if final_g is not None else jnp.ones((d,), F32)
    return _out_proj(merged, w_out.astype(BF16), x2d, g, final_g is not None, t.out_rows)


def kernel(x, mem, rel_bias, norm_g, mem_norm_g, w_in, rw_mu_r, rw_mu_k, rw_mu_v, rw_mu_w, rw_mu_a, rw_w0, rw_w_decay2, rw_a0, rw_w_aaa2, rw_k_k, rw_k_a, rw_r_k, rw_lnx_w, rw_lnx_b, w_mem_kv, w_proj_a, w_proj_b, w_proj_c, w_out, final_norm_g):
    bsz, seq, d = x.shape
    mem_len = mem.shape[1]
    depth = norm_g.shape[0]
    x2d = x.reshape(bsz * seq, d)
    mem2d = mem.reshape(bsz * mem_len, d)
    bias_tiles = _bias_tiles(rel_bias)
    for l in range(depth):
        rw = (rw_mu_r[l], rw_mu_k[l], rw_mu_v[l], rw_mu_w[l], rw_mu_a[l], rw_w0[l], rw_w_decay2[l], rw_a0[l], rw_w_aaa2[l],
              rw_k_k[l], rw_k_a[l], rw_r_k[l], rw_lnx_w[l], rw_lnx_b[l])
        x2d = _layer(x2d, mem2d, bsz, seq, mem_len, bias_tiles, rel_bias, norm_g[l], mem_norm_g[l], w_in[l], rw, w_mem_kv[l],
                     w_proj_a[l], w_proj_b[l], w_proj_c[l], w_out[l], final_norm_g if l == depth - 1 else None)
    return x2d.reshape(bsz, seq, d)
```

```python
import functools
import math
from typing import NamedTuple

import jax
import jax.numpy as jnp
from jax import lax
from jax.experimental import pallas as pl
from jax.experimental.pallas import tpu as pltpu

F32 = jnp.float32
BF16 = jnp.bfloat16
LANES = 128
BF16_SUBLANES = 16

RMS_EPS = 1e-6

A_HEADS = 12
A_HEAD_DIM = 128
A_WIDTH = A_HEADS * A_HEAD_DIM
MOBA_BLOCK = 256
MOBA_TOPK = 3
MOBA_HEADS_PER_STEP = 6
MOBA_FAR_GROUP = 4
MOBA_Q_SCALE = A_HEAD_DIM ** -0.5 * math.log2(math.e)
REL_BUCKETS = 32
REL_MAX_DIST = 128

B_HEAD_DIM = 64
B_WIDTH = 1536
LORA = 96
LORA_PAD = LANES
LNX_EPS = 64e-5
WKV_CHUNK = 64
WKV_GROUP = 256
WKV_HEADS_PER_GROUP = WKV_GROUP // B_HEAD_DIM
WKV_GROUPS_PER_STEP = 6
WKV_CHUNKS_PER_STEP = 4

C_HEADS = 4
C_HEAD_DIM = 256
C_WIDTH = C_HEADS * C_HEAD_DIM

N_BRANCHES = 3

VMEM_LIMIT = 48 * 1024 * 1024
VMEM_LIMIT_PROJ = 58 * 1024 * 1024

NN = (((1,), (0,)), ((), ()))
NT = (((1,), (1,)), ((), ()))
TN = (((0,), (0,)), ((), ()))

MASKED = -1e30
LOG2E = math.log2(math.e)


def _params(n_axes, vmem_limit=VMEM_LIMIT):
    return pltpu.CompilerParams(dimension_semantics=("arbitrary",) * n_axes, vmem_limit_bytes=vmem_limit)


def _sigmoid(x):
    return 0.5 * jnp.tanh(0.5 * x) + 0.5


def _silu(z):
    return z * _sigmoid(z)


def _bf(x):
    return x.astype(BF16)


def _mm(a, b, dn):
    return lax.dot_general(a, b, dn, preferred_element_type=F32)


def _each(f, *lists):
    return [f(*args) for args in zip(*lists)]


def _rmsnorm_kernel(x_ref, g_ref, o_ref):
    x = x_ref[...].astype(F32)
    y = x * lax.rsqrt(jnp.mean(x * x, axis=-1, keepdims=True) + RMS_EPS)
    o_ref[...] = (y * g_ref[...]).astype(o_ref.dtype)


def _rmsnorm(x2d, g, out_dtype, tm):
    m, d = x2d.shape
    return pl.pallas_call(
        _rmsnorm_kernel,
        grid=(m // tm,),
        in_specs=[pl.BlockSpec((tm, d), lambda i: (i, 0)), pl.BlockSpec((1, d), lambda i: (0, 0))],
        out_specs=pl.BlockSpec((tm, d), lambda i: (i, 0)),
        out_shape=jax.ShapeDtypeStruct((m, d), out_dtype),
        compiler_params=_params(1),
        name="rmsnorm",
    )(x2d, g.reshape(1, d))


def _matmul_kernel(a_ref, w_ref, o_ref):
    o_ref[...] = jnp.dot(a_ref[...], w_ref[...], preferred_element_type=F32).astype(o_ref.dtype)


def _matmul(a, w, out_dtype, tm, tn, name):
    m, k = a.shape
    n = w.shape[1]
    return pl.pallas_call(
        _matmul_kernel,
        grid=(n // tn, m // tm),
        in_specs=[pl.BlockSpec((tm, k), lambda j, i: (i, 0)), pl.BlockSpec((k, tn), lambda j, i: (0, j))],
        out_specs=pl.BlockSpec((tm, tn), lambda j, i: (i, j)),
        out_shape=jax.ShapeDtypeStruct((m, n), out_dtype),
        compiler_params=_params(2),
        name=name,
    )(a, w)


def _matmul_wt_kernel(a_ref, wt_ref, s_ref, o_ref, wb_ref, *carry_ref, seq_rows):
    i = pl.program_id(1)

    @pl.when(i == 0)
    def _():
        wb_ref[...] = wt_ref[...].astype(BF16)
        for ref in carry_ref:
            ref[...] = jnp.zeros_like(ref)

    x = lax.dot_general(a_ref[...], wb_ref[...], NT, preferred_element_type=F32)
    if seq_rows is None:
        o_ref[...] = (x * s_ref[...]).astype(o_ref.dtype)
    else:
        tm = x.shape[0]
        last, = carry_ref
        first_row = lax.broadcasted_iota(jnp.int32, x.shape, 0) == 0
        carried = jnp.where((i * tm) % seq_rows == 0, 0.0, last[0:1, :])
        prev = jnp.where(first_row, carried, pltpu.roll(x, 1, 0))
        last[0:1, :] = x[tm - 1:tm, :]
        o_ref[...] = (x + (prev - x) * s_ref[...]).astype(o_ref.dtype)


def _matmul_wt(a, wt, row0, n, col_vec, out_dtype, tm, tn, name, seq_rows=None):
    m, k = a.shape
    scratch = [pltpu.VMEM((tn, k), BF16)]
    if seq_rows is not None:
        assert seq_rows % tm == 0
        scratch.append(pltpu.VMEM((8, tn), F32))
    return pl.pallas_call(
        functools.partial(_matmul_wt_kernel, seq_rows=seq_rows),
        grid=(n // tn, m // tm),
        in_specs=[pl.BlockSpec((tm, k), lambda j, i: (i, 0)),
                  pl.BlockSpec((pl.Element(tn), pl.Element(k)), lambda j, i: (pl.multiple_of(row0 + j * tn, 8), 0)),
                  pl.BlockSpec((1, tn), lambda j, i: (0, j))],
        out_specs=pl.BlockSpec((tm, tn), lambda j, i: (i, j)),
        out_shape=jax.ShapeDtypeStruct((m, n), out_dtype),
        scratch_shapes=scratch,
        compiler_params=_params(2, VMEM_LIMIT_PROJ),
        name=name,
    )(a, wt, col_vec)


def _t5_bucket(dist):
    n = jnp.maximum(dist, 0)
    max_exact = REL_BUCKETS // 2
    nf = jnp.maximum(n, max_exact).astype(F32)
    large = max_exact + (jnp.log(nf / max_exact) / math.log(REL_MAX_DIST / max_exact) * (REL_BUCKETS - max_exact)).astype(jnp.int32)
    large = jnp.minimum(large, REL_BUCKETS - 1)
    return jnp.where(n < max_exact, n, large)


def _bias_kernel(rel_ref, bucket_ref, o_ref):
    h = pl.program_id(0)
    bucket = bucket_ref[...]
    acc = jnp.zeros(bucket.shape, F32)
    for b in range(REL_BUCKETS):
        acc = jnp.where(bucket == b, rel_ref[b, h], acc)
    qpos = lax.broadcasted_iota(jnp.int32, bucket.shape, 0) + MOBA_BLOCK
    kpos = lax.broadcasted_iota(jnp.int32, bucket.shape, 1)
    o_ref[0] = jnp.where(kpos <= qpos, acc * LOG2E, MASKED)


def _bias_tiles(rel_bias):
    blk = MOBA_BLOCK
    qpos = lax.broadcasted_iota(jnp.int32, (blk, 2 * blk), 0) + blk
    kpos = lax.broadcasted_iota(jnp.int32, (blk, 2 * blk), 1)
    bucket = _t5_bucket(qpos - kpos)
    return pl.pallas_call(
        _bias_kernel,
        grid=(A_HEADS,),
        in_specs=[pl.BlockSpec(memory_space=pltpu.SMEM), pl.BlockSpec((blk, 2 * blk), lambda h: (0, 0))],
        out_specs=pl.BlockSpec((1, blk, 2 * blk), lambda h: (h, 0, 0)),
        out_shape=jax.ShapeDtypeStruct((A_HEADS, blk, 2 * blk), F32),
        compiler_params=_params(1),
        name="t5_bias",
    )(rel_bias, bucket)


def _moba_kernel(rel_ref, q_ref, k_ref, v_ref, z_ref, bias_ref, o_ref, kmean_ref, m_ref, acc_ref):
    hb = pl.program_id(1)
    qi = pl.program_id(2)
    blk, hd = MOBA_BLOCK, A_HEAD_DIM
    n_heads = q_ref.shape[1] // hd
    nb = k_ref.shape[0] // blk
    nbp = kmean_ref.shape[1]

    @pl.when(qi == 0)
    def _():
        kmean_ref[...] = jnp.zeros_like(kmean_ref)
        for j in range(nb):
            mean_j = jnp.sum(k_ref[j * blk:(j + 1) * blk, :].astype(F32), axis=0, keepdims=True) * (1.0 / blk)
            for hh in range(n_heads):
                kmean_ref[hh, j:j + 1, :] = mean_j[:, hh * hd:(hh + 1) * hd]

    n_far = min(MOBA_FAR_GROUP, nb)
    never = LANES - 1
    blk_id = lax.broadcasted_iota(jnp.int32, (nbp, blk), 0)
    blk_f = blk_id.astype(F32)
    eye = jnp.where(lax.broadcasted_iota(jnp.int32, (nbp, LANES), 0) == lax.broadcasted_iota(jnp.int32, (nbp, LANES), 1), 1.0, 0.0).astype(BF16)
    j_prev = jnp.maximum(qi - 1, 0)
    own = pl.multiple_of(qi * blk, blk)
    prev = pl.multiple_of(j_prev * blk, blk)

    heads = [slice(hh * hd, (hh + 1) * hd) for hh in range(n_heads)]
    q = [q_ref[:, hl] for hl in heads]

    def gate_of(qh, hh):
        km = kmean_ref[hh]
        km_hi = km.astype(BF16)
        km_lo = (km - km_hi.astype(F32)).astype(BF16)
        return jnp.where(blk_id < qi, _mm(km_hi, qh, NT) + _mm(km_lo, qh, NT), -jnp.inf)

    g = _each(gate_of, q, range(n_heads))
    sel_t = [jnp.zeros((nbp, blk), F32) for _ in heads]
    for _ in range(MOBA_TOPK):
        gmax = _each(lambda x: jnp.max(x, axis=0, keepdims=True), g)
        first = _each(lambda x, mx: jnp.min(jnp.where(x == mx, blk_f, float(nbp)), axis=0, keepdims=True), g, gmax)
        pick = _each(lambda f, mx: (blk_f == f) & (mx > -jnp.inf), first, gmax)
        sel_t = _each(lambda p, s: jnp.where(p, 1.0, s), pick, sel_t)
        g = _each(lambda p, x: jnp.where(p, -jnp.inf, x), pick, g)
    sel = _each(lambda s: _mm(s.astype(BF16), eye, TN), sel_t)
    qm = _each(lambda qh, s: jnp.concatenate([qh, jnp.where(s > 0.0, 0.0, MASKED).astype(BF16)], axis=1), q, sel)

    lane = lax.broadcasted_iota(jnp.int32, (blk, LANES), 1)

    def masked_logits(keys, key_blocks):
        col = jnp.concatenate([jnp.where(lane == j, 1.0, 0.0).astype(BF16) for j in key_blocks], axis=0)
        return _each(lambda x, kk: _mm(x, jnp.concatenate([kk, col], axis=1), NT), qm, keys)

    def pv(p, values):
        ones = jnp.ones(values[0].shape, BF16)
        return _each(lambda x, v: jnp.dot(x.astype(BF16), jnp.concatenate([v, ones], axis=1), preferred_element_type=F32), p, values)

    cat0 = lambda ref, hl: jnp.concatenate([ref[pl.ds(prev, blk), hl], ref[pl.ds(own, blk), hl]], axis=0)
    raw = masked_logits([cat0(k_ref, hl) for hl in heads], [j_prev, -1])
    t = _each(lambda x, hh: x + bias_ref[hh], raw, range(n_heads))
    m = _each(lambda x: jnp.max(x, axis=-1, keepdims=True), t)
    p = _each(lambda x, mx: jnp.exp2(x - mx), t, m)
    acc = pv(p, [cat0(v_ref, hl) for hl in heads])
    for hh in range(n_heads):
        m_ref[hh] = jnp.broadcast_to(m[hh], (blk, LANES))
        acc_ref[hh] = acc[hh]

    bias_far = [rel_ref[REL_BUCKETS - 1, hb * n_heads + hh] * LOG2E for hh in range(n_heads)]

    def body(gi, carry):
        rows = pl.ds(pl.multiple_of(gi * (n_far * blk), n_far * blk), n_far * blk)
        blocks = [gi * n_far + j for j in range(n_far)]
        raw = masked_logits([k_ref[rows, hl] for hl in heads], [jnp.where(j < j_prev, j, never) for j in blocks])
        m = [m_ref[hh][:, :1] for hh in range(n_heads)]
        m_new = _each(lambda mx, x, b: jnp.maximum(mx, jnp.max(x, axis=-1, keepdims=True) + b), m, raw, bias_far)
        p = _each(lambda x, mn, b: jnp.exp2(x + (b - mn)), raw, m_new, bias_far)
        alpha = _each(lambda mx, mn: jnp.exp2(mx - mn), m, m_new)
        upd = pv(p, [v_ref[rows, hl] for hl in heads])
        for hh in range(n_heads):
            m_ref[hh] = jnp.broadcast_to(m_new[hh], (blk, LANES))
            acc_ref[hh] = alpha[hh] * acc_ref[hh] + upd[hh]
        return carry

    lax.fori_loop(0, (j_prev + n_far - 1) // n_far, body, 0)
    for hh, hl in enumerate(heads):
        y = acc_ref[hh, :, :hd] / acc_ref[hh, :, hd:]
        o_ref[:, hl] = (y * _silu(z_ref[:, hl].astype(F32))).astype(o_ref.dtype)


def _moba(p_a, rel_bias, bias_tiles, bsz, seq):
    blk = MOBA_BLOCK
    nq = seq // blk
    hw = MOBA_HEADS_PER_STEP * A_HEAD_DIM
    ns = A_WIDTH // hw
    nbp = -(-nq // BF16_SUBLANES) * BF16_SUBLANES
    assert nbp < LANES
    return pl.pallas_call(
        _moba_kernel,
        grid=(bsz, ns, nq),
        in_specs=[
            pl.BlockSpec(memory_space=pltpu.SMEM),
            pl.BlockSpec((blk, hw), lambda b, h, i: (b * nq + i, h)),
            pl.BlockSpec((seq, hw), lambda b, h, i: (b, ns + h)),
            pl.BlockSpec((seq, hw), lambda b, h, i: (b, 2 * ns + h)),
            pl.BlockSpec((blk, hw), lambda b, h, i: (b * nq + i, 3 * ns + h)),
            pl.BlockSpec((MOBA_HEADS_PER_STEP, blk, 2 * blk), lambda b, h, i: (h, 0, 0)),
        ],
        out_specs=pl.BlockSpec((blk, hw), lambda b, h, i: (b * nq + i, h)),
        out_shape=jax.ShapeDtypeStruct((bsz * seq, A_WIDTH), BF16),
        scratch_shapes=[pltpu.VMEM((MOBA_HEADS_PER_STEP, nbp, A_HEAD_DIM), F32), pltpu.VMEM((MOBA_HEADS_PER_STEP, blk, LANES), F32),
                        pltpu.VMEM((MOBA_HEADS_PER_STEP, blk, 2 * A_HEAD_DIM), F32)],
        compiler_params=_params(3),
        name="moba_attention",
    )(rel_bias, p_a, p_a, p_a, p_a, bias_tiles)


def _cross_kernel(q_ref, k_ref, v_ref, z_ref, o_ref):
    hd = C_HEAD_DIM
    heads = [slice(h * hd, (h + 1) * hd) for h in range(C_HEADS)]
    s = _each(lambda hl: _mm(q_ref[:, hl], k_ref[:, hl], NT) * (hd ** -0.5), heads)
    m = _each(lambda x: jnp.max(x, axis=-1, keepdims=True), s)
    p = _each(lambda x, mx: jnp.exp(x - mx), s, m)
    l = _each(lambda x: jnp.sum(x, axis=-1, keepdims=True), p)
    y = _each(lambda x, hl: jnp.dot(x.astype(BF16), v_ref[:, hl], preferred_element_type=F32), p, heads)
    for hl, yh, lh in zip(heads, y, l):
        o_ref[:, hl] = (yh / lh * _silu(z_ref[:, hl].astype(F32))).astype(o_ref.dtype)


def _cross(p_c, kv, bsz, seq, mem_len, tq):
    nt = seq // tq
    return pl.pallas_call(
        _cross_kernel,
        grid=(bsz, nt),
        in_specs=[
            pl.BlockSpec((tq, C_WIDTH), lambda b, i: (b * nt + i, 0)),
            pl.BlockSpec((mem_len, C_WIDTH), lambda b, i: (b, 0)),
            pl.BlockSpec((mem_len, C_WIDTH), lambda b, i: (b, 1)),
            pl.BlockSpec((tq, C_WIDTH), lambda b, i: (b * nt + i, 1)),
        ],
        out_specs=pl.BlockSpec((tq, C_WIDTH), lambda b, i: (b * nt + i, 0)),
        out_shape=jax.ShapeDtypeStruct((bsz * seq, C_WIDTH), BF16),
        compiler_params=_params(2),
        name="memory_attention",
    )(p_c, kv, kv, p_c)


class _WkvConsts:
    def __init__(self):
        c, g, hd = WKV_CHUNK, WKV_GROUP, B_HEAD_DIM
        row = lax.broadcasted_iota(jnp.int32, (g, g), 0)
        col = lax.broadcasted_iota(jnp.int32, (g, g), 1)
        self.same_head = jnp.where((row // hd) == (col // hd), 1.0, 0.0)
        self.ones_bd = self.same_head.astype(BF16)
        t = lax.broadcasted_iota(jnp.int32, (c, g), 0)
        s = lax.broadcasted_iota(jnp.int32, (c, g), 1) % hd
        self.strict = s < t
        self.incl = s <= t
        self.strict_incl = jnp.concatenate([self.strict, self.incl], axis=0)
        self.eye = jnp.where(s == t, 1.0, 0.0).astype(F32)
        tr = lax.broadcasted_iota(jnp.int32, (c, c), 0)
        tc = lax.broadcasted_iota(jnp.int32, (c, c), 1)
        self.lower = jnp.where(tc <= tr, 1.0, 0.0).astype(BF16)

    def bd(self, p):
        return jnp.concatenate([p] * WKV_HEADS_PER_GROUP, axis=0) * self.ones_bd


def _wkv_prepare(rm, kmod, vm, avec, bvec, logdec, k):
    c = WKV_CHUNK
    cat = lambda u, w: jnp.concatenate([u, w], axis=0)
    cl = _each(lambda x: jnp.dot(k.lower, _bf(x), preferred_element_type=F32), logdec)
    cl_last = _each(lambda x: x[c - 1:c, :], cl)
    rt = _each(lambda r, x: _bf(r * jnp.exp(x)), rm, cl)
    at = _each(lambda a, x, ld: _bf(a * jnp.exp(x - ld)), avec, cl, logdec)
    e_neg = _each(lambda x: jnp.exp(-x), cl)
    bt = _each(lambda b, e: k.bd(_bf(b * e)), bvec, e_neg)
    kt = _each(lambda kk, e: k.bd(_bf(kk * e)), kmod, e_neg)
    decay = _each(jnp.exp, cl_last)
    e_rem = _each(lambda d, e: d * e, decay, e_neg)
    bkh = _each(lambda b, kk, e: cat(_bf(b * e), _bf(kk * e)), bvec, kmod, e_rem)
    vb = _each(_bf, vm)

    lhs = _each(cat, at, rt)
    ab = _each(lambda x, y: _mm(x, y, NT), lhs, bt)
    ak = _each(lambda x, y: _mm(x, y, NT), lhs, kt)
    a_ab = _each(lambda x: jnp.where(k.strict, x[:c], 0.0), ab)
    a_rb = _each(lambda x: _bf(jnp.where(k.incl, x[c:], 0.0)), ab)
    a_k = _each(lambda x: _bf(jnp.where(k.strict_incl, x, 0.0)), ak)

    inv = _each(lambda x: k.eye + x, a_ab)
    pw = _each(_bf, a_ab)
    pw = _each(lambda p: _bf(_mm(p, k.bd(p), NN)), pw)
    for _ in range(int(math.log2(c)) - 2):
        tp = _each(lambda i, p: _mm(cat(_bf(i), p), k.bd(p), NN), inv, pw)
        inv = _each(lambda i, t: i + t[:c], inv, tp)
        pw = _each(lambda t: _bf(t[c:]), tp)
    inv = _each(lambda i, p: _bf(i + _mm(_bf(i), k.bd(p), NN)), inv, pw)

    av = _each(lambda x, v: _mm(x, k.bd(v), NN), a_k, vb)
    w = _each(lambda i, a: _mm(i, k.bd(a), NN), inv, at)
    u0 = _each(lambda i, x: _mm(i, k.bd(_bf(x[:c])), NN), inv, av)
    wr = _each(lambda ww, r: cat(_bf(ww), r), w, rt)
    y0 = _each(lambda x: x[c:], av)
    return list(zip(wr, u0, y0, a_rb, vb, bkh, decay))


def _wkv_apply(prep, state, k):
    c = WKV_CHUNK
    wr, u0, y0, a_rb, vb, bkh, decay = (list(x) for x in zip(*prep))
    uy = _each(lambda x, s: _mm(x, _bf(s), NT), wr, state)
    ub = _each(lambda x, y: _bf(x[:c] + y), uy, u0)
    y = _each(lambda x, a, uu, z: x[c:] + _mm(a, k.bd(uu), NN) + z, uy, a_rb, ub, y0)
    upd = _each(lambda uu, v, bk: _mm(jnp.concatenate([uu, v], axis=0), bk, TN), ub, vb, bkh)
    new_state = _each(lambda s, dc, d: s * dc + d * k.same_head, state, decay, upd)
    return y, new_state


def _rwkv_kernel(r_ref, k_ref, v_ref, lwla_ref, z_ref, w0_ref, a0_ref, kk_ref, ka_ref, rk_ref, lnw_ref, lnb_ref,
                 wd_ref, wa_ref, o_ref, state_ref):
    c, g = WKV_CHUNK, WKV_GROUP
    n_groups = r_ref.shape[1] // g
    n_chunks = WKV_CHUNKS_PER_STEP

    @pl.when(pl.program_id(2) == 0)
    def _():
        state_ref[...] = jnp.zeros_like(state_ref)

    k = _WkvConsts()
    wd = _bf(wd_ref[...])
    wa = _bf(wa_ref[...])
    inv_hd = 1.0 / B_HEAD_DIM
    groups = [slice(gi * g, (gi + 1) * g) for gi in range(n_groups)]
    split = lambda x: [x[ch * c:(ch + 1) * c, ln] for ch in range(n_chunks) for ln in groups]

    def segsums(xs):
        s = jnp.dot(jnp.concatenate(_each(_bf, xs), axis=0), k.ones_bd, preferred_element_type=F32)
        return [s[i * c:(i + 1) * c] for i in range(len(xs))]

    def step(ci, carry):
        rows = pl.ds(pl.multiple_of(ci * (n_chunks * c), n_chunks * c), n_chunks * c)
        rm, km, vm = r_ref[rows, :], k_ref[rows, :], v_ref[rows, :]
        lwla = lwla_ref[rows, :]
        lw = _bf(jnp.tanh(lwla[:, :LORA_PAD]))
        la = _bf(lwla[:, LORA_PAD:])
        logdec = -math.exp(-0.5) * _sigmoid(w0_ref[...] + jnp.dot(lw, wd, preferred_element_type=F32))
        a_lr = _sigmoid(a0_ref[...] + jnp.dot(la, wa, preferred_element_type=F32))
        kmod = km * (1.0 + (a_lr - 1.0) * ka_ref[...])
        kk = split(km * kk_ref[...])
        n = len(kk)
        sums = segsums(_each(lambda x: x * x, kk) + split(rm * kmod * rk_ref[...]))
        kk = _each(lambda x, ss: x * jnp.minimum(lax.rsqrt(ss), 1e12), kk, sums[:n])
        bonus = _each(lambda s, v: s * v, sums[n:], split(vm))
        prep = _wkv_prepare(split(rm), split(kmod), split(vm), _each(lambda x: -x, kk),
                            _each(lambda x, a: x * a, kk, split(a_lr)), split(logdec), k)
        state = [state_ref[gi] for gi in range(n_groups)]
        y = []
        for ch in range(n_chunks):
            y_ch, state = _wkv_apply(prep[ch * n_groups:(ch + 1) * n_groups], state, k)
            y += y_ch
        for gi in range(n_groups):
            state_ref[gi] = state[gi]
        yc = _each(lambda x, s: x - s * inv_hd, y, segsums(y))
        var = _each(lambda s: s * inv_hd, segsums(_each(lambda x: x * x, yc)))
        gate = _silu(z_ref[rows, :].astype(F32))
        for ch in range(n_chunks):
            out_rows = pl.ds(pl.multiple_of(ci * (n_chunks * c) + ch * c, c), c)
            for gi, ln in enumerate(groups):
                i = ch * n_groups + gi
                yn = yc[i] * lax.rsqrt(var[i] + LNX_EPS) * lnw_ref[:, ln] + lnb_ref[:, ln]
                o_ref[out_rows, ln] = ((yn + bonus[i]) * gate[ch * c:(ch + 1) * c, ln]).astype(o_ref.dtype)
        return carry

    lax.fori_loop(0, r_ref.shape[0] // (n_chunks * c), step, 0)


def _rwkv(p_b, p_lora, p_zb, vecs, wd2, wa2, bsz, seq, tt):
    g = WKV_GROUP
    gw = WKV_GROUPS_PER_STEP * g
    ns = B_WIDTH // gw
    nt = seq // tt
    row = lambda b, gi, t: b * nt + t
    vec_spec = pl.BlockSpec((1, gw), lambda b, gi, t: (0, gi))
    lora_w_spec = pl.BlockSpec((LORA_PAD, gw), lambda b, gi, t: (0, gi))
    return pl.pallas_call(
        _rwkv_kernel,
        grid=(bsz, ns, nt),
        in_specs=[
            pl.BlockSpec((tt, gw), lambda b, gi, t: (row(b, gi, t), gi)),
            pl.BlockSpec((tt, gw), lambda b, gi, t: (row(b, gi, t), ns + gi)),
            pl.BlockSpec((tt, gw), lambda b, gi, t: (row(b, gi, t), 2 * ns + gi)),
            pl.BlockSpec((tt, 2 * LORA_PAD), lambda b, gi, t: (row(b, gi, t), 0)),
            pl.BlockSpec((tt, gw), lambda b, gi, t: (row(b, gi, t), gi)),
        ] + [vec_spec] * len(vecs) + [lora_w_spec, lora_w_spec],
        out_specs=pl.BlockSpec((tt, gw), lambda b, gi, t: (row(b, gi, t), gi)),
        out_shape=jax.ShapeDtypeStruct((bsz * seq, B_WIDTH), BF16),
        scratch_shapes=[pltpu.VMEM((WKV_GROUPS_PER_STEP, g, g), F32)],
        compiler_params=_params(3),
        name="rwkv7_time_mix",
    )(p_b, p_b, p_b, p_lora, p_zb, *vecs, wd2, wa2)


def _merge_kernel(ya_ref, yb_ref, yc_ref, wa_ref, wb_ref, wc_ref, ga_ref, gb_ref, gc_ref, o_ref):
    def branch(y_ref, w_ref, g_ref):
        return _sigmoid(g_ref[...].astype(F32)) * jnp.dot(y_ref[...], w_ref[...], preferred_element_type=F32)

    o_ref[...] = (branch(ya_ref, wa_ref, ga_ref) + branch(yb_ref, wb_ref, gb_ref) + branch(yc_ref, wc_ref, gc_ref)).astype(o_ref.dtype)


def _merge(ya, yb, yc, wa, wb, wc, p_c, gate_col0, d_model, tm, tn):
    m = ya.shape[0]
    nj = d_model // tn
    j0 = gate_col0 // tn
    y_spec = lambda width: pl.BlockSpec((tm, width), lambda j, i: (i, 0))
    w_spec = lambda width: pl.BlockSpec((width, tn), lambda j, i: (0, j))
    g_spec = lambda br: pl.BlockSpec((tm, tn), lambda j, i: (i, j0 + br * nj + j))
    return pl.pallas_call(
        _merge_kernel,
        grid=(nj, m // tm),
        in_specs=[y_spec(A_WIDTH), y_spec(B_WIDTH), y_spec(C_WIDTH), w_spec(A_WIDTH), w_spec(B_WIDTH), w_spec(C_WIDTH),
                  g_spec(0), g_spec(1), g_spec(2)],
        out_specs=pl.BlockSpec((tm, tn), lambda j, i: (i, j)),
        out_shape=jax.ShapeDtypeStruct((m, d_model), BF16),
        compiler_params=_params(2, VMEM_LIMIT_PROJ),
        name="gated_merge",
    )(ya, yb, yc, wa, wb, wc, p_c, p_c, p_c)


def _out_kernel(m_ref, w_ref, x_ref, g_ref, o_ref, *, final_norm):
    y = x_ref[...] + jnp.dot(m_ref[...], w_ref[...], preferred_element_type=F32)
    if final_norm:
        y = y * lax.rsqrt(jnp.mean(y * y, axis=-1, keepdims=True) + RMS_EPS) * g_ref[...]
    o_ref[...] = y


def _out_proj(merged, w_out, x2d, g, final_norm, tm):
    m, d = x2d.shape
    return pl.pallas_call(
        functools.partial(_out_kernel, final_norm=final_norm),
        grid=(m // tm,),
        in_specs=[pl.BlockSpec((tm, d), lambda i: (i, 0)), pl.BlockSpec((d, d), lambda i: (0, 0)),
                  pl.BlockSpec((tm, d), lambda i: (i, 0)), pl.BlockSpec((1, d), lambda i: (0, 0))],
        out_specs=pl.BlockSpec((tm, d), lambda i: (i, 0)),
        out_shape=jax.ShapeDtypeStruct((m, d), F32),
        compiler_params=_params(1),
        name="out_proj",
    )(merged, w_out, x2d, g.reshape(1, d))


def _pick_tile(n, candidates):
    for t in candidates:
        if n % t == 0:
            return t
    raise ValueError(f"no tile for extent {n}")


class _Tiles(NamedTuple):
    norm_rows: int
    proj_rows: int
    proj_cols: int
    proj_cols_b: int
    mem_rows: int
    cross_rows: int
    wkv_rows: int
    merge_rows: int
    merge_cols: int
    out_rows: int


def _tiles(m, seq, mem_rows):
    return _Tiles(
        norm_rows=_pick_tile(m, (1024, 512, 256)),
        proj_rows=_pick_tile(seq, (2048, 1024, 512, 256)),
        proj_cols=1024,
        proj_cols_b=768,
        mem_rows=_pick_tile(mem_rows, (1024, 512, 256)),
        cross_rows=_pick_tile(seq, (1024, 512, 256)),
        wkv_rows=_pick_tile(seq, (512, 256, 128, 64)),
        merge_rows=_pick_tile(m, (1024, 512, 256)),
        merge_cols=1024,
        out_rows=_pick_tile(m, (512, 256)),
    )


def _pad_cols(w, width):
    return jnp.pad(w, ((0, 0), (0, width - w.shape[1])))


def _layer(x2d, mem2d, bsz, seq, mem_len, bias_tiles, rel_bias, norm_g, mem_norm_g, w_in, rw, w_mem_kv, w_proj_a, w_proj_b,
           w_proj_c, w_out, final_g):
    m, d = x2d.shape
    t = _tiles(m, seq, mem2d.shape[0])

    c_z_a = 3 * A_WIDTH
    c_rkv_b = c_z_a + A_WIDTH
    c_z_b = c_rkv_b + 3 * B_WIDTH
    c_lw = c_z_b + B_WIDTH
    c_la = c_lw + LORA
    c_q_c = c_la + LORA

    h = _rmsnorm(x2d, norm_g, BF16, t.norm_rows)
    ones = lambda n: jnp.ones((1, n), F32)
    wt = w_in.T
    scale_a = jnp.concatenate([jnp.full((1, A_WIDTH), MOBA_Q_SCALE, F32), ones(3 * A_WIDTH)], axis=1)
    p_a = _matmul_wt(h, wt, 0, 4 * A_WIDTH, scale_a, BF16, t.proj_rows, t.proj_cols, "in_proj_a")
    (mu_r, mu_k, mu_v, mu_w, mu_a, w0, w_decay2, a0, w_aaa2, k_k, k_a, r_k, lnx_w, lnx_b) = rw
    vec = lambda v: v.reshape(1, -1)
    p_b = _matmul_wt(h, wt, c_rkv_b, 3 * B_WIDTH, jnp.concatenate([vec(mu_r), vec(mu_k), vec(mu_v)], axis=1), F32,
                     t.proj_rows, t.proj_cols_b, "in_proj_b", seq_rows=seq)
    p_zb = _matmul_wt(h, wt, c_z_b, B_WIDTH, ones(B_WIDTH), BF16, t.proj_rows, t.proj_cols_b, "in_proj_zb")
    pad_rows = lambda w: jnp.pad(w, ((0, LORA_PAD - LORA), (0, 0)))
    wt_lora = jnp.concatenate([pad_rows(wt[c_lw:c_la]), pad_rows(wt[c_la:c_q_c])], axis=0)
    mu_wa = jnp.concatenate([_pad_cols(vec(mu_w), LORA_PAD), _pad_cols(vec(mu_a), LORA_PAD)], axis=1)
    p_lora = _matmul_wt(h, wt_lora, 0, 2 * LORA_PAD, mu_wa, F32, t.proj_rows, 2 * LORA_PAD, "in_proj_lora", seq_rows=seq)
    n_c = 2 * C_WIDTH + N_BRANCHES * d
    p_c = _matmul_wt(h, wt, c_q_c, n_c, ones(n_c), BF16, t.proj_rows, t.proj_cols, "in_proj_c")

    mem_n = _rmsnorm(mem2d, mem_norm_g, BF16, t.mem_rows)
    kv = _matmul(mem_n, w_mem_kv.astype(BF16), BF16, t.mem_rows, t.proj_cols, "mem_kv")

    ya = _moba(p_a, rel_bias, bias_tiles, bsz, seq)
    yc = _cross(p_c, kv, bsz, seq, mem_len, t.cross_rows)

    vecs = (vec(w0), vec(a0), vec(k_k), vec(k_a), vec(r_k), vec(lnx_w), vec(lnx_b))
    yb = _rwkv(p_b, p_lora, p_zb, vecs, pad_rows(w_decay2), pad_rows(w_aaa2), bsz, seq, t.wkv_rows)

    merged = _merge(ya, yb, yc, w_proj_a.astype(BF16), w_proj_b.astype(BF16), w_proj_c.astype(BF16), p_c, 2 * C_WIDTH, d,
                    t.merge_rows, t.merge_cols)
    g = final_g if final_g is not None else jnp.ones((d,), F32)
    return _out_proj(merged, w_out.astype(BF16), x2d, g, final_g is not None, t.out_rows)


def kernel(x, mem, rel_bias, norm_g, mem_norm_g, w_in, rw_mu_r, rw_mu_k, rw_mu_v, rw_mu_w, rw_mu_a, rw_w0, rw_w_decay2, rw_a0, rw_w_aaa2, rw_k_k, rw_k_a, rw_r_k, rw_lnx_w, rw_lnx_b, w_mem_kv, w_proj_a, w_proj_b, w_proj_c, w_out, final_norm_g):
    bsz, seq, d = x.shape
    mem_len = mem.shape[1]
    depth = norm_g.shape[0]
    x2d = x.reshape(bsz * seq, d)
    mem2d = mem.reshape(bsz * mem_len, d)
    bias_tiles = _bias_tiles(rel_bias)
    for l in range(depth):
        rw = (rw_mu_r[l], rw_mu_k[l], rw_mu_v[l], rw_mu_w[l], rw_mu_a[l], rw_w0[l], rw_w_decay2[l], rw_a0[l], rw_w_aaa2[l],
              rw_k_k[l], rw_k_a[l], rw_r_k[l], rw_lnx_w[l], rw_lnx_b[l])
        x2d = _layer(x2d, mem2d, bsz, seq, mem_len, bias_tiles, rel_bias, norm_g[l], mem_norm_g[l], w_in[l], rw, w_mem_kv[l],
                     w_proj_a[l], w_proj_b[l], w_proj_c[l], w_out[l], final_norm_g if l == depth - 1 else None)
    return x2d.reshape(bsz, seq, d)
```

```python
import functools
import math
from typing import NamedTuple

import jax
import jax.numpy as jnp
from jax import lax
from jax.experimental import pallas as pl
from jax.experimental.pallas import tpu as pltpu

F32 = jnp.float32
BF16 = jnp.bfloat16
LANES = 128
BF16_SUBLANES = 16

RMS_EPS = 1e-6

A_HEADS = 12
A_HEAD_DIM = 128
A_WIDTH = A_HEADS * A_HEAD_DIM
MOBA_BLOCK = 256
MOBA_TOPK = 3
MOBA_HEADS_PER_STEP = 6
MOBA_FAR_GROUP = 4
MOBA_Q_SCALE = A_HEAD_DIM ** -0.5 * math.log2(math.e)
REL_BUCKETS = 32
REL_MAX_DIST = 128

B_HEAD_DIM = 64
B_WIDTH = 1536
LORA = 96
LORA_PAD = LANES
LNX_EPS = 64e-5
WKV_CHUNK = 64
WKV_GROUP = 256
WKV_HEADS_PER_GROUP = WKV_GROUP // B_HEAD_DIM
WKV_GROUPS_PER_STEP = 6
WKV_CHUNKS_PER_STEP = 4

C_HEADS = 4
C_HEAD_DIM = 256
C_WIDTH = C_HEADS * C_HEAD_DIM

N_BRANCHES = 3

VMEM_LIMIT = 48 * 1024 * 1024
VMEM_LIMIT_PROJ = 58 * 1024 * 1024

NN = (((1,), (0,)), ((), ()))
NT = (((1,), (1,)), ((), ()))
TN = (((0,), (0,)), ((), ()))

MASKED = -1e30
LOG2E = math.log2(math.e)


def _params(n_axes, vmem_limit=VMEM_LIMIT):
    return pltpu.CompilerParams(dimension_semantics=("arbitrary",) * n_axes, vmem_limit_bytes=vmem_limit)


def _sigmoid(x):
    return 0.5 * jnp.tanh(0.5 * x) + 0.5


def _silu(z):
    return z * _sigmoid(z)


def _bf(x):
    return x.astype(BF16)


def _mm(a, b, dn):
    return lax.dot_general(a, b, dn, preferred_element_type=F32)


def _each(f, *lists):
    return [f(*args) for args in zip(*lists)]


def _rmsnorm_kernel(x_ref, g_ref, o_ref):
    x = x_ref[...].astype(F32)
    y = x * lax.rsqrt(jnp.mean(x * x, axis=-1, keepdims=True) + RMS_EPS)
    o_ref[...] = (y * g_ref[...]).astype(o_ref.dtype)


def _rmsnorm(x2d, g, out_dtype, tm):
    m, d = x2d.shape
    return pl.pallas_call(
        _rmsnorm_kernel,
        grid=(m // tm,),
        in_specs=[pl.BlockSpec((tm, d), lambda i: (i, 0)), pl.BlockSpec((1, d), lambda i: (0, 0))],
        out_specs=pl.BlockSpec((tm, d), lambda i: (i, 0)),
        out_shape=jax.ShapeDtypeStruct((m, d), out_dtype),
        compiler_params=_params(1),
        name="rmsnorm",
    )(x2d, g.reshape(1, d))


def _matmul_kernel(a_ref, w_ref, o_ref):
    o_ref[...] = jnp.dot(a_ref[...], w_ref[...], preferred_element_type=F32).astype(o_ref.dtype)


def _matmul(a, w, out_dtype, tm, tn, name):
    m, k = a.shape
    n = w.shape[1]
    return pl.pallas_call(
        _matmul_kernel,
        grid=(n // tn, m // tm),
        in_specs=[pl.BlockSpec((tm, k), lambda j, i: (i, 0)), pl.BlockSpec((k, tn), lambda j, i: (0, j))],
        out_specs=pl.BlockSpec((tm, tn), lambda j, i: (i, j)),
        out_shape=jax.ShapeDtypeStruct((m, n), out_dtype),
        compiler_params=_params(2),
        name=name,
    )(a, w)


def _matmul_wt_kernel(a_ref, wt_ref, s_ref, o_ref, wb_ref, *carry_ref, seq_rows):
    i = pl.program_id(1)

    @pl.when(i == 0)
    def _():
        wb_ref[...] = wt_ref[...].astype(BF16)
        for ref in carry_ref:
            ref[...] = jnp.zeros_like(ref)

    x = lax.dot_general(a_ref[...], wb_ref[...], NT, preferred_element_type=F32)
    if seq_rows is None:
        o_ref[...] = (x * s_ref[...]).astype(o_ref.dtype)
    else:
        tm = x.shape[0]
        last, = carry_ref
        first_row = lax.broadcasted_iota(jnp.int32, x.shape, 0) == 0
        carried = jnp.where((i * tm) % seq_rows == 0, 0.0, last[0:1, :])
        prev = jnp.where(first_row, carried, pltpu.roll(x, 1, 0))
        last[0:1, :] = x[tm - 1:tm, :]
        o_ref[...] = (x + (prev - x) * s_ref[...]).astype(o_ref.dtype)


def _matmul_wt(a, wt, row0, n, col_vec, out_dtype, tm, tn, name, seq_rows=None):
    m, k = a.shape
    scratch = [pltpu.VMEM((tn, k), BF16)]
    if seq_rows is not None:
        assert seq_rows % tm == 0
        scratch.append(pltpu.VMEM((8, tn), F32))
    return pl.pallas_call(
        functools.partial(_matmul_wt_kernel, seq_rows=seq_rows),
        grid=(n // tn, m // tm),
        in_specs=[pl.BlockSpec((tm, k), lambda j, i: (i, 0)),
                  pl.BlockSpec((pl.Element(tn), pl.Element(k)), lambda j, i: (pl.multiple_of(row0 + j * tn, 8), 0)),
                  pl.BlockSpec((1, tn), lambda j, i: (0, j))],
        out_specs=pl.BlockSpec((tm, tn), lambda j, i: (i, j)),
        out_shape=jax.ShapeDtypeStruct((m, n), out_dtype),
        scratch_shapes=scratch,
        compiler_params=_params(2, VMEM_LIMIT_PROJ),
        name=name,
    )(a, wt, col_vec)


def _norm_proj_kernel(x_ref, g_ref, wt_ref, mu_ref, h_ref, o_ref, wb_ref, last, *, seq_rows):
    i = pl.program_id(0)

    @pl.when(i == 0)
    def _():
        wb_ref[...] = wt_ref[...].astype(BF16)
        last[...] = jnp.zeros_like(last)

    x = x_ref[...]
    h = (x * lax.rsqrt(jnp.mean(x * x, axis=-1, keepdims=True) + RMS_EPS) * g_ref[...]).astype(BF16)
    h_ref[...] = h
    y = lax.dot_general(h, wb_ref[...], NT, preferred_element_type=F32)
    tm = y.shape[0]
    first_row = lax.broadcasted_iota(jnp.int32, y.shape, 0) == 0
    carried = jnp.where((i * tm) % seq_rows == 0, 0.0, last[0:1, :])
    prev = jnp.where(first_row, carried, pltpu.roll(y, 1, 0))
    last[0:1, :] = y[tm - 1:tm, :]
    o_ref[...] = y + (prev - y) * mu_ref[...]


def _norm_proj(x2d, g, wt, mu, tm, seq_rows):
    m, d = x2d.shape
    n = wt.shape[0]
    assert seq_rows % tm == 0
    return pl.pallas_call(
        functools.partial(_norm_proj_kernel, seq_rows=seq_rows),
        grid=(m // tm,),
        in_specs=[pl.BlockSpec((tm, d), lambda i: (i, 0)), pl.BlockSpec((1, d), lambda i: (0, 0)),
                  pl.BlockSpec((n, d), lambda i: (0, 0)), pl.BlockSpec((1, n), lambda i: (0, 0))],
        out_specs=[pl.BlockSpec((tm, d), lambda i: (i, 0)), pl.BlockSpec((tm, n), lambda i: (i, 0))],
        out_shape=[jax.ShapeDtypeStruct((m, d), BF16), jax.ShapeDtypeStruct((m, n), F32)],
        scratch_shapes=[pltpu.VMEM((n, d), BF16), pltpu.VMEM((8, n), F32)],
        compiler_params=_params(1),
        name="rmsnorm_in_proj_lora",
    )(x2d, g.reshape(1, d), wt, mu)


def _t5_bucket(dist):
    n = jnp.maximum(dist, 0)
    max_exact = REL_BUCKETS // 2
    nf = jnp.maximum(n, max_exact).astype(F32)
    large = max_exact + (jnp.log(nf / max_exact) / math.log(REL_MAX_DIST / max_exact) * (REL_BUCKETS - max_exact)).astype(jnp.int32)
    large = jnp.minimum(large, REL_BUCKETS - 1)
    return jnp.where(n < max_exact, n, large)


def _bias_kernel(rel_ref, bucket_ref, o_ref):
    h = pl.program_id(0)
    bucket = bucket_ref[...]
    acc = jnp.zeros(bucket.shape, F32)
    for b in range(REL_BUCKETS):
        acc = jnp.where(bucket == b, rel_ref[b, h], acc)
    qpos = lax.broadcasted_iota(jnp.int32, bucket.shape, 0) + MOBA_BLOCK
    kpos = lax.broadcasted_iota(jnp.int32, bucket.shape, 1)
    o_ref[0] = jnp.where(kpos <= qpos, acc * LOG2E, MASKED)


def _bias_tiles(rel_bias):
    blk = MOBA_BLOCK
    qpos = lax.broadcasted_iota(jnp.int32, (blk, 2 * blk), 0) + blk
    kpos = lax.broadcasted_iota(jnp.int32, (blk, 2 * blk), 1)
    bucket = _t5_bucket(qpos - kpos)
    return pl.pallas_call(
        _bias_kernel,
        grid=(A_HEADS,),
        in_specs=[pl.BlockSpec(memory_space=pltpu.SMEM), pl.BlockSpec((blk, 2 * blk), lambda h: (0, 0))],
        out_specs=pl.BlockSpec((1, blk, 2 * blk), lambda h: (h, 0, 0)),
        out_shape=jax.ShapeDtypeStruct((A_HEADS, blk, 2 * blk), F32),
        compiler_params=_params(1),
        name="t5_bias",
    )(rel_bias, bucket)


def _moba_kernel(rel_ref, q_ref, k_ref, v_ref, z_ref, bias_ref, o_ref, kmean_ref, m_ref, acc_ref):
    hb = pl.program_id(1)
    qi = pl.program_id(2)
    blk, hd = MOBA_BLOCK, A_HEAD_DIM
    n_heads = q_ref.shape[1] // hd
    nb = k_ref.shape[0] // blk
    nbp = kmean_ref.shape[1]

    @pl.when(qi == 0)
    def _():
        kmean_ref[...] = jnp.zeros_like(kmean_ref)
        for j in range(nb):
            mean_j = jnp.sum(k_ref[j * blk:(j + 1) * blk, :].astype(F32), axis=0, keepdims=True) * (1.0 / blk)
            for hh in range(n_heads):
                kmean_ref[hh, j:j + 1, :] = mean_j[:, hh * hd:(hh + 1) * hd]

    n_far = min(MOBA_FAR_GROUP, nb)
    never = LANES - 1
    blk_id = lax.broadcasted_iota(jnp.int32, (nbp, blk), 0)
    blk_f = blk_id.astype(F32)
    eye = jnp.where(lax.broadcasted_iota(jnp.int32, (nbp, LANES), 0) == lax.broadcasted_iota(jnp.int32, (nbp, LANES), 1), 1.0, 0.0).astype(BF16)
    j_prev = jnp.maximum(qi - 1, 0)
    own = pl.multiple_of(qi * blk, blk)
    prev = pl.multiple_of(j_prev * blk, blk)

    heads = [slice(hh * hd, (hh + 1) * hd) for hh in range(n_heads)]
    q = [q_ref[:, hl] for hl in heads]

    def gate_of(qh, hh):
        km = kmean_ref[hh]
        km_hi = km.astype(BF16)
        km_lo = (km - km_hi.astype(F32)).astype(BF16)
        return jnp.where(blk_id < qi, _mm(km_hi, qh, NT) + _mm(km_lo, qh, NT), -jnp.inf)

    g = _each(gate_of, q, range(n_heads))
    sel_t = [jnp.zeros((nbp, blk), F32) for _ in heads]
    for _ in range(MOBA_TOPK):
        gmax = _each(lambda x: jnp.max(x, axis=0, keepdims=True), g)
        first = _each(lambda x, mx: jnp.min(jnp.where(x == mx, blk_f, float(nbp)), axis=0, keepdims=True), g, gmax)
        pick = _each(lambda f, mx: (blk_f == f) & (mx > -jnp.inf), first, gmax)
        sel_t = _each(lambda p, s: jnp.where(p, 1.0, s), pick, sel_t)
        g = _each(lambda p, x: jnp.where(p, -jnp.inf, x), pick, g)
    sel = _each(lambda s: _mm(s.astype(BF16), eye, TN), sel_t)
    qm = _each(lambda qh, s: jnp.concatenate([qh, jnp.where(s > 0.0, 0.0, MASKED).astype(BF16)], axis=1), q, sel)

    lane = lax.broadcasted_iota(jnp.int32, (blk, LANES), 1)

    def masked_logits(keys, key_blocks):
        col = jnp.concatenate([jnp.where(lane == j, 1.0, 0.0).astype(BF16) for j in key_blocks], axis=0)
        return _each(lambda x, kk: _mm(x, jnp.concatenate([kk, col], axis=1), NT), qm, keys)

    def pv(p, values):
        ones = jnp.ones(values[0].shape, BF16)
        return _each(lambda x, v: jnp.dot(x.astype(BF16), jnp.concatenate([v, ones], axis=1), preferred_element_type=F32), p, values)

    cat0 = lambda ref, hl: jnp.concatenate([ref[pl.ds(prev, blk), hl], ref[pl.ds(own, blk), hl]], axis=0)
    raw = masked_logits([cat0(k_ref, hl) for hl in heads], [j_prev, -1])
    t = _each(lambda x, hh: x + bias_ref[hh], raw, range(n_heads))
    m = _each(lambda x: jnp.max(x, axis=-1, keepdims=True), t)
    p = _each(lambda x, mx: jnp.exp2(x - mx), t, m)
    acc = pv(p, [cat0(v_ref, hl) for hl in heads])
    for hh in range(n_heads):
        m_ref[hh] = jnp.broadcast_to(m[hh], (blk, LANES))
        acc_ref[hh] = acc[hh]

    bias_far = [rel_ref[REL_BUCKETS - 1, hb * n_heads + hh] * LOG2E for hh in range(n_heads)]

    def body(gi, carry):
        rows = pl.ds(pl.multiple_of(gi * (n_far * blk), n_far * blk), n_far * blk)
        blocks = [gi * n_far + j for j in range(n_far)]
        raw = masked_logits([k_ref[rows, hl] for hl in heads], [jnp.where(j < j_prev, j, never) for j in blocks])
        m = [m_ref[hh][:, :1] for hh in range(n_heads)]
        m_new = _each(lambda mx, x, b: jnp.maximum(mx, jnp.max(x, axis=-1, keepdims=True) + b), m, raw, bias_far)
        p = _each(lambda x, mn, b: jnp.exp2(x + (b - mn)), raw, m_new, bias_far)
        alpha = _each(lambda mx, mn: jnp.exp2(mx - mn), m, m_new)
        upd = pv(p, [v_ref[rows, hl] for hl in heads])
        for hh in range(n_heads):
            m_ref[hh] = jnp.broadcast_to(m_new[hh], (blk, LANES))
            acc_ref[hh] = alpha[hh] * acc_ref[hh] + upd[hh]
        return carry

    lax.fori_loop(0, (j_prev + n_far - 1) // n_far, body, 0)
    for hh, hl in enumerate(heads):
        y = acc_ref[hh, :, :hd] / acc_ref[hh, :, hd:]
        o_ref[:, hl] = (y * _silu(z_ref[:, hl].astype(F32))).astype(o_ref.dtype)


def _moba(p_a, rel_bias, bias_tiles, bsz, seq):
    blk = MOBA_BLOCK
    nq = seq // blk
    hw = MOBA_HEADS_PER_STEP * A_HEAD_DIM
    ns = A_WIDTH // hw
    nbp = -(-nq // BF16_SUBLANES) * BF16_SUBLANES
    assert nbp < LANES
    return pl.pallas_call(
        _moba_kernel,
        grid=(bsz, ns, nq),
        in_specs=[
            pl.BlockSpec(memory_space=pltpu.SMEM),
            pl.BlockSpec((blk, hw), lambda b, h, i: (b * nq + i, h)),
            pl.BlockSpec((seq, hw), lambda b, h, i: (b, ns + h)),
            pl.BlockSpec((seq, hw), lambda b, h, i: (b, 2 * ns + h)),
            pl.BlockSpec((blk, hw), lambda b, h, i: (b * nq + i, 3 * ns + h)),
            pl.BlockSpec((MOBA_HEADS_PER_STEP, blk, 2 * blk), lambda b, h, i: (h, 0, 0)),
        ],
        out_specs=pl.BlockSpec((blk, hw), lambda b, h, i: (b * nq + i, h)),
        out_shape=jax.ShapeDtypeStruct((bsz * seq, A_WIDTH), BF16),
        scratch_shapes=[pltpu.VMEM((MOBA_HEADS_PER_STEP, nbp, A_HEAD_DIM), F32), pltpu.VMEM((MOBA_HEADS_PER_STEP, blk, LANES), F32),
                        pltpu.VMEM((MOBA_HEADS_PER_STEP, blk, 2 * A_HEAD_DIM), F32)],
        compiler_params=_params(3),
        name="moba_attention",
    )(rel_bias, p_a, p_a, p_a, p_a, bias_tiles)


def _cross_kernel(q_ref, k_ref, v_ref, z_ref, o_ref):
    hd = C_HEAD_DIM
    heads = [slice(h * hd, (h + 1) * hd) for h in range(C_HEADS)]
    s = _each(lambda hl: _mm(q_ref[:, hl], k_ref[:, hl], NT) * (hd ** -0.5), heads)
    m = _each(lambda x: jnp.max(x, axis=-1, keepdims=True), s)
    p = _each(lambda x, mx: jnp.exp(x - mx), s, m)
    l = _each(lambda x: jnp.sum(x, axis=-1, keepdims=True), p)
    y = _each(lambda x, hl: jnp.dot(x.astype(BF16), v_ref[:, hl], preferred_element_type=F32), p, heads)
    for hl, yh, lh in zip(heads, y, l):
        o_ref[:, hl] = (yh / lh * _silu(z_ref[:, hl].astype(F32))).astype(o_ref.dtype)


def _cross(p_c, kv, bsz, seq, mem_len, tq):
    nt = seq // tq
    return pl.pallas_call(
        _cross_kernel,
        grid=(bsz, nt),
        in_specs=[
            pl.BlockSpec((tq, C_WIDTH), lambda b, i: (b * nt + i, 0)),
            pl.BlockSpec((mem_len, C_WIDTH), lambda b, i: (b, 0)),
            pl.BlockSpec((mem_len, C_WIDTH), lambda b, i: (b, 1)),
            pl.BlockSpec((tq, C_WIDTH), lambda b, i: (b * nt + i, 1)),
        ],
        out_specs=pl.BlockSpec((tq, C_WIDTH), lambda b, i: (b * nt + i, 0)),
        out_shape=jax.ShapeDtypeStruct((bsz * seq, C_WIDTH), BF16),
        compiler_params=_params(2),
        name="memory_attention",
    )(p_c, kv, kv, p_c)


class _WkvConsts:
    def __init__(self):
        c, g, hd = WKV_CHUNK, WKV_GROUP, B_HEAD_DIM
        row = lax.broadcasted_iota(jnp.int32, (g, g), 0)
        col = lax.broadcasted_iota(jnp.int32, (g, g), 1)
        self.same_head = jnp.where((row // hd) == (col // hd), 1.0, 0.0)
        self.ones_bd = self.same_head.astype(BF16)
        t = lax.broadcasted_iota(jnp.int32, (c, g), 0)
        s = lax.broadcasted_iota(jnp.int32, (c, g), 1) % hd
        self.strict = s < t
        self.incl = s <= t
        self.strict_incl = jnp.concatenate([self.strict, self.incl], axis=0)
        self.eye = jnp.where(s == t, 1.0, 0.0).astype(F32)
        tr = lax.broadcasted_iota(jnp.int32, (c, c), 0)
        tc = lax.broadcasted_iota(jnp.int32, (c, c), 1)
        self.lower = jnp.where(tc <= tr, 1.0, 0.0).astype(BF16)

    def bd(self, p):
        return jnp.concatenate([p] * WKV_HEADS_PER_GROUP, axis=0) * self.ones_bd


def _wkv_prepare(rm, kmod, vm, avec, bvec, logdec, k):
    c = WKV_CHUNK
    cat = lambda u, w: jnp.concatenate([u, w], axis=0)
    cl = _each(lambda x: jnp.dot(k.lower, _bf(x), preferred_element_type=F32), logdec)
    cl_last = _each(lambda x: x[c - 1:c, :], cl)
    rt = _each(lambda r, x: _bf(r * jnp.exp(x)), rm, cl)
    at = _each(lambda a, x, ld: _bf(a * jnp.exp(x - ld)), avec, cl, logdec)
    e_neg = _each(lambda x: jnp.exp(-x), cl)
    bt = _each(lambda b, e: k.bd(_bf(b * e)), bvec, e_neg)
    kt = _each(lambda kk, e: k.bd(_bf(kk * e)), kmod, e_neg)
    decay = _each(jnp.exp, cl_last)
    e_rem = _each(lambda d, e: d * e, decay, e_neg)
    bkh = _each(lambda b, kk, e: cat(_bf(b * e), _bf(kk * e)), bvec, kmod, e_rem)
    vb = _each(_bf, vm)

    lhs = _each(cat, at, rt)
    ab = _each(lambda x, y: _mm(x, y, NT), lhs, bt)
    ak = _each(lambda x, y: _mm(x, y, NT), lhs, kt)
    a_ab = _each(lambda x: jnp.where(k.strict, x[:c], 0.0), ab)
    a_rb = _each(lambda x: _bf(jnp.where(k.incl, x[c:], 0.0)), ab)
    a_k = _each(lambda x: _bf(jnp.where(k.strict_incl, x, 0.0)), ak)

    inv = _each(lambda x: k.eye + x, a_ab)
    pw = _each(_bf, a_ab)
    pw = _each(lambda p: _bf(_mm(p, k.bd(p), NN)), pw)
    for _ in range(int(math.log2(c)) - 2):
        tp = _each(lambda i, p: _mm(cat(_bf(i), p), k.bd(p), NN), inv, pw)
        inv = _each(lambda i, t: i + t[:c], inv, tp)
        pw = _each(lambda t: _bf(t[c:]), tp)
    inv = _each(lambda i, p: _bf(i + _mm(_bf(i), k.bd(p), NN)), inv, pw)

    av = _each(lambda x, v: _mm(x, k.bd(v), NN), a_k, vb)
    w = _each(lambda i, a: _mm(i, k.bd(a), NN), inv, at)
    u0 = _each(lambda i, x: _mm(i, k.bd(_bf(x[:c])), NN), inv, av)
    wr = _each(lambda ww, r: cat(_bf(ww), r), w, rt)
    y0 = _each(lambda x: x[c:], av)
    return list(zip(wr, u0, y0, a_rb, vb, bkh, decay))


def _wkv_apply(prep, state, k):
    c = WKV_CHUNK
    wr, u0, y0, a_rb, vb, bkh, decay = (list(x) for x in zip(*prep))
    uy = _each(lambda x, s: _mm(x, _bf(s), NT), wr, state)
    ub = _each(lambda x, y: _bf(x[:c] + y), uy, u0)
    y = _each(lambda x, a, uu, z: x[c:] + _mm(a, k.bd(uu), NN) + z, uy, a_rb, ub, y0)
    upd = _each(lambda uu, v, bk: _mm(jnp.concatenate([uu, v], axis=0), bk, TN), ub, vb, bkh)
    new_state = _each(lambda s, dc, d: s * dc + d * k.same_head, state, decay, upd)
    return y, new_state


def _rwkv_kernel(r_ref, k_ref, v_ref, lwla_ref, z_ref, w0_ref, a0_ref, kk_ref, ka_ref, rk_ref, lnw_ref, lnb_ref,
                 wd_ref, wa_ref, o_ref, state_ref):
    c, g = WKV_CHUNK, WKV_GROUP
    n_groups = r_ref.shape[1] // g
    n_chunks = WKV_CHUNKS_PER_STEP

    @pl.when(pl.program_id(2) == 0)
    def _():
        state_ref[...] = jnp.zeros_like(state_ref)

    k = _WkvConsts()
    wd = _bf(wd_ref[...])
    wa = _bf(wa_ref[...])
    inv_hd = 1.0 / B_HEAD_DIM
    groups = [slice(gi * g, (gi + 1) * g) for gi in range(n_groups)]
    split = lambda x: [x[ch * c:(ch + 1) * c, ln] for ch in range(n_chunks) for ln in groups]

    def segsums(xs):
        s = jnp.dot(jnp.concatenate(_each(_bf, xs), axis=0), k.ones_bd, preferred_element_type=F32)
        return [s[i * c:(i + 1) * c] for i in range(len(xs))]

    def step(ci, carry):
        rows = pl.ds(pl.multiple_of(ci * (n_chunks * c), n_chunks * c), n_chunks * c)
        rm, km, vm = r_ref[rows, :], k_ref[rows, :], v_ref[rows, :]
        lwla = lwla_ref[rows, :]
        lw = _bf(jnp.tanh(lwla[:, :LORA_PAD]))
        la = _bf(lwla[:, LORA_PAD:])
        logdec = -math.exp(-0.5) * _sigmoid(w0_ref[...] + jnp.dot(lw, wd, preferred_element_type=F32))
        a_lr = _sigmoid(a0_ref[...] + jnp.dot(la, wa, preferred_element_type=F32))
        kmod = km * (1.0 + (a_lr - 1.0) * ka_ref[...])
        kk = split(km * kk_ref[...])
        n = len(kk)
        sums = segsums(_each(lambda x: x * x, kk) + split(rm * kmod * rk_ref[...]))
        kk = _each(lambda x, ss: x * jnp.minimum(lax.rsqrt(ss), 1e12), kk, sums[:n])
        bonus = _each(lambda s, v: s * v, sums[n:], split(vm))
        prep = _wkv_prepare(split(rm), split(kmod), split(vm), _each(lambda x: -x, kk),
                            _each(lambda x, a: x * a, kk, split(a_lr)), split(logdec), k)
        state = [state_ref[gi] for gi in range(n_groups)]
        y = []
        for ch in range(n_chunks):
            y_ch, state = _wkv_apply(prep[ch * n_groups:(ch + 1) * n_groups], state, k)
            y += y_ch
        for gi in range(n_groups):
            state_ref[gi] = state[gi]
        yc = _each(lambda x, s: x - s * inv_hd, y, segsums(y))
        var = _each(lambda s: s * inv_hd, segsums(_each(lambda x: x * x, yc)))
        gate = _silu(z_ref[rows, :].astype(F32))
        for ch in range(n_chunks):
            out_rows = pl.ds(pl.multiple_of(ci * (n_chunks * c) + ch * c, c), c)
            for gi, ln in enumerate(groups):
                i = ch * n_groups + gi
                yn = yc[i] * lax.rsqrt(var[i] + LNX_EPS) * lnw_ref[:, ln] + lnb_ref[:, ln]
                o_ref[out_rows, ln] = ((yn + bonus[i]) * gate[ch * c:(ch + 1) * c, ln]).astype(o_ref.dtype)
        return carry

    lax.fori_loop(0, r_ref.shape[0] // (n_chunks * c), step, 0)


def _rwkv(p_b, p_lora, p_zb, vecs, wd2, wa2, bsz, seq, tt):
    g = WKV_GROUP
    gw = WKV_GROUPS_PER_STEP * g
    ns = B_WIDTH // gw
    nt = seq // tt
    row = lambda b, gi, t: b * nt + t
    vec_spec = pl.BlockSpec((1, gw), lambda b, gi, t: (0, gi))
    lora_w_spec = pl.BlockSpec((LORA_PAD, gw), lambda b, gi, t: (0, gi))
    return pl.pallas_call(
        _rwkv_kernel,
        grid=(bsz, ns, nt),
        in_specs=[
            pl.BlockSpec((tt, gw), lambda b, gi, t: (row(b, gi, t), gi)),
            pl.BlockSpec((tt, gw), lambda b, gi, t: (row(b, gi, t), ns + gi)),
            pl.BlockSpec((tt, gw), lambda b, gi, t: (row(b, gi, t), 2 * ns + gi)),
            pl.BlockSpec((tt, 2 * LORA_PAD), lambda b, gi, t: (row(b, gi, t), 0)),
            pl.BlockSpec((tt, gw), lambda b, gi, t: (row(b, gi, t), gi)),
        ] + [vec_spec] * len(vecs) + [lora_w_spec, lora_w_spec],
        out_specs=pl.BlockSpec((tt, gw), lambda b, gi, t: (row(b, gi, t), gi)),
        out_shape=jax.ShapeDtypeStruct((bsz * seq, B_WIDTH), BF16),
        scratch_shapes=[pltpu.VMEM((WKV_GROUPS_PER_STEP, g, g), F32)],
        compiler_params=_params(3),
        name="rwkv7_time_mix",
    )(p_b, p_b, p_b, p_lora, p_zb, *vecs, wd2, wa2)


def _merge_kernel(ya_ref, yb_ref, yc_ref, wa_ref, wb_ref, wc_ref, ga_ref, gb_ref, gc_ref, o_ref):
    def branch(y_ref, w_ref, g_ref):
        return _sigmoid(g_ref[...].astype(F32)) * jnp.dot(y_ref[...], w_ref[...], preferred_element_type=F32)

    o_ref[...] = (branch(ya_ref, wa_ref, ga_ref) + branch(yb_ref, wb_ref, gb_ref) + branch(yc_ref, wc_ref, gc_ref)).astype(o_ref.dtype)


def _merge(ya, yb, yc, wa, wb, wc, p_c, gate_col0, d_model, tm, tn):
    m = ya.shape[0]
    nj = d_model // tn
    j0 = gate_col0 // tn
    y_spec = lambda width: pl.BlockSpec((tm, width), lambda j, i: (i, 0))
    w_spec = lambda width: pl.BlockSpec((width, tn), lambda j, i: (0, j))
    g_spec = lambda br: pl.BlockSpec((tm, tn), lambda j, i: (i, j0 + br * nj + j))
    return pl.pallas_call(
        _merge_kernel,
        grid=(nj, m // tm),
        in_specs=[y_spec(A_WIDTH), y_spec(B_WIDTH), y_spec(C_WIDTH), w_spec(A_WIDTH), w_spec(B_WIDTH), w_spec(C_WIDTH),
                  g_spec(0), g_spec(1), g_spec(2)],
        out_specs=pl.BlockSpec((tm, tn), lambda j, i: (i, j)),
        out_shape=jax.ShapeDtypeStruct((m, d_model), BF16),
        compiler_params=_params(2, VMEM_LIMIT_PROJ),
        name="gated_merge",
    )(ya, yb, yc, wa, wb, wc, p_c, p_c, p_c)


def _out_kernel(m_ref, w_ref, x_ref, g_ref, o_ref, *, final_norm):
    y = x_ref[...] + jnp.dot(m_ref[...], w_ref[...], preferred_element_type=F32)
    if final_norm:
        y = y * lax.rsqrt(jnp.mean(y * y, axis=-1, keepdims=True) + RMS_EPS) * g_ref[...]
    o_ref[...] = y


def _out_proj(merged, w_out, x2d, g, final_norm, tm):
    m, d = x2d.shape
    return pl.pallas_call(
        functools.partial(_out_kernel, final_norm=final_norm),
        grid=(m // tm,),
        in_specs=[pl.BlockSpec((tm, d), lambda i: (i, 0)), pl.BlockSpec((d, d), lambda i: (0, 0)),
                  pl.BlockSpec((tm, d), lambda i: (i, 0)), pl.BlockSpec((1, d), lambda i: (0, 0))],
        out_specs=pl.BlockSpec((tm, d), lambda i: (i, 0)),
        out_shape=jax.ShapeDtypeStruct((m, d), F32),
        compiler_params=_params(1),
        name="out_proj",
    )(merged, w_out, x2d, g.reshape(1, d))


def _pick_tile(n, candidates):
    for t in candidates:
        if n % t == 0:
            return t
    raise ValueError(f"no tile for extent {n}")


class _Tiles(NamedTuple):
    norm_rows: int
    proj_rows: int
    proj_cols: int
    proj_cols_b: int
    mem_rows: int
    cross_rows: int
    wkv_rows: int
    merge_rows: int
    merge_cols: int
    out_rows: int


def _tiles(m, seq, mem_rows):
    return _Tiles(
        norm_rows=_pick_tile(m, (1024, 512, 256)),
        proj_rows=_pick_tile(seq, (2048, 1024, 512, 256)),
        proj_cols=1024,
        proj_cols_b=768,
        mem_rows=_pick_tile(mem_rows, (1024, 512, 256)),
        cross_rows=_pick_tile(seq, (1024, 512, 256)),
        wkv_rows=_pick_tile(seq, (512, 256, 128, 64)),
        merge_rows=_pick_tile(m, (1024, 512, 256)),
        merge_cols=1024,
        out_rows=_pick_tile(m, (512, 256)),
    )


def _pad_cols(w, width):
    return jnp.pad(w, ((0, 0), (0, width - w.shape[1])))


def _layer(x2d, mem2d, bsz, seq, mem_len, bias_tiles, rel_bias, norm_g, mem_norm_g, w_in, rw, w_mem_kv, w_proj_a, w_proj_b,
           w_proj_c, w_out, final_g):
    m, d = x2d.shape
    t = _tiles(m, seq, mem2d.shape[0])

    c_z_a = 3 * A_WIDTH
    c_rkv_b = c_z_a + A_WIDTH
    c_z_b = c_rkv_b + 3 * B_WIDTH
    c_lw = c_z_b + B_WIDTH
    c_la = c_lw + LORA
    c_q_c = c_la + LORA

    ones = lambda n: jnp.ones((1, n), F32)
    wt = w_in.T
    (mu_r, mu_k, mu_v, mu_w, mu_a, w0, w_decay2, a0, w_aaa2, k_k, k_a, r_k, lnx_w, lnx_b) = rw
    vec = lambda v: v.reshape(1, -1)
    pad_rows = lambda w: jnp.pad(w, ((0, LORA_PAD - LORA), (0, 0)))
    wt_lora = jnp.concatenate([pad_rows(wt[c_lw:c_la]), pad_rows(wt[c_la:c_q_c])], axis=0)
    mu_wa = jnp.concatenate([_pad_cols(vec(mu_w), LORA_PAD), _pad_cols(vec(mu_a), LORA_PAD)], axis=1)
    h, p_lora = _norm_proj(x2d, norm_g, wt_lora, mu_wa, t.norm_rows, seq)
    scale_a = jnp.concatenate([jnp.full((1, A_WIDTH), MOBA_Q_SCALE, F32), ones(3 * A_WIDTH)], axis=1)
    p_a = _matmul_wt(h, wt, 0, 4 * A_WIDTH, scale_a, BF16, t.proj_rows, t.proj_cols, "in_proj_a")
    p_b = _matmul_wt(h, wt, c_rkv_b, 3 * B_WIDTH, jnp.concatenate([vec(mu_r), vec(mu_k), vec(mu_v)], axis=1), F32,
                     t.proj_rows, t.proj_cols_b, "in_proj_b", seq_rows=seq)
    p_zb = _matmul_wt(h, wt, c_z_b, B_WIDTH, ones(B_WIDTH), BF16, t.proj_rows, t.proj_cols_b, "in_proj_zb")
    n_c = 2 * C_WIDTH + N_BRANCHES * d
    p_c = _matmul_wt(h, wt, c_q_c, n_c, ones(n_c), BF16, t.proj_rows, t.proj_cols, "in_proj_c")

    mem_n = _rmsnorm(mem2d, mem_norm_g, BF16, t.mem_rows)
    kv = _matmul(mem_n, w_mem_kv.astype(BF16), BF16, t.mem_rows, t.proj_cols, "mem_kv")

    ya = _moba(p_a, rel_bias, bias_tiles, bsz, seq)
    yc = _cross(p_c, kv, bsz, seq, mem_len, t.cross_rows)

    vecs = (vec(w0), vec(a0), vec(k_k), vec(k_a), vec(r_k), vec(lnx_w), vec(lnx_b))
    yb = _rwkv(p_b, p_lora, p_zb, vecs, pad_rows(w_decay2), pad_rows(w_aaa2), bsz, seq, t.wkv_rows)

    merged = _merge(ya, yb, yc, w_proj_a.astype(BF16), w_proj_b.astype(BF16), w_proj_c.astype(BF16), p_c, 2 * C_WIDTH, d,
                    t.merge_rows, t.merge_cols)
    g = final_g if final_g is not None else jnp.ones((d,), F32)
    return _out_proj(merged, w_out.astype(BF16), x2d, g, final_g is not None, t.out_rows)


def kernel(x, mem, rel_bias, norm_g, mem_norm_g, w_in, rw_mu_r, rw_mu_k, rw_mu_v, rw_mu_w, rw_mu_a, rw_w0, rw_w_decay2, rw_a0, rw_w_aaa2, rw_k_k, rw_k_a, rw_r_k, rw_lnx_w, rw_lnx_b, w_mem_kv, w_proj_a, w_proj_b, w_proj_c, w_out, final_norm_g):
    bsz, seq, d = x.shape
    mem_len = mem.shape[1]
    depth = norm_g.shape[0]
    x2d = x.reshape(bsz * seq, d)
    mem2d = mem.reshape(bsz * mem_len, d)
    bias_tiles = _bias_tiles(rel_bias)
    for l in range(depth):
        rw = (rw_mu_r[l], rw_mu_k[l], rw_mu_v[l], rw_mu_w[l], rw_mu_a[l], rw_w0[l], rw_w_decay2[l], rw_a0[l], rw_w_aaa2[l],
              rw_k_k[l], rw_k_a[l], rw_r_k[l], rw_lnx_w[l], rw_lnx_b[l])
        x2d = _layer(x2d, mem2d, bsz, seq, mem_len, bias_tiles, rel_bias, norm_g[l], mem_norm_g[l], w_in[l], rw, w_mem_kv[l],
                     w_proj_a[l], w_proj_b[l], w_proj_c[l], w_out[l], final_norm_g if l == depth - 1 else None)
    return x2d.reshape(bsz, seq, d)
```

```python
import functools
import math
from typing import NamedTuple

import jax
import jax.numpy as jnp
from jax import lax
from jax.experimental import pallas as pl
from jax.experimental.pallas import tpu as pltpu

F32 = jnp.float32
BF16 = jnp.bfloat16
LANES = 128
BF16_SUBLANES = 16

RMS_EPS = 1e-6

A_HEADS = 12
A_HEAD_DIM = 128
A_WIDTH = A_HEADS * A_HEAD_DIM
MOBA_BLOCK = 256
MOBA_TOPK = 3
MOBA_HEADS_PER_STEP = 6
MOBA_FAR_GROUP = 4
MOBA_Q_SCALE = A_HEAD_DIM ** -0.5 * math.log2(math.e)
REL_BUCKETS = 32
REL_MAX_DIST = 128

B_HEAD_DIM = 64
B_WIDTH = 1536
LORA = 96
LORA_PAD = LANES
LNX_EPS = 64e-5
WKV_CHUNK = 64
WKV_GROUP = 256
WKV_HEADS_PER_GROUP = WKV_GROUP // B_HEAD_DIM
WKV_GROUPS_PER_STEP = 6
WKV_CHUNKS_PER_STEP = 4

C_HEADS = 4
C_HEAD_DIM = 256
C_WIDTH = C_HEADS * C_HEAD_DIM

N_BRANCHES = 3

VMEM_LIMIT = 48 * 1024 * 1024
VMEM_LIMIT_PROJ = 58 * 1024 * 1024

NN = (((1,), (0,)), ((), ()))
NT = (((1,), (1,)), ((), ()))
TN = (((0,), (0,)), ((), ()))

MASKED = -1e30
LOG2E = math.log2(math.e)


def _params(n_axes, vmem_limit=VMEM_LIMIT):
    return pltpu.CompilerParams(dimension_semantics=("arbitrary",) * n_axes, vmem_limit_bytes=vmem_limit)


def _sigmoid(x):
    return 0.5 * jnp.tanh(0.5 * x) + 0.5


def _silu(z):
    return z * _sigmoid(z)


def _bf(x):
    return x.astype(BF16)


def _mm(a, b, dn):
    return lax.dot_general(a, b, dn, preferred_element_type=F32)


def _each(f, *lists):
    return [f(*args) for args in zip(*lists)]


def _norm_matmul_kernel(x_ref, g_ref, w_ref, o_ref):
    x = x_ref[...]
    h = (x * lax.rsqrt(jnp.mean(x * x, axis=-1, keepdims=True) + RMS_EPS) * g_ref[...]).astype(BF16)
    o_ref[...] = jnp.dot(h, w_ref[...].astype(BF16), preferred_element_type=F32).astype(o_ref.dtype)


def _norm_matmul(x2d, g, w, out_dtype, tm, tn, name):
    m, k = x2d.shape
    n = w.shape[1]
    return pl.pallas_call(
        _norm_matmul_kernel,
        grid=(n // tn, m // tm),
        in_specs=[pl.BlockSpec((tm, k), lambda j, i: (i, 0)), pl.BlockSpec((1, k), lambda j, i: (0, 0)),
                  pl.BlockSpec((k, tn), lambda j, i: (0, j))],
        out_specs=pl.BlockSpec((tm, tn), lambda j, i: (i, j)),
        out_shape=jax.ShapeDtypeStruct((m, n), out_dtype),
        compiler_params=_params(2),
        name=name,
    )(x2d, g.reshape(1, k), w)


def _matmul_wt_kernel(a_ref, wt_ref, s_ref, o_ref, wb_ref, *carry_ref, seq_rows):
    i = pl.program_id(1)

    @pl.when(i == 0)
    def _():
        wb_ref[...] = wt_ref[...].astype(BF16)
        for ref in carry_ref:
            ref[...] = jnp.zeros_like(ref)

    x = lax.dot_general(a_ref[...], wb_ref[...], NT, preferred_element_type=F32)
    if seq_rows is None:
        o_ref[...] = (x * s_ref[...]).astype(o_ref.dtype)
    else:
        tm = x.shape[0]
        last, = carry_ref
        first_row = lax.broadcasted_iota(jnp.int32, x.shape, 0) == 0
        carried = jnp.where((i * tm) % seq_rows == 0, 0.0, last[0:1, :])
        prev = jnp.where(first_row, carried, pltpu.roll(x, 1, 0))
        last[0:1, :] = x[tm - 1:tm, :]
        o_ref[...] = (x + (prev - x) * s_ref[...]).astype(o_ref.dtype)


def _matmul_wt(a, wt, row0, n, col_vec, out_dtype, tm, tn, name, seq_rows=None):
    m, k = a.shape
    scratch = [pltpu.VMEM((tn, k), BF16)]
    if seq_rows is not None:
        assert seq_rows % tm == 0
        scratch.append(pltpu.VMEM((8, tn), F32))
    return pl.pallas_call(
        functools.partial(_matmul_wt_kernel, seq_rows=seq_rows),
        grid=(n // tn, m // tm),
        in_specs=[pl.BlockSpec((tm, k), lambda j, i: (i, 0)),
                  pl.BlockSpec((pl.Element(tn), pl.Element(k)), lambda j, i: (pl.multiple_of(row0 + j * tn, 8), 0)),
                  pl.BlockSpec((1, tn), lambda j, i: (0, j))],
        out_specs=pl.BlockSpec((tm, tn), lambda j, i: (i, j)),
        out_shape=jax.ShapeDtypeStruct((m, n), out_dtype),
        scratch_shapes=scratch,
        compiler_params=_params(2, VMEM_LIMIT_PROJ),
        name=name,
    )(a, wt, col_vec)


def _norm_proj_kernel(x_ref, g_ref, wt_ref, mu_ref, h_ref, o_ref, wb_ref, last, *, seq_rows):
    i = pl.program_id(0)

    @pl.when(i == 0)
    def _():
        wb_ref[...] = wt_ref[...].astype(BF16)
        last[...] = jnp.zeros_like(last)

    x = x_ref[...]
    h = (x * lax.rsqrt(jnp.mean(x * x, axis=-1, keepdims=True) + RMS_EPS) * g_ref[...]).astype(BF16)
    h_ref[...] = h
    y = lax.dot_general(h, wb_ref[...], NT, preferred_element_type=F32)
    tm = y.shape[0]
    first_row = lax.broadcasted_iota(jnp.int32, y.shape, 0) == 0
    carried = jnp.where((i * tm) % seq_rows == 0, 0.0, last[0:1, :])
    prev = jnp.where(first_row, carried, pltpu.roll(y, 1, 0))
    last[0:1, :] = y[tm - 1:tm, :]
    o_ref[...] = y + (prev - y) * mu_ref[...]


def _norm_proj(x2d, g, wt, mu, tm, seq_rows):
    m, d = x2d.shape
    n = wt.shape[0]
    assert seq_rows % tm == 0
    return pl.pallas_call(
        functools.partial(_norm_proj_kernel, seq_rows=seq_rows),
        grid=(m // tm,),
        in_specs=[pl.BlockSpec((tm, d), lambda i: (i, 0)), pl.BlockSpec((1, d), lambda i: (0, 0)),
                  pl.BlockSpec((n, d), lambda i: (0, 0)), pl.BlockSpec((1, n), lambda i: (0, 0))],
        out_specs=[pl.BlockSpec((tm, d), lambda i: (i, 0)), pl.BlockSpec((tm, n), lambda i: (i, 0))],
        out_shape=[jax.ShapeDtypeStruct((m, d), BF16), jax.ShapeDtypeStruct((m, n), F32)],
        scratch_shapes=[pltpu.VMEM((n, d), BF16), pltpu.VMEM((8, n), F32)],
        compiler_params=_params(1),
        name="rmsnorm_in_proj_lora",
    )(x2d, g.reshape(1, d), wt, mu)


def _t5_bucket(dist):
    n = jnp.maximum(dist, 0)
    max_exact = REL_BUCKETS // 2
    nf = jnp.maximum(n, max_exact).astype(F32)
    large = max_exact + (jnp.log(nf / max_exact) / math.log(REL_MAX_DIST / max_exact) * (REL_BUCKETS - max_exact)).astype(jnp.int32)
    large = jnp.minimum(large, REL_BUCKETS - 1)
    return jnp.where(n < max_exact, n, large)


def _bias_kernel(rel_ref, bucket_ref, o_ref):
    h = pl.program_id(0)
    bucket = bucket_ref[...]
    acc = jnp.zeros(bucket.shape, F32)
    for b in range(REL_BUCKETS):
        acc = jnp.where(bucket == b, rel_ref[b, h], acc)
    qpos = lax.broadcasted_iota(jnp.int32, bucket.shape, 0) + MOBA_BLOCK
    kpos = lax.broadcasted_iota(jnp.int32, bucket.shape, 1)
    o_ref[0] = jnp.where(kpos <= qpos, acc * LOG2E, MASKED)


def _bias_tiles(rel_bias):
    blk = MOBA_BLOCK
    qpos = lax.broadcasted_iota(jnp.int32, (blk, 2 * blk), 0) + blk
    kpos = lax.broadcasted_iota(jnp.int32, (blk, 2 * blk), 1)
    bucket = _t5_bucket(qpos - kpos)
    return pl.pallas_call(
        _bias_kernel,
        grid=(A_HEADS,),
        in_specs=[pl.BlockSpec(memory_space=pltpu.SMEM), pl.BlockSpec((blk, 2 * blk), lambda h: (0, 0))],
        out_specs=pl.BlockSpec((1, blk, 2 * blk), lambda h: (h, 0, 0)),
        out_shape=jax.ShapeDtypeStruct((A_HEADS, blk, 2 * blk), F32),
        compiler_params=_params(1),
        name="t5_bias",
    )(rel_bias, bucket)


def _moba_kernel(rel_ref, q_ref, k_ref, v_ref, z_ref, bias_ref, o_ref, kmean_ref, m_ref, acc_ref):
    hb = pl.program_id(1)
    qi = pl.program_id(2)
    blk, hd = MOBA_BLOCK, A_HEAD_DIM
    n_heads = q_ref.shape[1] // hd
    nb = k_ref.shape[0] // blk
    nbp = kmean_ref.shape[1]

    @pl.when(qi == 0)
    def _():
        kmean_ref[...] = jnp.zeros_like(kmean_ref)
        for j in range(nb):
            mean_j = jnp.sum(k_ref[j * blk:(j + 1) * blk, :].astype(F32), axis=0, keepdims=True) * (1.0 / blk)
            for hh in range(n_heads):
                kmean_ref[hh, j:j + 1, :] = mean_j[:, hh * hd:(hh + 1) * hd]

    n_far = min(MOBA_FAR_GROUP, nb)
    never = LANES - 1
    blk_id = lax.broadcasted_iota(jnp.int32, (nbp, blk), 0)
    blk_f = blk_id.astype(F32)
    eye = jnp.where(lax.broadcasted_iota(jnp.int32, (nbp, LANES), 0) == lax.broadcasted_iota(jnp.int32, (nbp, LANES), 1), 1.0, 0.0).astype(BF16)
    j_prev = jnp.maximum(qi - 1, 0)
    own = pl.multiple_of(qi * blk, blk)
    prev = pl.multiple_of(j_prev * blk, blk)

    heads = [slice(hh * hd, (hh + 1) * hd) for hh in range(n_heads)]
    q = [q_ref[:, hl] for hl in heads]

    def gate_of(qh, hh):
        km = kmean_ref[hh]
        km_hi = km.astype(BF16)
        km_lo = (km - km_hi.astype(F32)).astype(BF16)
        return jnp.where(blk_id < qi, _mm(km_hi, qh, NT) + _mm(km_lo, qh, NT), -jnp.inf)

    g = _each(gate_of, q, range(n_heads))
    sel_t = [jnp.zeros((nbp, blk), F32) for _ in heads]
    for _ in range(MOBA_TOPK):
        gmax = _each(lambda x: jnp.max(x, axis=0, keepdims=True), g)
        first = _each(lambda x, mx: jnp.min(jnp.where(x == mx, blk_f, float(nbp)), axis=0, keepdims=True), g, gmax)
        pick = _each(lambda f, mx: (blk_f == f) & (mx > -jnp.inf), first, gmax)
        sel_t = _each(lambda p, s: jnp.where(p, 1.0, s), pick, sel_t)
        g = _each(lambda p, x: jnp.where(p, -jnp.inf, x), pick, g)
    sel = _each(lambda s: _mm(s.astype(BF16), eye, TN), sel_t)
    qm = _each(lambda qh, s: jnp.concatenate([qh, jnp.where(s > 0.0, 0.0, MASKED).astype(BF16)], axis=1), q, sel)

    lane = lax.broadcasted_iota(jnp.int32, (blk, LANES), 1)

    def masked_logits(keys, key_blocks):
        col = jnp.concatenate([jnp.where(lane == j, 1.0, 0.0).astype(BF16) for j in key_blocks], axis=0)
        return _each(lambda x, kk: _mm(x, jnp.concatenate([kk, col], axis=1), NT), qm, keys)

    def pv(p, values):
        ones = jnp.ones(values[0].shape, BF16)
        return _each(lambda x, v: jnp.dot(x.astype(BF16), jnp.concatenate([v, ones], axis=1), preferred_element_type=F32), p, values)

    cat0 = lambda ref, hl: jnp.concatenate([ref[pl.ds(prev, blk), hl], ref[pl.ds(own, blk), hl]], axis=0)
    raw = masked_logits([cat0(k_ref, hl) for hl in heads], [j_prev, -1])
    t = _each(lambda x, hh: x + bias_ref[hh], raw, range(n_heads))
    m = _each(lambda x: jnp.max(x, axis=-1, keepdims=True), t)
    p = _each(lambda x, mx: jnp.exp2(x - mx), t, m)
    acc = pv(p, [cat0(v_ref, hl) for hl in heads])
    for hh in range(n_heads):
        m_ref[hh] = jnp.broadcast_to(m[hh], (blk, LANES))
        acc_ref[hh] = acc[hh]

    bias_far = [rel_ref[REL_BUCKETS - 1, hb * n_heads + hh] * LOG2E for hh in range(n_heads)]

    def body(gi, carry):
        rows = pl.ds(pl.multiple_of(gi * (n_far * blk), n_far * blk), n_far * blk)
        blocks = [gi * n_far + j for j in range(n_far)]
        raw = masked_logits([k_ref[rows, hl] for hl in heads], [jnp.where(j < j_prev, j, never) for j in blocks])
        m = [m_ref[hh][:, :1] for hh in range(n_heads)]
        m_new = _each(lambda mx, x, b: jnp.maximum(mx, jnp.max(x, axis=-1, keepdims=True) + b), m, raw, bias_far)
        p = _each(lambda x, mn, b: jnp.exp2(x + (b - mn)), raw, m_new, bias_far)
        alpha = _each(lambda mx, mn: jnp.exp2(mx - mn), m, m_new)
        upd = pv(p, [v_ref[rows, hl] for hl in heads])
        for hh in range(n_heads):
            m_ref[hh] = jnp.broadcast_to(m_new[hh], (blk, LANES))
            acc_ref[hh] = alpha[hh] * acc_ref[hh] + upd[hh]
        return carry

    lax.fori_loop(0, (j_prev + n_far - 1) // n_far, body, 0)
    for hh, hl in enumerate(heads):
        y = acc_ref[hh, :, :hd] / acc_ref[hh, :, hd:]
        o_ref[:, hl] = (y * _silu(z_ref[:, hl].astype(F32))).astype(o_ref.dtype)


def _moba(p_a, rel_bias, bias_tiles, bsz, seq):
    blk = MOBA_BLOCK
    nq = seq // blk
    hw = MOBA_HEADS_PER_STEP * A_HEAD_DIM
    ns = A_WIDTH // hw
    nbp = -(-nq // BF16_SUBLANES) * BF16_SUBLANES
    assert nbp < LANES
    return pl.pallas_call(
        _moba_kernel,
        grid=(bsz, ns, nq),
        in_specs=[
            pl.BlockSpec(memory_space=pltpu.SMEM),
            pl.BlockSpec((blk, hw), lambda b, h, i: (b * nq + i, h)),
            pl.BlockSpec((seq, hw), lambda b, h, i: (b, ns + h)),
            pl.BlockSpec((seq, hw), lambda b, h, i: (b, 2 * ns + h)),
            pl.BlockSpec((blk, hw), lambda b, h, i: (b * nq + i, 3 * ns + h)),
            pl.BlockSpec((MOBA_HEADS_PER_STEP, blk, 2 * blk), lambda b, h, i: (h, 0, 0)),
        ],
        out_specs=pl.BlockSpec((blk, hw), lambda b, h, i: (b * nq + i, h)),
        out_shape=jax.ShapeDtypeStruct((bsz * seq, A_WIDTH), BF16),
        scratch_shapes=[pltpu.VMEM((MOBA_HEADS_PER_STEP, nbp, A_HEAD_DIM), F32), pltpu.VMEM((MOBA_HEADS_PER_STEP, blk, LANES), F32),
                        pltpu.VMEM((MOBA_HEADS_PER_STEP, blk, 2 * A_HEAD_DIM), F32)],
        compiler_params=_params(3),
        name="moba_attention",
    )(rel_bias, p_a, p_a, p_a, p_a, bias_tiles)


def _cross_kernel(q_ref, k_ref, v_ref, z_ref, o_ref):
    hd = C_HEAD_DIM
    heads = [slice(h * hd, (h + 1) * hd) for h in range(C_HEADS)]
    s = _each(lambda hl: _mm(q_ref[:, hl], k_ref[:, hl], NT) * (hd ** -0.5), heads)
    m = _each(lambda x: jnp.max(x, axis=-1, keepdims=True), s)
    p = _each(lambda x, mx: jnp.exp(x - mx), s, m)
    l = _each(lambda x: jnp.sum(x, axis=-1, keepdims=True), p)
    y = _each(lambda x, hl: jnp.dot(x.astype(BF16), v_ref[:, hl], preferred_element_type=F32), p, heads)
    for hl, yh, lh in zip(heads, y, l):
        o_ref[:, hl] = (yh / lh * _silu(z_ref[:, hl].astype(F32))).astype(o_ref.dtype)


def _cross(p_c, kv, bsz, seq, mem_len, tq):
    nt = seq // tq
    return pl.pallas_call(
        _cross_kernel,
        grid=(bsz, nt),
        in_specs=[
            pl.BlockSpec((tq, C_WIDTH), lambda b, i: (b * nt + i, 0)),
            pl.BlockSpec((mem_len, C_WIDTH), lambda b, i: (b, 0)),
            pl.BlockSpec((mem_len, C_WIDTH), lambda b, i: (b, 1)),
            pl.BlockSpec((tq, C_WIDTH), lambda b, i: (b * nt + i, 1)),
        ],
        out_specs=pl.BlockSpec((tq, C_WIDTH), lambda b, i: (b * nt + i, 0)),
        out_shape=jax.ShapeDtypeStruct((bsz * seq, C_WIDTH), BF16),
        compiler_params=_params(2),
        name="memory_attention",
    )(p_c, kv, kv, p_c)


class _WkvConsts:
    def __init__(self):
        c, g, hd = WKV_CHUNK, WKV_GROUP, B_HEAD_DIM
        row = lax.broadcasted_iota(jnp.int32, (g, g), 0)
        col = lax.broadcasted_iota(jnp.int32, (g, g), 1)
        self.same_head = jnp.where((row // hd) == (col // hd), 1.0, 0.0)
        self.ones_bd = self.same_head.astype(BF16)
        t = lax.broadcasted_iota(jnp.int32, (c, g), 0)
        s = lax.broadcasted_iota(jnp.int32, (c, g), 1) % hd
        self.strict = s < t
        self.incl = s <= t
        self.strict_incl = jnp.concatenate([self.strict, self.incl], axis=0)
        self.eye = jnp.where(s == t, 1.0, 0.0).astype(F32)
        tr = lax.broadcasted_iota(jnp.int32, (c, c), 0)
        tc = lax.broadcasted_iota(jnp.int32, (c, c), 1)
        self.lower = jnp.where(tc <= tr, 1.0, 0.0).astype(BF16)

    def bd(self, p):
        return jnp.concatenate([p] * WKV_HEADS_PER_GROUP, axis=0) * self.ones_bd


def _wkv_prepare(rm, kmod, vm, avec, bvec, logdec, k):
    c = WKV_CHUNK
    cat = lambda u, w: jnp.concatenate([u, w], axis=0)
    cl = _each(lambda x: jnp.dot(k.lower, _bf(x), preferred_element_type=F32), logdec)
    cl_last = _each(lambda x: x[c - 1:c, :], cl)
    rt = _each(lambda r, x: _bf(r * jnp.exp(x)), rm, cl)
    at = _each(lambda a, x, ld: _bf(a * jnp.exp(x - ld)), avec, cl, logdec)
    e_neg = _each(lambda x: jnp.exp(-x), cl)
    bt = _each(lambda b, e: k.bd(_bf(b * e)), bvec, e_neg)
    kt = _each(lambda kk, e: k.bd(_bf(kk * e)), kmod, e_neg)
    decay = _each(jnp.exp, cl_last)
    e_rem = _each(lambda d, e: d * e, decay, e_neg)
    bkh = _each(lambda b, kk, e: cat(_bf(b * e), _bf(kk * e)), bvec, kmod, e_rem)
    vb = _each(_bf, vm)

    lhs = _each(cat, at, rt)
    ab = _each(lambda x, y: _mm(x, y, NT), lhs, bt)
    ak = _each(lambda x, y: _mm(x, y, NT), lhs, kt)
    a_ab = _each(lambda x: jnp.where(k.strict, x[:c], 0.0), ab)
    a_rb = _each(lambda x: _bf(jnp.where(k.incl, x[c:], 0.0)), ab)
    a_k = _each(lambda x: _bf(jnp.where(k.strict_incl, x, 0.0)), ak)

    inv = _each(lambda x: k.eye + x, a_ab)
    pw = _each(_bf, a_ab)
    pw = _each(lambda p: _bf(_mm(p, k.bd(p), NN)), pw)
    for _ in range(int(math.log2(c)) - 2):
        tp = _each(lambda i, p: _mm(cat(_bf(i), p), k.bd(p), NN), inv, pw)
        inv = _each(lambda i, t: i + t[:c], inv, tp)
        pw = _each(lambda t: _bf(t[c:]), tp)
    inv = _each(lambda i, p: _bf(i + _mm(_bf(i), k.bd(p), NN)), inv, pw)

    av = _each(lambda x, v: _mm(x, k.bd(v), NN), a_k, vb)
    w = _each(lambda i, a: _mm(i, k.bd(a), NN), inv, at)
    u0 = _each(lambda i, x: _mm(i, k.bd(_bf(x[:c])), NN), inv, av)
    wr = _each(lambda ww, r: cat(_bf(ww), r), w, rt)
    y0 = _each(lambda x: x[c:], av)
    return list(zip(wr, u0, y0, a_rb, vb, bkh, decay))


def _wkv_apply(prep, state, k):
    c = WKV_CHUNK
    wr, u0, y0, a_rb, vb, bkh, decay = (list(x) for x in zip(*prep))
    uy = _each(lambda x, s: _mm(x, _bf(s), NT), wr, state)
    ub = _each(lambda x, y: _bf(x[:c] + y), uy, u0)
    y = _each(lambda x, a, uu, z: x[c:] + _mm(a, k.bd(uu), NN) + z, uy, a_rb, ub, y0)
    upd = _each(lambda uu, v, bk: _mm(jnp.concatenate([uu, v], axis=0), bk, TN), ub, vb, bkh)
    new_state = _each(lambda s, dc, d: s * dc + d * k.same_head, state, decay, upd)
    return y, new_state


def _rwkv_kernel(r_ref, k_ref, v_ref, lwla_ref, z_ref, w0_ref, a0_ref, kk_ref, ka_ref, rk_ref, lnw_ref, lnb_ref,
                 wd_ref, wa_ref, o_ref, state_ref):
    c, g = WKV_CHUNK, WKV_GROUP
    n_groups = r_ref.shape[1] // g
    n_chunks = WKV_CHUNKS_PER_STEP

    @pl.when(pl.program_id(2) == 0)
    def _():
        state_ref[...] = jnp.zeros_like(state_ref)

    k = _WkvConsts()
    wd = _bf(wd_ref[...])
    wa = _bf(wa_ref[...])
    inv_hd = 1.0 / B_HEAD_DIM
    groups = [slice(gi * g, (gi + 1) * g) for gi in range(n_groups)]
    split = lambda x: [x[ch * c:(ch + 1) * c, ln] for ch in range(n_chunks) for ln in groups]

    def segsums(xs):
        s = jnp.dot(jnp.concatenate(_each(_bf, xs), axis=0), k.ones_bd, preferred_element_type=F32)
        return [s[i * c:(i + 1) * c] for i in range(len(xs))]

    def step(ci, carry):
        rows = pl.ds(pl.multiple_of(ci * (n_chunks * c), n_chunks * c), n_chunks * c)
        rm, km, vm = r_ref[rows, :], k_ref[rows, :], v_ref[rows, :]
        lwla = lwla_ref[rows, :]
        lw = _bf(jnp.tanh(lwla[:, :LORA_PAD]))
        la = _bf(lwla[:, LORA_PAD:])
        logdec = -math.exp(-0.5) * _sigmoid(w0_ref[...] + jnp.dot(lw, wd, preferred_element_type=F32))
        a_lr = _sigmoid(a0_ref[...] + jnp.dot(la, wa, preferred_element_type=F32))
        kmod = km * (1.0 + (a_lr - 1.0) * ka_ref[...])
        kk = split(km * kk_ref[...])
        n = len(kk)
        sums = segsums(_each(lambda x: x * x, kk) + split(rm * kmod * rk_ref[...]))
        kk = _each(lambda x, ss: x * jnp.minimum(lax.rsqrt(ss), 1e12), kk, sums[:n])
        bonus = _each(lambda s, v: s * v, sums[n:], split(vm))
        prep = _wkv_prepare(split(rm), split(kmod), split(vm), _each(lambda x: -x, kk),
                            _each(lambda x, a: x * a, kk, split(a_lr)), split(logdec), k)
        state = [state_ref[gi] for gi in range(n_groups)]
        y = []
        for ch in range(n_chunks):
            y_ch, state = _wkv_apply(prep[ch * n_groups:(ch + 1) * n_groups], state, k)
            y += y_ch
        for gi in range(n_groups):
            state_ref[gi] = state[gi]
        yc = _each(lambda x, s: x - s * inv_hd, y, segsums(y))
        var = _each(lambda s: s * inv_hd, segsums(_each(lambda x: x * x, yc)))
        gate = _silu(z_ref[rows, :].astype(F32))
        for ch in range(n_chunks):
            out_rows = pl.ds(pl.multiple_of(ci * (n_chunks * c) + ch * c, c), c)
            for gi, ln in enumerate(groups):
                i = ch * n_groups + gi
                yn = yc[i] * lax.rsqrt(var[i] + LNX_EPS) * lnw_ref[:, ln] + lnb_ref[:, ln]
                o_ref[out_rows, ln] = ((yn + bonus[i]) * gate[ch * c:(ch + 1) * c, ln]).astype(o_ref.dtype)
        return carry

    lax.fori_loop(0, r_ref.shape[0] // (n_chunks * c), step, 0)


def _rwkv(p_b, p_lora, p_zb, vecs, wd2, wa2, bsz, seq, tt):
    g = WKV_GROUP
    gw = WKV_GROUPS_PER_STEP * g
    ns = B_WIDTH // gw
    nt = seq // tt
    row = lambda b, gi, t: b * nt + t
    vec_spec = pl.BlockSpec((1, gw), lambda b, gi, t: (0, gi))
    lora_w_spec = pl.BlockSpec((LORA_PAD, gw), lambda b, gi, t: (0, gi))
    return pl.pallas_call(
        _rwkv_kernel,
        grid=(bsz, ns, nt),
        in_specs=[
            pl.BlockSpec((tt, gw), lambda b, gi, t: (row(b, gi, t), gi)),
            pl.BlockSpec((tt, gw), lambda b, gi, t: (row(b, gi, t), ns + gi)),
            pl.BlockSpec((tt, gw), lambda b, gi, t: (row(b, gi, t), 2 * ns + gi)),
            pl.BlockSpec((tt, 2 * LORA_PAD), lambda b, gi, t: (row(b, gi, t), 0)),
            pl.BlockSpec((tt, gw), lambda b, gi, t: (row(b, gi, t), gi)),
        ] + [vec_spec] * len(vecs) + [lora_w_spec, lora_w_spec],
        out_specs=pl.BlockSpec((tt, gw), lambda b, gi, t: (row(b, gi, t), gi)),
        out_shape=jax.ShapeDtypeStruct((bsz * seq, B_WIDTH), BF16),
        scratch_shapes=[pltpu.VMEM((WKV_GROUPS_PER_STEP, g, g), F32)],
        compiler_params=_params(3),
        name="rwkv7_time_mix",
    )(p_b, p_b, p_b, p_lora, p_zb, *vecs, wd2, wa2)


def _merge_kernel(ya_ref, yb_ref, yc_ref, wa_ref, wb_ref, wc_ref, ga_ref, gb_ref, gc_ref, o_ref):
    def branch(y_ref, w_ref, g_ref):
        return _sigmoid(g_ref[...].astype(F32)) * jnp.dot(y_ref[...], w_ref[...], preferred_element_type=F32)

    o_ref[...] = (branch(ya_ref, wa_ref, ga_ref) + branch(yb_ref, wb_ref, gb_ref) + branch(yc_ref, wc_ref, gc_ref)).astype(o_ref.dtype)


def _merge(ya, yb, yc, wa, wb, wc, p_c, gate_col0, d_model, tm, tn):
    m = ya.shape[0]
    nj = d_model // tn
    j0 = gate_col0 // tn
    y_spec = lambda width: pl.BlockSpec((tm, width), lambda j, i: (i, 0))
    w_spec = lambda width: pl.BlockSpec((width, tn), lambda j, i: (0, j))
    g_spec = lambda br: pl.BlockSpec((tm, tn), lambda j, i: (i, j0 + br * nj + j))
    return pl.pallas_call(
        _merge_kernel,
        grid=(nj, m // tm),
        in_specs=[y_spec(A_WIDTH), y_spec(B_WIDTH), y_spec(C_WIDTH), w_spec(A_WIDTH), w_spec(B_WIDTH), w_spec(C_WIDTH),
                  g_spec(0), g_spec(1), g_spec(2)],
        out_specs=pl.BlockSpec((tm, tn), lambda j, i: (i, j)),
        out_shape=jax.ShapeDtypeStruct((m, d_model), BF16),
        compiler_params=_params(2, VMEM_LIMIT_PROJ),
        name="gated_merge",
    )(ya, yb, yc, wa, wb, wc, p_c, p_c, p_c)


def _out_kernel(m_ref, w_ref, x_ref, g_ref, o_ref, wb_ref, *, final_norm):
    @pl.when(pl.program_id(0) == 0)
    def _():
        wb_ref[...] = w_ref[...].astype(BF16)

    y = x_ref[...] + jnp.dot(m_ref[...], wb_ref[...], preferred_element_type=F32)
    if final_norm:
        y = y * lax.rsqrt(jnp.mean(y * y, axis=-1, keepdims=True) + RMS_EPS) * g_ref[...]
    o_ref[...] = y


def _out_proj(merged, w_out, x2d, g, final_norm, tm):
    m, d = x2d.shape
    return pl.pallas_call(
        functools.partial(_out_kernel, final_norm=final_norm),
        grid=(m // tm,),
        in_specs=[pl.BlockSpec((tm, d), lambda i: (i, 0)),
                  pl.BlockSpec((d, d), lambda i: (0, 0), pipeline_mode=pl.Buffered(1)),
                  pl.BlockSpec((tm, d), lambda i: (i, 0)), pl.BlockSpec((1, d), lambda i: (0, 0))],
        out_specs=pl.BlockSpec((tm, d), lambda i: (i, 0)),
        out_shape=jax.ShapeDtypeStruct((m, d), F32),
        scratch_shapes=[pltpu.VMEM((d, d), BF16)],
        compiler_params=_params(1, VMEM_LIMIT_PROJ),
        name="out_proj",
    )(merged, w_out, x2d, g.reshape(1, d))


def _pick_tile(n, candidates):
    for t in candidates:
        if n % t == 0:
            return t
    raise ValueError(f"no tile for extent {n}")


class _Tiles(NamedTuple):
    norm_rows: int
    proj_rows: int
    proj_cols: int
    proj_cols_b: int
    mem_rows: int
    cross_rows: int
    wkv_rows: int
    merge_rows: int
    merge_cols: int
    out_rows: int


def _tiles(m, seq, mem_rows):
    return _Tiles(
        norm_rows=_pick_tile(m, (1024, 512, 256)),
        proj_rows=_pick_tile(seq, (2048, 1024, 512, 256)),
        proj_cols=1024,
        proj_cols_b=768,
        mem_rows=_pick_tile(mem_rows, (1024, 512, 256)),
        cross_rows=_pick_tile(seq, (1024, 512, 256)),
        wkv_rows=_pick_tile(seq, (512, 256, 128, 64)),
        merge_rows=_pick_tile(m, (1024, 512, 256)),
        merge_cols=1024,
        out_rows=_pick_tile(m, (512, 256)),
    )


def _pad_cols(w, width):
    return jnp.pad(w, ((0, 0), (0, width - w.shape[1])))


def _layer(x2d, mem2d, bsz, seq, mem_len, bias_tiles, rel_bias, norm_g, mem_norm_g, w_in, rw, w_mem_kv, w_proj_a, w_proj_b,
           w_proj_c, w_out, final_g):
    m, d = x2d.shape
    t = _tiles(m, seq, mem2d.shape[0])

    c_z_a = 3 * A_WIDTH
    c_rkv_b = c_z_a + A_WIDTH
    c_z_b = c_rkv_b + 3 * B_WIDTH
    c_lw = c_z_b + B_WIDTH
    c_la = c_lw + LORA
    c_q_c = c_la + LORA

    ones = lambda n: jnp.ones((1, n), F32)
    wt = w_in.T
    (mu_r, mu_k, mu_v, mu_w, mu_a, w0, w_decay2, a0, w_aaa2, k_k, k_a, r_k, lnx_w, lnx_b) = rw
    vec = lambda v: v.reshape(1, -1)
    pad_rows = lambda w: jnp.pad(w, ((0, LORA_PAD - LORA), (0, 0)))
    wt_lora = jnp.concatenate([pad_rows(wt[c_lw:c_la]), pad_rows(wt[c_la:c_q_c])], axis=0)
    mu_wa = jnp.concatenate([_pad_cols(vec(mu_w), LORA_PAD), _pad_cols(vec(mu_a), LORA_PAD)], axis=1)
    h, p_lora = _norm_proj(x2d, norm_g, wt_lora, mu_wa, t.norm_rows, seq)
    scale_a = jnp.concatenate([jnp.full((1, A_WIDTH), MOBA_Q_SCALE, F32), ones(3 * A_WIDTH)], axis=1)
    p_a = _matmul_wt(h, wt, 0, 4 * A_WIDTH, scale_a, BF16, t.proj_rows, t.proj_cols, "in_proj_a")
    p_b = _matmul_wt(h, wt, c_rkv_b, 3 * B_WIDTH, jnp.concatenate([vec(mu_r), vec(mu_k), vec(mu_v)], axis=1), F32,
                     t.proj_rows, t.proj_cols_b, "in_proj_b", seq_rows=seq)
    p_zb = _matmul_wt(h, wt, c_z_b, B_WIDTH, ones(B_WIDTH), BF16, t.proj_rows, t.proj_cols_b, "in_proj_zb")
    n_c = 2 * C_WIDTH + N_BRANCHES * d
    p_c = _matmul_wt(h, wt, c_q_c, n_c, ones(n_c), BF16, t.proj_rows, t.proj_cols, "in_proj_c")

    kv = _norm_matmul(mem2d, mem_norm_g, w_mem_kv, BF16, t.mem_rows, t.proj_cols, "mem_kv")

    ya = _moba(p_a, rel_bias, bias_tiles, bsz, seq)
    yc = _cross(p_c, kv, bsz, seq, mem_len, t.cross_rows)

    vecs = (vec(w0), vec(a0), vec(k_k), vec(k_a), vec(r_k), vec(lnx_w), vec(lnx_b))
    yb = _rwkv(p_b, p_lora, p_zb, vecs, pad_rows(w_decay2), pad_rows(w_aaa2), bsz, seq, t.wkv_rows)

    merged = _merge(ya, yb, yc, w_proj_a.astype(BF16), w_proj_b.astype(BF16), w_proj_c.astype(BF16), p_c, 2 * C_WIDTH, d,
                    t.merge_rows, t.merge_cols)
    g = final_g if final_g is not None else jnp.ones((d,), F32)
    return _out_proj(merged, w_out, x2d, g, final_g is not None, t.out_rows)


def kernel(x, mem, rel_bias, norm_g, mem_norm_g, w_in, rw_mu_r, rw_mu_k, rw_mu_v, rw_mu_w, rw_mu_a, rw_w0, rw_w_decay2, rw_a0, rw_w_aaa2, rw_k_k, rw_k_a, rw_r_k, rw_lnx_w, rw_lnx_b, w_mem_kv, w_proj_a, w_proj_b, w_proj_c, w_out, final_norm_g):
    bsz, seq, d = x.shape
    mem_len = mem.shape[1]
    depth = norm_g.shape[0]
    x2d = x.reshape(bsz * seq, d)
    mem2d = mem.reshape(bsz * mem_len, d)
    bias_tiles = _bias_tiles(rel_bias)
    for l in range(depth):
        rw = (rw_mu_r[l], rw_mu_k[l], rw_mu_v[l], rw_mu_w[l], rw_mu_a[l], rw_w0[l], rw_w_decay2[l], rw_a0[l], rw_w_aaa2[l],
              rw_k_k[l], rw_k_a[l], rw_r_k[l], rw_lnx_w[l], rw_lnx_b[l])
        x2d = _layer(x2d, mem2d, bsz, seq, mem_len, bias_tiles, rel_bias, norm_g[l], mem_norm_g[l], w_in[l], rw, w_mem_kv[l],
                     w_proj_a[l], w_proj_b[l], w_proj_c[l], w_out[l], final_norm_g if l == depth - 1 else None)
    return x2d.reshape(bsz, seq, d)
```

```python
import functools
import math
from typing import NamedTuple

import jax
import jax.numpy as jnp
from jax import lax
from jax.experimental import pallas as pl
from jax.experimental.pallas import tpu as pltpu

F32 = jnp.float32
BF16 = jnp.bfloat16
LANES = 128
BF16_SUBLANES = 16

RMS_EPS = 1e-6

A_HEADS = 12
A_HEAD_DIM = 128
A_WIDTH = A_HEADS * A_HEAD_DIM
MOBA_BLOCK = 256
MOBA_TOPK = 3
MOBA_HEADS_PER_STEP = 6
MOBA_FAR_GROUP = 4
MOBA_Q_SCALE = A_HEAD_DIM ** -0.5 * math.log2(math.e)
REL_BUCKETS = 32
REL_MAX_DIST = 128

B_HEAD_DIM = 64
B_WIDTH = 1536
LORA = 96
LORA_PAD = LANES
LNX_EPS = 64e-5
WKV_CHUNK = 64
WKV_GROUP = 256
WKV_HEADS_PER_GROUP = WKV_GROUP // B_HEAD_DIM
WKV_GROUPS_PER_STEP = 6
WKV_CHUNKS_PER_STEP = 4

C_HEADS = 4
C_HEAD_DIM = 256
C_WIDTH = C_HEADS * C_HEAD_DIM

N_BRANCHES = 3

VMEM_LIMIT = 48 * 1024 * 1024
VMEM_LIMIT_PROJ = 58 * 1024 * 1024

NN = (((1,), (0,)), ((), ()))
NT = (((1,), (1,)), ((), ()))
TN = (((0,), (0,)), ((), ()))

MASKED = -1e30
LOG2E = math.log2(math.e)


def _params(n_axes, vmem_limit=VMEM_LIMIT):
    return pltpu.CompilerParams(dimension_semantics=("arbitrary",) * n_axes, vmem_limit_bytes=vmem_limit)


def _sigmoid(x):
    return 0.5 * jnp.tanh(0.5 * x) + 0.5


def _silu(z):
    return z * _sigmoid(z)


def _bf(x):
    return x.astype(BF16)


def _mm(a, b, dn):
    return lax.dot_general(a, b, dn, preferred_element_type=F32)


def _each(f, *lists):
    return [f(*args) for args in zip(*lists)]


def _norm_matmul_kernel(x_ref, g_ref, w_ref, o_ref):
    x = x_ref[...]
    h = (x * lax.rsqrt(jnp.mean(x * x, axis=-1, keepdims=True) + RMS_EPS) * g_ref[...]).astype(BF16)
    o_ref[...] = jnp.dot(h, w_ref[...].astype(BF16), preferred_element_type=F32).astype(o_ref.dtype)


def _norm_matmul(x2d, g, w, out_dtype, tm, tn, name):
    m, k = x2d.shape
    n = w.shape[1]
    return pl.pallas_call(
        _norm_matmul_kernel,
        grid=(n // tn, m // tm),
        in_specs=[pl.BlockSpec((tm, k), lambda j, i: (i, 0)), pl.BlockSpec((1, k), lambda j, i: (0, 0)),
                  pl.BlockSpec((k, tn), lambda j, i: (0, j))],
        out_specs=pl.BlockSpec((tm, tn), lambda j, i: (i, j)),
        out_shape=jax.ShapeDtypeStruct((m, n), out_dtype),
        compiler_params=_params(2),
        name=name,
    )(x2d, g.reshape(1, k), w)


def _matmul_wt_kernel(a_ref, wt_ref, s_ref, o_ref, wb_ref, *carry_ref, seq_rows, sigmoid_from):
    i = pl.program_id(1)

    @pl.when(i == 0)
    def _():
        wb_ref[...] = wt_ref[...].astype(BF16)
        for ref in carry_ref:
            ref[...] = jnp.zeros_like(ref)

    x = lax.dot_general(a_ref[...], wb_ref[...], NT, preferred_element_type=F32)
    if sigmoid_from is not None:
        is_gate = pl.program_id(0) * x.shape[1] >= sigmoid_from

        @pl.when(is_gate)
        def _():
            o_ref[...] = _sigmoid(x * s_ref[...]).astype(o_ref.dtype)

        @pl.when(jnp.logical_not(is_gate))
        def _():
            o_ref[...] = (x * s_ref[...]).astype(o_ref.dtype)
    elif seq_rows is None:
        o_ref[...] = (x * s_ref[...]).astype(o_ref.dtype)
    else:
        tm = x.shape[0]
        last, = carry_ref
        first_row = lax.broadcasted_iota(jnp.int32, x.shape, 0) == 0
        carried = jnp.where((i * tm) % seq_rows == 0, 0.0, last[0:1, :])
        prev = jnp.where(first_row, carried, pltpu.roll(x, 1, 0))
        last[0:1, :] = x[tm - 1:tm, :]
        o_ref[...] = (x + (prev - x) * s_ref[...]).astype(o_ref.dtype)


def _matmul_wt(a, wt, row0, n, col_vec, out_dtype, tm, tn, name, seq_rows=None, sigmoid_from=None):
    m, k = a.shape
    scratch = [pltpu.VMEM((tn, k), BF16)]
    if seq_rows is not None:
        assert seq_rows % tm == 0
        scratch.append(pltpu.VMEM((8, tn), F32))
    return pl.pallas_call(
        functools.partial(_matmul_wt_kernel, seq_rows=seq_rows, sigmoid_from=sigmoid_from),
        grid=(n // tn, m // tm),
        in_specs=[pl.BlockSpec((tm, k), lambda j, i: (i, 0)),
                  pl.BlockSpec((pl.Element(tn), pl.Element(k)), lambda j, i: (pl.multiple_of(row0 + j * tn, 8), 0)),
                  pl.BlockSpec((1, tn), lambda j, i: (0, j))],
        out_specs=pl.BlockSpec((tm, tn), lambda j, i: (i, j)),
        out_shape=jax.ShapeDtypeStruct((m, n), out_dtype),
        scratch_shapes=scratch,
        compiler_params=_params(2, VMEM_LIMIT_PROJ),
        name=name,
    )(a, wt, col_vec)


def _norm_proj_kernel(x_ref, g_ref, wt_ref, mu_ref, h_ref, o_ref, wb_ref, last, *, seq_rows):
    i = pl.program_id(0)

    @pl.when(i == 0)
    def _():
        wb_ref[...] = wt_ref[...].astype(BF16)
        last[...] = jnp.zeros_like(last)

    x = x_ref[...]
    h = (x * lax.rsqrt(jnp.mean(x * x, axis=-1, keepdims=True) + RMS_EPS) * g_ref[...]).astype(BF16)
    h_ref[...] = h
    y = lax.dot_general(h, wb_ref[...], NT, preferred_element_type=F32)
    tm = y.shape[0]
    first_row = lax.broadcasted_iota(jnp.int32, y.shape, 0) == 0
    carried = jnp.where((i * tm) % seq_rows == 0, 0.0, last[0:1, :])
    prev = jnp.where(first_row, carried, pltpu.roll(y, 1, 0))
    last[0:1, :] = y[tm - 1:tm, :]
    o_ref[...] = y + (prev - y) * mu_ref[...]


def _norm_proj(x2d, g, wt, mu, tm, seq_rows):
    m, d = x2d.shape
    n = wt.shape[0]
    assert seq_rows % tm == 0
    return pl.pallas_call(
        functools.partial(_norm_proj_kernel, seq_rows=seq_rows),
        grid=(m // tm,),
        in_specs=[pl.BlockSpec((tm, d), lambda i: (i, 0)), pl.BlockSpec((1, d), lambda i: (0, 0)),
                  pl.BlockSpec((n, d), lambda i: (0, 0)), pl.BlockSpec((1, n), lambda i: (0, 0))],
        out_specs=[pl.BlockSpec((tm, d), lambda i: (i, 0)), pl.BlockSpec((tm, n), lambda i: (i, 0))],
        out_shape=[jax.ShapeDtypeStruct((m, d), BF16), jax.ShapeDtypeStruct((m, n), F32)],
        scratch_shapes=[pltpu.VMEM((n, d), BF16), pltpu.VMEM((8, n), F32)],
        compiler_params=_params(1),
        name="rmsnorm_in_proj_lora",
    )(x2d, g.reshape(1, d), wt, mu)


def _t5_bucket(dist):
    n = jnp.maximum(dist, 0)
    max_exact = REL_BUCKETS // 2
    nf = jnp.maximum(n, max_exact).astype(F32)
    large = max_exact + (jnp.log(nf / max_exact) / math.log(REL_MAX_DIST / max_exact) * (REL_BUCKETS - max_exact)).astype(jnp.int32)
    large = jnp.minimum(large, REL_BUCKETS - 1)
    return jnp.where(n < max_exact, n, large)


def _bias_kernel(rel_ref, bucket_ref, o_ref):
    h = pl.program_id(0)
    bucket = bucket_ref[...]
    acc = jnp.zeros(bucket.shape, F32)
    for b in range(REL_BUCKETS):
        acc = jnp.where(bucket == b, rel_ref[b, h], acc)
    qpos = lax.broadcasted_iota(jnp.int32, bucket.shape, 0) + MOBA_BLOCK
    kpos = lax.broadcasted_iota(jnp.int32, bucket.shape, 1)
    o_ref[0] = jnp.where(kpos <= qpos, acc * LOG2E, MASKED)


def _bias_tiles(rel_bias):
    blk = MOBA_BLOCK
    qpos = lax.broadcasted_iota(jnp.int32, (blk, 2 * blk), 0) + blk
    kpos = lax.broadcasted_iota(jnp.int32, (blk, 2 * blk), 1)
    bucket = _t5_bucket(qpos - kpos)
    return pl.pallas_call(
        _bias_kernel,
        grid=(A_HEADS,),
        in_specs=[pl.BlockSpec(memory_space=pltpu.SMEM), pl.BlockSpec((blk, 2 * blk), lambda h: (0, 0))],
        out_specs=pl.BlockSpec((1, blk, 2 * blk), lambda h: (h, 0, 0)),
        out_shape=jax.ShapeDtypeStruct((A_HEADS, blk, 2 * blk), F32),
        compiler_params=_params(1),
        name="t5_bias",
    )(rel_bias, bucket)


def _moba_kernel(rel_ref, q_ref, k_ref, v_ref, z_ref, bias_ref, o_ref, kmean_ref, m_ref, acc_ref):
    hb = pl.program_id(1)
    qi = pl.program_id(2)
    blk, hd = MOBA_BLOCK, A_HEAD_DIM
    n_heads = q_ref.shape[1] // hd
    nb = k_ref.shape[0] // blk
    nbp = kmean_ref.shape[1]

    @pl.when(qi == 0)
    def _():
        kmean_ref[...] = jnp.zeros_like(kmean_ref)
        for j in range(nb):
            mean_j = jnp.sum(k_ref[j * blk:(j + 1) * blk, :].astype(F32), axis=0, keepdims=True) * (1.0 / blk)
            for hh in range(n_heads):
                kmean_ref[hh, j:j + 1, :] = mean_j[:, hh * hd:(hh + 1) * hd]

    n_far = min(MOBA_FAR_GROUP, nb)
    never = LANES - 1
    blk_id = lax.broadcasted_iota(jnp.int32, (nbp, blk), 0)
    blk_f = blk_id.astype(F32)
    eye = jnp.where(lax.broadcasted_iota(jnp.int32, (nbp, LANES), 0) == lax.broadcasted_iota(jnp.int32, (nbp, LANES), 1), 1.0, 0.0).astype(BF16)
    j_prev = jnp.maximum(qi - 1, 0)
    own = pl.multiple_of(qi * blk, blk)
    prev = pl.multiple_of(j_prev * blk, blk)

    heads = [slice(hh * hd, (hh + 1) * hd) for hh in range(n_heads)]
    q = [q_ref[:, hl] for hl in heads]

    def gate_of(qh, hh):
        km = kmean_ref[hh]
        km_hi = km.astype(BF16)
        km_lo = (km - km_hi.astype(F32)).astype(BF16)
        return jnp.where(blk_id < qi, _mm(km_hi, qh, NT) + _mm(km_lo, qh, NT), -jnp.inf)

    g = _each(gate_of, q, range(n_heads))
    sel_t = [jnp.zeros((nbp, blk), F32) for _ in heads]
    for _ in range(MOBA_TOPK):
        gmax = _each(lambda x: jnp.max(x, axis=0, keepdims=True), g)
        first = _each(lambda x, mx: jnp.min(jnp.where(x == mx, blk_f, float(nbp)), axis=0, keepdims=True), g, gmax)
        pick = _each(lambda f, mx: (blk_f == f) & (mx > -jnp.inf), first, gmax)
        sel_t = _each(lambda p, s: jnp.where(p, 1.0, s), pick, sel_t)
        g = _each(lambda p, x: jnp.where(p, -jnp.inf, x), pick, g)
    sel = _each(lambda s: _mm(s.astype(BF16), eye, TN), sel_t)
    qm = _each(lambda qh, s: jnp.concatenate([qh, jnp.where(s > 0.0, 0.0, MASKED).astype(BF16)], axis=1), q, sel)

    lane = lax.broadcasted_iota(jnp.int32, (blk, LANES), 1)

    def masked_logits(keys, key_blocks):
        col = jnp.concatenate([jnp.where(lane == j, 1.0, 0.0).astype(BF16) for j in key_blocks], axis=0)
        return _each(lambda x, kk: _mm(x, jnp.concatenate([kk, col], axis=1), NT), qm, keys)

    def pv(p, values):
        ones = jnp.ones(values[0].shape, BF16)
        return _each(lambda x, v: jnp.dot(x.astype(BF16), jnp.concatenate([v, ones], axis=1), preferred_element_type=F32), p, values)

    cat0 = lambda ref, hl: jnp.concatenate([ref[pl.ds(prev, blk), hl], ref[pl.ds(own, blk), hl]], axis=0)
    raw = masked_logits([cat0(k_ref, hl) for hl in heads], [j_prev, -1])
    t = _each(lambda x, hh: x + bias_ref[hh], raw, range(n_heads))
    m = _each(lambda x: jnp.max(x, axis=-1, keepdims=True), t)
    p = _each(lambda x, mx: jnp.exp2(x - mx), t, m)
    acc = pv(p, [cat0(v_ref, hl) for hl in heads])
    for hh in range(n_heads):
        m_ref[hh] = jnp.broadcast_to(m[hh], (blk, LANES))
        acc_ref[hh] = acc[hh]

    bias_far = [rel_ref[REL_BUCKETS - 1, hb * n_heads + hh] * LOG2E for hh in range(n_heads)]

    def body(gi, carry):
        rows = pl.ds(pl.multiple_of(gi * (n_far * blk), n_far * blk), n_far * blk)
        blocks = [gi * n_far + j for j in range(n_far)]
        raw = masked_logits([k_ref[rows, hl] for hl in heads], [jnp.where(j < j_prev, j, never) for j in blocks])
        m = [m_ref[hh][:, :1] for hh in range(n_heads)]
        m_new = _each(lambda mx, x, b: jnp.maximum(mx, jnp.max(x, axis=-1, keepdims=True) + b), m, raw, bias_far)
        p = _each(lambda x, mn, b: jnp.exp2(x + (b - mn)), raw, m_new, bias_far)
        alpha = _each(lambda mx, mn: jnp.exp2(mx - mn), m, m_new)
        upd = pv(p, [v_ref[rows, hl] for hl in heads])
        for hh in range(n_heads):
            m_ref[hh] = jnp.broadcast_to(m_new[hh], (blk, LANES))
            acc_ref[hh] = alpha[hh] * acc_ref[hh] + upd[hh]
        return carry

    lax.fori_loop(0, (j_prev + n_far - 1) // n_far, body, 0)
    for hh, hl in enumerate(heads):
        y = acc_ref[hh, :, :hd] / acc_ref[hh, :, hd:]
        o_ref[:, hl] = (y * _silu(z_ref[:, hl].astype(F32))).astype(o_ref.dtype)


def _moba(p_a, rel_bias, bias_tiles, bsz, seq):
    blk = MOBA_BLOCK
    nq = seq // blk
    hw = MOBA_HEADS_PER_STEP * A_HEAD_DIM
    ns = A_WIDTH // hw
    nbp = -(-nq // BF16_SUBLANES) * BF16_SUBLANES
    assert nbp < LANES
    return pl.pallas_call(
        _moba_kernel,
        grid=(bsz, ns, nq),
        in_specs=[
            pl.BlockSpec(memory_space=pltpu.SMEM),
            pl.BlockSpec((blk, hw), lambda b, h, i: (b * nq + i, h)),
            pl.BlockSpec((seq, hw), lambda b, h, i: (b, ns + h)),
            pl.BlockSpec((seq, hw), lambda b, h, i: (b, 2 * ns + h)),
            pl.BlockSpec((blk, hw), lambda b, h, i: (b * nq + i, 3 * ns + h)),
            pl.BlockSpec((MOBA_HEADS_PER_STEP, blk, 2 * blk), lambda b, h, i: (h, 0, 0)),
        ],
        out_specs=pl.BlockSpec((blk, hw), lambda b, h, i: (b * nq + i, h)),
        out_shape=jax.ShapeDtypeStruct((bsz * seq, A_WIDTH), BF16),
        scratch_shapes=[pltpu.VMEM((MOBA_HEADS_PER_STEP, nbp, A_HEAD_DIM), F32), pltpu.VMEM((MOBA_HEADS_PER_STEP, blk, LANES), F32),
                        pltpu.VMEM((MOBA_HEADS_PER_STEP, blk, 2 * A_HEAD_DIM), F32)],
        compiler_params=_params(3),
        name="moba_attention",
    )(rel_bias, p_a, p_a, p_a, p_a, bias_tiles)


def _cross_kernel(q_ref, k_ref, v_ref, z_ref, o_ref):
    hd = C_HEAD_DIM
    heads = [slice(h * hd, (h + 1) * hd) for h in range(C_HEADS)]
    s = _each(lambda hl: _mm(q_ref[:, hl], k_ref[:, hl], NT) * (hd ** -0.5), heads)
    m = _each(lambda x: jnp.max(x, axis=-1, keepdims=True), s)
    p = _each(lambda x, mx: jnp.exp(x - mx), s, m)
    l = _each(lambda x: jnp.sum(x, axis=-1, keepdims=True), p)
    y = _each(lambda x, hl: jnp.dot(x.astype(BF16), v_ref[:, hl], preferred_element_type=F32), p, heads)
    for hl, yh, lh in zip(heads, y, l):
        o_ref[:, hl] = (yh / lh * _silu(z_ref[:, hl].astype(F32))).astype(o_ref.dtype)


def _cross(p_c, kv, bsz, seq, mem_len, tq):
    nt = seq // tq
    return pl.pallas_call(
        _cross_kernel,
        grid=(bsz, nt),
        in_specs=[
            pl.BlockSpec((tq, C_WIDTH), lambda b, i: (b * nt + i, 0)),
            pl.BlockSpec((mem_len, C_WIDTH), lambda b, i: (b, 0)),
            pl.BlockSpec((mem_len, C_WIDTH), lambda b, i: (b, 1)),
            pl.BlockSpec((tq, C_WIDTH), lambda b, i: (b * nt + i, 1)),
        ],
        out_specs=pl.BlockSpec((tq, C_WIDTH), lambda b, i: (b * nt + i, 0)),
        out_shape=jax.ShapeDtypeStruct((bsz * seq, C_WIDTH), BF16),
        compiler_params=_params(2),
        name="memory_attention",
    )(p_c, kv, kv, p_c)


class _WkvConsts:
    def __init__(self):
        c, g, hd = WKV_CHUNK, WKV_GROUP, B_HEAD_DIM
        row = lax.broadcasted_iota(jnp.int32, (g, g), 0)
        col = lax.broadcasted_iota(jnp.int32, (g, g), 1)
        self.same_head = jnp.where((row // hd) == (col // hd), 1.0, 0.0)
        self.ones_bd = self.same_head.astype(BF16)
        t = lax.broadcasted_iota(jnp.int32, (c, g), 0)
        s = lax.broadcasted_iota(jnp.int32, (c, g), 1) % hd
        self.strict = s < t
        self.incl = s <= t
        self.strict_incl = jnp.concatenate([self.strict, self.incl], axis=0)
        self.eye = jnp.where(s == t, 1.0, 0.0).astype(F32)
        tr = lax.broadcasted_iota(jnp.int32, (c, c), 0)
        tc = lax.broadcasted_iota(jnp.int32, (c, c), 1)
        self.lower = jnp.where(tc <= tr, 1.0, 0.0).astype(BF16)

    def bd(self, p):
        return jnp.concatenate([p] * WKV_HEADS_PER_GROUP, axis=0) * self.ones_bd


def _wkv_prepare(rm, kmod, vm, avec, bvec, logdec, k):
    c = WKV_CHUNK
    cat = lambda u, w: jnp.concatenate([u, w], axis=0)
    cl = _each(lambda x: jnp.dot(k.lower, _bf(x), preferred_element_type=F32), logdec)
    cl_last = _each(lambda x: x[c - 1:c, :], cl)
    rt = _each(lambda r, x: _bf(r * jnp.exp(x)), rm, cl)
    at = _each(lambda a, x, ld: _bf(a * jnp.exp(x - ld)), avec, cl, logdec)
    e_neg = _each(lambda x: jnp.exp(-x), cl)
    bt = _each(lambda b, e: k.bd(_bf(b * e)), bvec, e_neg)
    kt = _each(lambda kk, e: k.bd(_bf(kk * e)), kmod, e_neg)
    decay = _each(jnp.exp, cl_last)
    e_rem = _each(lambda d, e: d * e, decay, e_neg)
    bkh = _each(lambda b, kk, e: cat(_bf(b * e), _bf(kk * e)), bvec, kmod, e_rem)
    vb = _each(_bf, vm)

    lhs = _each(cat, at, rt)
    ab = _each(lambda x, y: _mm(x, y, NT), lhs, bt)
    ak = _each(lambda x, y: _mm(x, y, NT), lhs, kt)
    a_ab = _each(lambda x: jnp.where(k.strict, x[:c], 0.0), ab)
    a_rb = _each(lambda x: _bf(jnp.where(k.incl, x[c:], 0.0)), ab)
    a_k = _each(lambda x: _bf(jnp.where(k.strict_incl, x, 0.0)), ak)

    inv = _each(lambda x: k.eye + x, a_ab)
    pw = _each(_bf, a_ab)
    pw = _each(lambda p: _bf(_mm(p, k.bd(p), NN)), pw)
    for _ in range(int(math.log2(c)) - 2):
        tp = _each(lambda i, p: _mm(cat(_bf(i), p), k.bd(p), NN), inv, pw)
        inv = _each(lambda i, t: i + t[:c], inv, tp)
        pw = _each(lambda t: _bf(t[c:]), tp)
    inv = _each(lambda i, p: _bf(i + _mm(_bf(i), k.bd(p), NN)), inv, pw)

    av = _each(lambda x, v: _mm(x, k.bd(v), NN), a_k, vb)
    w = _each(lambda i, a: _mm(i, k.bd(a), NN), inv, at)
    u0 = _each(lambda i, x: _mm(i, k.bd(_bf(x[:c])), NN), inv, av)
    wr = _each(lambda ww, r: cat(_bf(ww), r), w, rt)
    y0 = _each(lambda x: x[c:], av)
    return list(zip(wr, u0, y0, a_rb, vb, bkh, decay))


def _wkv_apply(prep, state, k):
    c = WKV_CHUNK
    wr, u0, y0, a_rb, vb, bkh, decay = (list(x) for x in zip(*prep))
    uy = _each(lambda x, s: _mm(x, _bf(s), NT), wr, state)
    ub = _each(lambda x, y: _bf(x[:c] + y), uy, u0)
    y = _each(lambda x, a, uu, z: x[c:] + _mm(a, k.bd(uu), NN) + z, uy, a_rb, ub, y0)
    upd = _each(lambda uu, v, bk: _mm(jnp.concatenate([uu, v], axis=0), bk, TN), ub, vb, bkh)
    new_state = _each(lambda s, dc, d: s * dc + d * k.same_head, state, decay, upd)
    return y, new_state


def _rwkv_kernel(r_ref, k_ref, v_ref, lwla_ref, z_ref, w0_ref, a0_ref, kk_ref, ka_ref, rk_ref, lnw_ref, lnb_ref,
                 wd_ref, wa_ref, o_ref, state_ref):
    c, g = WKV_CHUNK, WKV_GROUP
    n_groups = r_ref.shape[1] // g
    n_chunks = WKV_CHUNKS_PER_STEP

    @pl.when(pl.program_id(2) == 0)
    def _():
        state_ref[...] = jnp.zeros_like(state_ref)

    k = _WkvConsts()
    wd = _bf(wd_ref[...])
    wa = _bf(wa_ref[...])
    inv_hd = 1.0 / B_HEAD_DIM
    groups = [slice(gi * g, (gi + 1) * g) for gi in range(n_groups)]
    split = lambda x: [x[ch * c:(ch + 1) * c, ln] for ch in range(n_chunks) for ln in groups]

    def segsums(xs):
        s = jnp.dot(jnp.concatenate(_each(_bf, xs), axis=0), k.ones_bd, preferred_element_type=F32)
        return [s[i * c:(i + 1) * c] for i in range(len(xs))]

    def step(ci, carry):
        rows = pl.ds(pl.multiple_of(ci * (n_chunks * c), n_chunks * c), n_chunks * c)
        rm, km, vm = r_ref[rows, :], k_ref[rows, :], v_ref[rows, :]
        lwla = lwla_ref[rows, :]
        lw = _bf(jnp.tanh(lwla[:, :LORA_PAD]))
        la = _bf(lwla[:, LORA_PAD:])
        logdec = -math.exp(-0.5) * _sigmoid(w0_ref[...] + jnp.dot(lw, wd, preferred_element_type=F32))
        a_lr = _sigmoid(a0_ref[...] + jnp.dot(la, wa, preferred_element_type=F32))
        kmod = km * (1.0 + (a_lr - 1.0) * ka_ref[...])
        kk = split(km * kk_ref[...])
        n = len(kk)
        sums = segsums(_each(lambda x: x * x, kk) + split(rm * kmod * rk_ref[...]))
        kk = _each(lambda x, ss: x * jnp.minimum(lax.rsqrt(ss), 1e12), kk, sums[:n])
        bonus = _each(lambda s, v: s * v, sums[n:], split(vm))
        prep = _wkv_prepare(split(rm), split(kmod), split(vm), _each(lambda x: -x, kk),
                            _each(lambda x, a: x * a, kk, split(a_lr)), split(logdec), k)
        state = [state_ref[gi] for gi in range(n_groups)]
        y = []
        for ch in range(n_chunks):
            y_ch, state = _wkv_apply(prep[ch * n_groups:(ch + 1) * n_groups], state, k)
            y += y_ch
        for gi in range(n_groups):
            state_ref[gi] = state[gi]
        yc = _each(lambda x, s: x - s * inv_hd, y, segsums(y))
        var = _each(lambda s: s * inv_hd, segsums(_each(lambda x: x * x, yc)))
        gate = _silu(z_ref[rows, :].astype(F32))
        for ch in range(n_chunks):
            out_rows = pl.ds(pl.multiple_of(ci * (n_chunks * c) + ch * c, c), c)
            for gi, ln in enumerate(groups):
                i = ch * n_groups + gi
                yn = yc[i] * lax.rsqrt(var[i] + LNX_EPS) * lnw_ref[:, ln] + lnb_ref[:, ln]
                o_ref[out_rows, ln] = ((yn + bonus[i]) * gate[ch * c:(ch + 1) * c, ln]).astype(o_ref.dtype)
        return carry

    lax.fori_loop(0, r_ref.shape[0] // (n_chunks * c), step, 0)


def _rwkv(p_b, p_lora, p_zb, vecs, wd2, wa2, bsz, seq, tt):
    g = WKV_GROUP
    gw = WKV_GROUPS_PER_STEP * g
    ns = B_WIDTH // gw
    nt = seq // tt
    row = lambda b, gi, t: b * nt + t
    vec_spec = pl.BlockSpec((1, gw), lambda b, gi, t: (0, gi))
    lora_w_spec = pl.BlockSpec((LORA_PAD, gw), lambda b, gi, t: (0, gi))
    return pl.pallas_call(
        _rwkv_kernel,
        grid=(bsz, ns, nt),
        in_specs=[
            pl.BlockSpec((tt, gw), lambda b, gi, t: (row(b, gi, t), gi)),
            pl.BlockSpec((tt, gw), lambda b, gi, t: (row(b, gi, t), ns + gi)),
            pl.BlockSpec((tt, gw), lambda b, gi, t: (row(b, gi, t), 2 * ns + gi)),
            pl.BlockSpec((tt, 2 * LORA_PAD), lambda b, gi, t: (row(b, gi, t), 0)),
            pl.BlockSpec((tt, gw), lambda b, gi, t: (row(b, gi, t), gi)),
        ] + [vec_spec] * len(vecs) + [lora_w_spec, lora_w_spec],
        out_specs=pl.BlockSpec((tt, gw), lambda b, gi, t: (row(b, gi, t), gi)),
        out_shape=jax.ShapeDtypeStruct((bsz * seq, B_WIDTH), BF16),
        scratch_shapes=[pltpu.VMEM((WKV_GROUPS_PER_STEP, g, g), F32)],
        compiler_params=_params(3),
        name="rwkv7_time_mix",
    )(p_b, p_b, p_b, p_lora, p_zb, *vecs, wd2, wa2)


def _merge_kernel(ya_ref, yb_ref, yc_ref, wa_ref, wb_ref, wc_ref, ga_ref, gb_ref, gc_ref, o_ref):
    def branch(y_ref, w_ref, g_ref):
        return g_ref[...].astype(F32) * jnp.dot(y_ref[...], w_ref[...], preferred_element_type=F32)

    o_ref[...] = (branch(ya_ref, wa_ref, ga_ref) + branch(yb_ref, wb_ref, gb_ref) + branch(yc_ref, wc_ref, gc_ref)).astype(o_ref.dtype)


def _merge(ya, yb, yc, wa, wb, wc, p_c, gate_col0, d_model, tm, tn):
    m = ya.shape[0]
    nj = d_model // tn
    j0 = gate_col0 // tn
    y_spec = lambda width: pl.BlockSpec((tm, width), lambda j, i: (i, 0))
    w_spec = lambda width: pl.BlockSpec((width, tn), lambda j, i: (0, j))
    g_spec = lambda br: pl.BlockSpec((tm, tn), lambda j, i: (i, j0 + br * nj + j))
    return pl.pallas_call(
        _merge_kernel,
        grid=(nj, m // tm),
        in_specs=[y_spec(A_WIDTH), y_spec(B_WIDTH), y_spec(C_WIDTH), w_spec(A_WIDTH), w_spec(B_WIDTH), w_spec(C_WIDTH),
                  g_spec(0), g_spec(1), g_spec(2)],
        out_specs=pl.BlockSpec((tm, tn), lambda j, i: (i, j)),
        out_shape=jax.ShapeDtypeStruct((m, d_model), BF16),
        compiler_params=_params(2, VMEM_LIMIT_PROJ),
        name="gated_merge",
    )(ya, yb, yc, wa, wb, wc, p_c, p_c, p_c)


def _out_kernel(m_ref, w_ref, x_ref, g_ref, o_ref, wb_ref, *, final_norm):
    @pl.when(pl.program_id(0) == 0)
    def _():
        wb_ref[...] = w_ref[...].astype(BF16)

    y = x_ref[...] + jnp.dot(m_ref[...], wb_ref[...], preferred_element_type=F32)
    if final_norm:
        y = y * lax.rsqrt(jnp.mean(y * y, axis=-1, keepdims=True) + RMS_EPS) * g_ref[...]
    o_ref[...] = y


def _out_proj(merged, w_out, x2d, g, final_norm, tm):
    m, d = x2d.shape
    return pl.pallas_call(
        functools.partial(_out_kernel, final_norm=final_norm),
        grid=(m // tm,),
        in_specs=[pl.BlockSpec((tm, d), lambda i: (i, 0)),
                  pl.BlockSpec((d, d), lambda i: (0, 0), pipeline_mode=pl.Buffered(1)),
                  pl.BlockSpec((tm, d), lambda i: (i, 0)), pl.BlockSpec((1, d), lambda i: (0, 0))],
        out_specs=pl.BlockSpec((tm, d), lambda i: (i, 0)),
        out_shape=jax.ShapeDtypeStruct((m, d), F32),
        scratch_shapes=[pltpu.VMEM((d, d), BF16)],
        compiler_params=_params(1, VMEM_LIMIT_PROJ),
        name="out_proj",
    )(merged, w_out, x2d, g.reshape(1, d))


def _pick_tile(n, candidates):
    for t in candidates:
        if n % t == 0:
            return t
    raise ValueError(f"no tile for extent {n}")


class _Tiles(NamedTuple):
    norm_rows: int
    proj_rows: int
    proj_cols: int
    proj_cols_b: int
    mem_rows: int
    cross_rows: int
    wkv_rows: int
    merge_rows: int
    merge_cols: int
    out_rows: int


def _tiles(m, seq, mem_rows):
    return _Tiles(
        norm_rows=_pick_tile(m, (1024, 512, 256)),
        proj_rows=_pick_tile(seq, (2048, 1024, 512, 256)),
        proj_cols=1024,
        proj_cols_b=768,
        mem_rows=_pick_tile(mem_rows, (1024, 512, 256)),
        cross_rows=_pick_tile(seq, (1024, 512, 256)),
        wkv_rows=_pick_tile(seq, (512, 256, 128, 64)),
        merge_rows=_pick_tile(m, (1024, 512, 256)),
        merge_cols=1024,
        out_rows=_pick_tile(m, (512, 256)),
    )


def _pad_cols(w, width):
    return jnp.pad(w, ((0, 0), (0, width - w.shape[1])))


def _layer(x2d, mem2d, bsz, seq, mem_len, bias_tiles, rel_bias, norm_g, mem_norm_g, w_in, rw, w_mem_kv, w_proj_a, w_proj_b,
           w_proj_c, w_out, final_g):
    m, d = x2d.shape
    t = _tiles(m, seq, mem2d.shape[0])

    c_z_a = 3 * A_WIDTH
    c_rkv_b = c_z_a + A_WIDTH
    c_z_b = c_rkv_b + 3 * B_WIDTH
    c_lw = c_z_b + B_WIDTH
    c_la = c_lw + LORA
    c_q_c = c_la + LORA

    ones = lambda n: jnp.ones((1, n), F32)
    wt = w_in.T
    (mu_r, mu_k, mu_v, mu_w, mu_a, w0, w_decay2, a0, w_aaa2, k_k, k_a, r_k, lnx_w, lnx_b) = rw
    vec = lambda v: v.reshape(1, -1)
    pad_rows = lambda w: jnp.pad(w, ((0, LORA_PAD - LORA), (0, 0)))
    wt_lora = jnp.concatenate([pad_rows(wt[c_lw:c_la]), pad_rows(wt[c_la:c_q_c])], axis=0)
    mu_wa = jnp.concatenate([_pad_cols(vec(mu_w), LORA_PAD), _pad_cols(vec(mu_a), LORA_PAD)], axis=1)
    h, p_lora = _norm_proj(x2d, norm_g, wt_lora, mu_wa, t.norm_rows, seq)
    scale_a = jnp.concatenate([jnp.full((1, A_WIDTH), MOBA_Q_SCALE, F32), ones(3 * A_WIDTH)], axis=1)
    p_a = _matmul_wt(h, wt, 0, 4 * A_WIDTH, scale_a, BF16, t.proj_rows, t.proj_cols, "in_proj_a")
    p_b = _matmul_wt(h, wt, c_rkv_b, 3 * B_WIDTH, jnp.concatenate([vec(mu_r), vec(mu_k), vec(mu_v)], axis=1), F32,
                     t.proj_rows, t.proj_cols_b, "in_proj_b", seq_rows=seq)
    p_zb = _matmul_wt(h, wt, c_z_b, B_WIDTH, ones(B_WIDTH), BF16, t.proj_rows, t.proj_cols_b, "in_proj_zb")
    n_c = 2 * C_WIDTH + N_BRANCHES * d
    assert (2 * C_WIDTH) % t.proj_cols == 0
    p_c = _matmul_wt(h, wt, c_q_c, n_c, ones(n_c), BF16, t.proj_rows, t.proj_cols, "in_proj_c", sigmoid_from=2 * C_WIDTH)

    kv = _norm_matmul(mem2d, mem_norm_g, w_mem_kv, BF16, t.mem_rows, t.proj_cols, "mem_kv")

    ya = _moba(p_a, rel_bias, bias_tiles, bsz, seq)
    yc = _cross(p_c, kv, bsz, seq, mem_len, t.cross_rows)

    vecs = (vec(w0), vec(a0), vec(k_k), vec(k_a), vec(r_k), vec(lnx_w), vec(lnx_b))
    yb = _rwkv(p_b, p_lora, p_zb, vecs, pad_rows(w_decay2), pad_rows(w_aaa2), bsz, seq, t.wkv_rows)

    merged = _merge(ya, yb, yc, w_proj_a.astype(BF16), w_proj_b.astype(BF16), w_proj_c.astype(BF16), p_c, 2 * C_WIDTH, d,
                    t.merge_rows, t.merge_cols)
    g = final_g if final_g is not None else jnp.ones((d,), F32)
    return _out_proj(merged, w_out, x2d, g, final_g is not None, t.out_rows)


def kernel(x, mem, rel_bias, norm_g, mem_norm_g, w_in, rw_mu_r, rw_mu_k, rw_mu_v, rw_mu_w, rw_mu_a, rw_w0, rw_w_decay2, rw_a0, rw_w_aaa2, rw_k_k, rw_k_a, rw_r_k, rw_lnx_w, rw_lnx_b, w_mem_kv, w_proj_a, w_proj_b, w_proj_c, w_out, final_norm_g):
    bsz, seq, d = x.shape
    mem_len = mem.shape[1]
    depth = norm_g.shape[0]
    x2d = x.reshape(bsz * seq, d)
    mem2d = mem.reshape(bsz * mem_len, d)
    bias_tiles = _bias_tiles(rel_bias)
    for l in range(depth):
        rw = (rw_mu_r[l], rw_mu_k[l], rw_mu_v[l], rw_mu_w[l], rw_mu_a[l], rw_w0[l], rw_w_decay2[l], rw_a0[l], rw_w_aaa2[l],
              rw_k_k[l], rw_k_a[l], rw_r_k[l], rw_lnx_w[l], rw_lnx_b[l])
        x2d = _layer(x2d, mem2d, bsz, seq, mem_len, bias_tiles, rel_bias, norm_g[l], mem_norm_g[l], w_in[l], rw, w_mem_kv[l],
                     w_proj_a[l], w_proj_b[l], w_proj_c[l], w_out[l], final_norm_g if l == depth - 1 else None)
    return x2d.reshape(bsz, seq, d)
```
